```python
import math
import jax, jax.numpy as jnp
from jax import lax
import numpy as np

D_MODEL = 2048
BATCH = 4
SEQ = 2048
DEPTH = 2

GRID_W = 64
CTX_LEN = 256
N_MIXERS = 2

RET_HEADS = 8
RET_DK = D_MODEL // RET_HEADS
RET_DV = 2 * D_MODEL // RET_HEADS
RET_CHUNK = 128
RET_IN = 2 * RET_HEADS * RET_DK + 2 * RET_HEADS * RET_DV
ROPE_BASE = 10000.0

GDN_DK = 128
GDN_DV = 128
GDN_QK_HEADS = D_MODEL // GDN_DK
GDN_V_HEADS = 2 * GDN_QK_HEADS
GDN_CONV = 5
GDN_CHUNK = 64
GDN_QKV = 2 * GDN_QK_HEADS * GDN_DK + GDN_V_HEADS * GDN_DV
GDN_IN = GDN_QKV + GDN_V_HEADS * GDN_DV + 4 * GDN_V_HEADS

D_FF = ((8 * D_MODEL // 3 + 255) // 256) * 256

DEEPNORM_ALPHA = (2 * DEPTH) ** 0.25
DEEPNORM_BETA = (8 * DEPTH) ** -0.25
LN_EPS = 1e-5
N_RET_LAYERS = (DEPTH + 1) // 2
N_GDN_LAYERS = DEPTH // 2

kernel_name = 'hybrid_retention_gdn_dit'


def _split_cols(t, widths):
    cuts = [int(s) for s in np.cumsum(widths)[:-1]]
    return jnp.split(t, cuts, axis=-1)


def _layer_norm(x, g, b):
    xf = x.astype(jnp.float32)
    mu = xf.mean(-1, keepdims=True)
    var = jnp.square(xf - mu).mean(-1, keepdims=True)
    return ((xf - mu) * lax.rsqrt(var + LN_EPS)).astype(x.dtype) * g + b


def _seg_flip(t, n_ctx):
    return jnp.concatenate([jnp.flip(t[:, :n_ctx], 1), jnp.flip(t[:, n_ctx:], 1)], axis=1)


def _axial_rope(rows, dim):
    row = jnp.repeat(jnp.arange(rows, dtype=jnp.float32), GRID_W)
    col = jnp.tile(jnp.arange(GRID_W, dtype=jnp.float32), rows)
    n_freq = dim // 4
    inv_freq = ROPE_BASE ** (-jnp.arange(n_freq, dtype=jnp.float32) / n_freq)
    ang = jnp.concatenate([row[:, None] * inv_freq, col[:, None] * inv_freq], axis=-1)
    return jnp.cos(ang), jnp.sin(ang)


def _apply_rope(x, cos, sin):
    half = x.shape[-1] // 2
    x1, x2 = x[..., :half], x[..., half:]
    cs = cos[None, :, None, :].astype(x.dtype)
    sn = sin[None, :, None, :].astype(x.dtype)
    return jnp.concatenate([x1 * cs - x2 * sn, x1 * sn + x2 * cs], axis=-1)


def _to_chunks(t, chunk):
    b, l, h, d = t.shape
    return t.reshape(b, l // chunk, chunk, h, d).transpose(1, 0, 3, 2, 4)


def _from_chunks(t):
    n, b, h, c, d = t.shape
    return t.transpose(1, 0, 3, 2, 4).reshape(b, n * c, h, d)


def _retention_scan(q, k, v, log_gamma, strict):
    bsz, _, h, dk = q.shape
    dv = v.shape[-1]
    C = RET_CHUNK
    qc = _to_chunks(q.astype(jnp.float32), C)
    kc = _to_chunks(k.astype(jnp.float32), C)
    vc = _to_chunks(v.astype(jnp.float32), C)
    idx = jnp.arange(C, dtype=jnp.float32)
    diff = idx[:, None] - idx[None, :]
    mask = diff > 0 if strict else diff >= 0
    lg = log_gamma[:, None, None]
    intra = jnp.where(mask, jnp.exp(lg * jnp.where(mask, diff, 0.0)), 0.0)
    q_decay = jnp.exp(log_gamma[:, None] * (idx + 1.0))[..., None]
    k_decay = jnp.exp(log_gamma[:, None] * (C - 1.0 - idx))[..., None]
    chunk_decay = jnp.exp(log_gamma * C)[:, None, None]

    def step(state, inp):
        qi, ki, vi = inp
        s = jnp.einsum('bhid,bhjd->bhij', qi, ki) * intra
        o = jnp.einsum('bhij,bhjv->bhiv', s, vi) + jnp.einsum('bhid,bhdv->bhiv', qi * q_decay, state)
        state = chunk_decay * state + jnp.einsum('bhjd,bhjv->bhdv', ki * k_decay, vi)
        return state, o

    state0 = jnp.zeros((bsz, h, dk, dv), jnp.float32)
    _, o = lax.scan(step, state0, (qc, kc, vc))
    return _from_chunks(o)


def _gdn_scan(q, k, v, beta, log_alpha):
    bsz, l, h, dk = q.shape
    dv = v.shape[-1]
    C = GDN_CHUNK
    n = l // C
    qc, kc, vc = _to_chunks(q, C), _to_chunks(k, C), _to_chunks(v, C)
    bc = beta.reshape(bsz, n, C, h).transpose(1, 0, 3, 2)
    gc = jnp.cumsum(log_alpha.reshape(bsz, n, C, h).transpose(1, 0, 3, 2), axis=-1)
    idx = jnp.arange(C)
    incl = idx[:, None] >= idx[None, :]
    strict = idx[:, None] > idx[None, :]
    decay = jnp.exp(jnp.where(incl, gc[..., :, None] - gc[..., None, :], -jnp.inf))
    kk = jnp.einsum('nbhid,nbhjd->nbhij', kc, kc)
    a_mat = jnp.where(strict, bc[..., :, None] * kk * decay, 0.0)
    eye = jnp.eye(C, dtype=jnp.float32)
    rhs = jnp.concatenate([vc * bc[..., None], kc * (bc * jnp.exp(gc))[..., None]], axis=-1)
    sol = lax.linalg.triangular_solve(eye + a_mat, rhs, left_side=True, lower=True, unit_diagonal=True)
    u0, w = sol[..., :dv], sol[..., dv:]
    qk = jnp.einsum('nbhid,nbhjd->nbhij', qc, kc) * decay
    q_dec = qc * jnp.exp(gc)[..., None]
    k_dec = kc * jnp.exp(gc[..., -1:] - gc)[..., None]
    chunk_dec = jnp.exp(gc[..., -1])[..., None, None]

    def step(m, inp):
        u0_i, w_i, qk_i, qd_i, kd_i, cd_i = inp
        u = u0_i - jnp.einsum('bhcd,bhdv->bhcv', w_i, m)
        o = jnp.einsum('bhcd,bhdv->bhcv', qd_i, m) + jnp.einsum('bhij,bhjv->bhiv', qk_i, u)
        m = cd_i * m + jnp.einsum('bhcd,bhcv->bhdv', kd_i, u)
        return m, o

    m0 = jnp.zeros((bsz, h, dk, dv), jnp.float32)
    _, o = lax.scan(step, m0, (u0, w, qk, q_dec, k_dec, chunk_dec))
    return _from_chunks(o)


def _centred_conv_silu(x, w):
    pad = GDN_CONV // 2
    y = lax.conv_general_dilated(x, w[:, None, :].astype(x.dtype), window_strides=(1,), padding=[(pad, pad)],
                                 dimension_numbers=('NWC', 'WIO', 'NWC'), feature_group_count=x.shape[-1])
    return jax.nn.silu(y)


def _l2norm(t):
    tf = t.astype(jnp.float32)
    return tf * lax.rsqrt(jnp.sum(tf * tf, axis=-1, keepdims=True) + 1e-6)


def retention_mixer(u_ctx, u_lat, w_in, decay_raw, w_out, cos, sin, need_ctx):
    n_ctx = u_ctx.shape[1]
    u = jnp.concatenate([u_ctx, u_lat], axis=1)
    bsz, n_tot, _ = u.shape
    q, k, v, g = _split_cols(u @ w_in, [RET_HEADS * RET_DK, RET_HEADS * RET_DK,
                                        RET_HEADS * RET_DV, RET_HEADS * RET_DV])
    q = q.reshape(bsz, n_tot, RET_HEADS, RET_DK)
    k = k.reshape(bsz, n_tot, RET_HEADS, RET_DK)
    v = v.reshape(bsz, n_tot, RET_HEADS, RET_DV)
    q = jnp.concatenate([q[:, :n_ctx], _apply_rope(q[:, n_ctx:], cos, sin)], axis=1) * RET_DK ** -0.5
    k = jnp.concatenate([k[:, :n_ctx], _apply_rope(k[:, n_ctx:], cos, sin)], axis=1)
    log_gamma = -jnp.exp(decay_raw.astype(jnp.float32))
    o_f = _retention_scan(q, k, v, log_gamma[0], strict=False)
    o_b = _seg_flip(_retention_scan(_seg_flip(q, n_ctx), _seg_flip(k, n_ctx), _seg_flip(v, n_ctx),
                                    log_gamma[1], strict=True), n_ctx)
    o = o_f + o_b
    if not need_ctx:
        o, g = o[:, n_ctx:], g[:, n_ctx:]
    mu = o.mean(-1, keepdims=True)
    var = jnp.square(o - mu).mean(-1, keepdims=True)
    o = ((o - mu) * lax.rsqrt(var + LN_EPS)).astype(u.dtype)
    o = o.reshape(bsz, -1, RET_HEADS * RET_DV) * jax.nn.silu(g)
    y = o @ w_out
    if need_ctx:
        return y[:, :n_ctx], y[:, n_ctx:]
    return None, y


def gated_deltanet_mixer(u_ctx, u_lat, w_in, conv_w, a_log, dt_bias, norm_w, w_out, need_ctx):
    n_ctx = u_ctx.shape[1]
    u = jnp.concatenate([u_ctx, u_lat], axis=1)
    bsz, n_tot, _ = u.shape
    qkv, z, b, a = _split_cols(u @ w_in, [GDN_QKV, GDN_V_HEADS * GDN_DV, 2 * GDN_V_HEADS, 2 * GDN_V_HEADS])
    qkv = jnp.concatenate([_centred_conv_silu(qkv[:, :n_ctx], conv_w),
                           _centred_conv_silu(qkv[:, n_ctx:], conv_w)], axis=1)
    q, k, v = _split_cols(qkv, [GDN_QK_HEADS * GDN_DK, GDN_QK_HEADS * GDN_DK, GDN_V_HEADS * GDN_DV])
    rep = GDN_V_HEADS // GDN_QK_HEADS
    q = jnp.repeat(_l2norm(q.reshape(bsz, n_tot, GDN_QK_HEADS, GDN_DK)), rep, axis=2) * GDN_DK ** -0.5
    k = jnp.repeat(_l2norm(k.reshape(bsz, n_tot, GDN_QK_HEADS, GDN_DK)), rep, axis=2)
    v = v.reshape(bsz, n_tot, GDN_V_HEADS, GDN_DV).astype(jnp.float32)
    beta = jax.nn.sigmoid(b.astype(jnp.float32)).reshape(bsz, n_tot, 2, GDN_V_HEADS)
    log_alpha = -jnp.exp(a_log.astype(jnp.float32)) * jax.nn.softplus(
        a.astype(jnp.float32).reshape(bsz, n_tot, 2, GDN_V_HEADS) + dt_bias.astype(jnp.float32))
    o_f = _gdn_scan(q, k, v, beta[:, :, 0], log_alpha[:, :, 0])
    o_b = _seg_flip(_gdn_scan(_seg_flip(q, n_ctx), _seg_flip(k, n_ctx), _seg_flip(v, n_ctx),
                              _seg_flip(beta[:, :, 1], n_ctx), _seg_flip(log_alpha[:, :, 1], n_ctx)), n_ctx)
    o = o_f + o_b
    if not need_ctx:
        o, z = o[:, n_ctx:], z[:, n_ctx:]
    zf = z.reshape(bsz, -1, GDN_V_HEADS, GDN_DV).astype(jnp.float32)
    o = o * lax.rsqrt(jnp.mean(o * o, axis=-1, keepdims=True) + 1e-6) * norm_w.astype(jnp.float32) * jax.nn.silu(zf)
    y = o.reshape(bsz, -1, GDN_V_HEADS * GDN_DV).astype(u.dtype) @ w_out
    if need_ctx:
        return y[:, :n_ctx], y[:, n_ctx:]
    return None, y


def _swiglu(u, w_in, w_out):
    gate, up = jnp.split(u @ w_in, 2, axis=-1)
    return (jax.nn.silu(gate) * up) @ w_out


def setup_inputs(seed: int = 0) -> dict:
    key = jax.random.key(seed)
    ks = jax.random.split(key, 20)
    f32 = jnp.float32
    D = D_MODEL

    def nrm(k, shape, scale):
        return jax.random.normal(k, shape, f32) * scale

    x = nrm(ks[0], (BATCH, SEQ, D), 1.0)
    c = nrm(ks[1], (BATCH, D), 1.0)
    ctx = nrm(ks[2], (BATCH, CTX_LEN, D), 1.0)
    c_ctx = nrm(ks[3], (D,), 1.0)
    ada_w = nrm(ks[4], (DEPTH, D, 6 * D), 0.5 * D ** -0.5)
    ada_b = nrm(ks[5], (DEPTH, 6 * D), 0.02)
    ln_g = 1.0 + nrm(ks[6], (DEPTH, 2, D), 0.02)
    ln_b = nrm(ks[7], (DEPTH, 2, D), 0.02)
    ret_w_in = nrm(ks[8], (N_RET_LAYERS, D, RET_IN), D ** -0.5)
    ret_base = -(5.0 + jnp.arange(RET_HEADS, dtype=f32)) * math.log(2.0)
    ret_decay = ret_base + nrm(ks[9], (N_RET_LAYERS, 2, RET_HEADS), 0.1)
    ret_w_out = nrm(ks[10], (N_RET_LAYERS, RET_HEADS * RET_DV, D), (RET_HEADS * RET_DV) ** -0.5 * DEEPNORM_BETA)
    gdn_w_in = nrm(ks[11], (N_GDN_LAYERS, D, GDN_IN), D ** -0.5)
    gdn_conv = nrm(ks[12], (N_GDN_LAYERS, GDN_CONV, GDN_QKV), GDN_CONV ** -0.5)
    gdn_a_log = jnp.log(jax.random.uniform(ks[13], (N_GDN_LAYERS, 2, GDN_V_HEADS), f32, 1.0, 16.0))
    dt = jnp.exp(jax.random.uniform(ks[14], (N_GDN_LAYERS, 2, GDN_V_HEADS), f32, math.log(1e-3), math.log(1e-1)))
    gdn_dt_bias = dt + jnp.log(-jnp.expm1(-dt))
    gdn_norm = 1.0 + nrm(ks[15], (N_GDN_LAYERS, GDN_DV), 0.02)
    gdn_w_out = nrm(ks[16], (N_GDN_LAYERS, GDN_V_HEADS * GDN_DV, D), (GDN_V_HEADS * GDN_DV) ** -0.5 * DEEPNORM_BETA)
    ffn_w_in = nrm(ks[17], (DEPTH, D, 2 * D_FF), D ** -0.5)
    ffn_w_out = nrm(ks[18], (DEPTH, D_FF, D), D_FF ** -0.5 * DEEPNORM_BETA)
    return {'x': x, 'c': c, 'ctx': ctx, 'c_ctx': c_ctx, 'ada_w': ada_w, 'ada_b': ada_b,
            'ln_g': ln_g, 'ln_b': ln_b, 'ret_w_in': ret_w_in, 'ret_decay': ret_decay, 'ret_w_out': ret_w_out,
            'gdn_w_in': gdn_w_in, 'gdn_conv': gdn_conv, 'gdn_a_log': gdn_a_log, 'gdn_dt_bias': gdn_dt_bias,
            'gdn_norm': gdn_norm, 'gdn_w_out': gdn_w_out, 'ffn_w_in': ffn_w_in, 'ffn_w_out': ffn_w_out}


def reference(x, c, ctx, c_ctx, ada_w, ada_b, ln_g, ln_b, ret_w_in, ret_decay, ret_w_out,
              gdn_w_in, gdn_conv, gdn_a_log, gdn_dt_bias, gdn_norm, gdn_w_out, ffn_w_in, ffn_w_out):
    n_lat = x.shape[1]
    rows = n_lat // GRID_W
    cos, sin = _axial_rope(rows, RET_DK)
    h_lat, h_ctx = x, ctx
    for i in range(DEPTH):
        need_ctx = i < DEPTH - 1
        mod_l = (jax.nn.silu(c) @ ada_w[i] + ada_b[i]).reshape(-1, 6, 1, D_MODEL)
        mod_c = (jax.nn.silu(c_ctx) @ ada_w[i] + ada_b[i]).reshape(6, D_MODEL)
        u_l = h_lat * (1.0 + mod_l[:, 1]) + mod_l[:, 0]
        u_c = h_ctx * (1.0 + mod_c[1]) + mod_c[0]
        j = i // N_MIXERS
        if i % N_MIXERS == 0:
            y_c, y_l = retention_mixer(u_c, u_l, ret_w_in[j], ret_decay[j], ret_w_out[j], cos, sin, need_ctx)
        else:
            y_c, y_l = gated_deltanet_mixer(u_c, u_l, gdn_w_in[j], gdn_conv[j], gdn_a_log[j], gdn_dt_bias[j],
                                            gdn_norm[j], gdn_w_out[j], need_ctx)
        h_lat = _layer_norm(DEEPNORM_ALPHA * h_lat + mod_l[:, 2] * y_l, ln_g[i, 0], ln_b[i, 0])
        u_l = h_lat * (1.0 + mod_l[:, 4]) + mod_l[:, 3]
        h_lat = _layer_norm(DEEPNORM_ALPHA * h_lat + mod_l[:, 5] * _swiglu(u_l, ffn_w_in[i], ffn_w_out[i]),
                            ln_g[i, 1], ln_b[i, 1])
        if need_ctx:
            h_ctx = _layer_norm(DEEPNORM_ALPHA * h_ctx + mod_c[2] * y_c, ln_g[i, 0], ln_b[i, 0])
            u_c = h_ctx * (1.0 + mod_c[4]) + mod_c[3]
            h_ctx = _layer_norm(DEEPNORM_ALPHA * h_ctx + mod_c[5] * _swiglu(u_c, ffn_w_in[i], ffn_w_out[i]),
                                ln_g[i, 1], ln_b[i, 1])
    return h_lat
```

```python
import functools
import math

import jax
import jax.numpy as jnp
from jax import lax
from jax.experimental import pallas as pl
from jax.experimental.pallas import tpu as pltpu

F32 = jnp.float32
BF16 = jnp.bfloat16

GRID_W = 64
RET_HEADS = 8
RET_CHUNK = 128
ROPE_BASE = 10000.0
GDN_DK = 128
GDN_DV = 128
GDN_CONV = 5
GDN_CHUNK = 64
LN_EPS = 1e-5

VMEM_LIMIT_BYTES = 56 * 1024 * 1024
ROW_TILE = 256
MM_ROW_TILE = 1024


def _cparams(*sem):
    return pltpu.CompilerParams(dimension_semantics=sem, vmem_limit_bytes=VMEM_LIMIT_BYTES)


def _dot(a, b):
    return jnp.dot(a, b, preferred_element_type=F32)


def _dot_nt(a, b):
    return lax.dot_general(a, b, (((1,), (1,)), ((), ())), preferred_element_type=F32)


def _ada_kernel(c_ref, w_ref, b_ref, o_ref):
    a = jax.nn.silu(c_ref[...]).astype(BF16)
    o_ref[...] = _dot(a, w_ref[...].astype(BF16)) + b_ref[...]


def _ada_mod(cc, ada_w, ada_b, tn=1024):
    depth, d, n = ada_w.shape
    rows = cc.shape[0]
    return pl.pallas_call(
        _ada_kernel,
        grid=(depth, n // tn),
        in_specs=[
            pl.BlockSpec((rows, d), lambda l, j: (0, 0)),
            pl.BlockSpec((None, d, tn), lambda l, j: (l, 0, j)),
            pl.BlockSpec((None, 1, tn), lambda l, j: (l, 0, j)),
        ],
        out_specs=pl.BlockSpec((None, rows, tn), lambda l, j: (l, 0, j)),
        out_shape=jax.ShapeDtypeStruct((depth, rows, n), F32),
        compiler_params=_cparams("arbitrary", "arbitrary"),
        name="ada_mod",
    )(cc, ada_w, ada_b.reshape(depth, 1, n))


def _mod_kernel(h_ref, p_ref, u_ref, *, shift, scale):
    p = p_ref[...]
    u_ref[...] = (h_ref[...] * (1.0 + p[scale:scale + 1]) + p[shift:shift + 1]).astype(u_ref.dtype)


def _param_spec(d, seg_of_tile):
    return pl.BlockSpec((None, None, 6, d), lambda b, r: (b, seg_of_tile(r), 0, 0))


def _modulate(h, p, *, shift, scale, n_ctx):
    bsz, n, d = h.shape
    ctx_tiles = n_ctx // ROW_TILE
    seg = lambda r: jnp.where(r < ctx_tiles, 0, 1)
    return pl.pallas_call(
        functools.partial(_mod_kernel, shift=shift, scale=scale),
        grid=(bsz, n // ROW_TILE),
        in_specs=[pl.BlockSpec((None, ROW_TILE, d), lambda b, r: (b, r, 0)), _param_spec(d, seg)],
        out_specs=pl.BlockSpec((None, ROW_TILE, d), lambda b, r: (b, r, 0)),
        out_shape=jax.ShapeDtypeStruct((bsz, n, d), BF16),
        compiler_params=_cparams("arbitrary", "arbitrary"),
        name="modulate",
    )(h, p)


def _resid_ln_kernel(h_ref, y_ref, p_ref, g_ref, b_ref, p2_ref, h_out, *u_out, alpha, gate, shift, scale):
    p = p_ref[...]
    x = alpha * h_ref[...] + p[gate:gate + 1] * y_ref[...].astype(F32)
    mu = jnp.mean(x, axis=-1, keepdims=True)
    xc = x - mu
    var = jnp.mean(xc * xc, axis=-1, keepdims=True)
    hn = xc * lax.rsqrt(var + LN_EPS) * g_ref[...] + b_ref[...]
    h_out[...] = hn
    if u_out:
        p2 = p2_ref[...]
        u_out[0][...] = (hn * (1.0 + p2[scale:scale + 1]) + p2[shift:shift + 1]).astype(BF16)


def _resid_ln(h, y, p, ln_g, ln_b, p2, *, alpha, gate, shift, scale, n_ctx, h_row_offset=0, with_u=True):
    bsz, n, d = y.shape
    ctx_tiles = n_ctx // ROW_TILE
    off = h_row_offset // ROW_TILE
    seg = lambda r: jnp.where(r < ctx_tiles, 0, 1)
    row = pl.BlockSpec((None, ROW_TILE, d), lambda b, r: (b, r, 0))
    vec = pl.BlockSpec((1, d), lambda b, r: (0, 0))
    out_shape = [jax.ShapeDtypeStruct((bsz, n, d), F32)]
    out_specs = [row]
    if with_u:
        out_shape.append(jax.ShapeDtypeStruct((bsz, n, d), BF16))
        out_specs.append(row)
    outs = pl.pallas_call(
        functools.partial(_resid_ln_kernel, alpha=alpha, gate=gate, shift=shift, scale=scale),
        grid=(bsz, n // ROW_TILE),
        in_specs=[pl.BlockSpec((None, ROW_TILE, d), lambda b, r: (b, r + off, 0)), row,
                  _param_spec(d, seg), vec, vec, _param_spec(d, seg)],
        out_specs=out_specs,
        out_shape=out_shape,
        compiler_params=_cparams("arbitrary", "arbitrary"),
        name="resid_ln",
    )(h, y, p, ln_g.reshape(1, d), ln_b.reshape(1, d), p2)
    return outs if with_u else outs[0]


def _mm_kernel(a_ref, w_ref, o_ref, wb_ref):
    @pl.when(pl.program_id(1) == 0)
    def _():
        wb_ref[...] = w_ref[...].astype(BF16)

    o_ref[...] = _dot(a_ref[...], wb_ref[...]).astype(o_ref.dtype)


def _matmul(a, w, *, n_out, col_off=0, tn, tm=MM_ROW_TILE, out_dtype, name):
    m, k = a.shape
    assert m % tm == 0 and n_out % tn == 0 and col_off % tn == 0
    off = col_off // tn
    return pl.pallas_call(
        _mm_kernel,
        grid=(n_out // tn, m // tm),
        in_specs=[pl.BlockSpec((tm, k), lambda j, i: (i, 0)),
                  pl.BlockSpec((k, tn), lambda j, i: (0, j + off))],
        out_specs=pl.BlockSpec((tm, tn), lambda j, i: (i, j)),
        out_shape=jax.ShapeDtypeStruct((m, n_out), out_dtype),
        scratch_shapes=[pltpu.VMEM((k, tn), BF16)],
        compiler_params=_cparams("arbitrary", "arbitrary"),
        name=name,
    )(a, w)


def _swiglu_kernel(a_ref, wg_ref, wu_ref, o_ref, wgb_ref, wub_ref):
    @pl.when(pl.program_id(1) == 0)
    def _():
        wgb_ref[...] = wg_ref[...].astype(BF16)
        wub_ref[...] = wu_ref[...].astype(BF16)

    a = a_ref[...]
    gate = _dot(a, wgb_ref[...])
    up = _dot(a, wub_ref[...])
    o_ref[...] = (jax.nn.silu(gate) * up).astype(o_ref.dtype)


def _swiglu_in(a, w, *, tn=512, tm=MM_ROW_TILE):
    m, k = a.shape
    f = w.shape[1] // 2
    assert m % tm == 0 and f % tn == 0
    nb = f // tn
    return pl.pallas_call(
        _swiglu_kernel,
        grid=(nb, m // tm),
        in_specs=[pl.BlockSpec((tm, k), lambda j, i: (i, 0)),
                  pl.BlockSpec((k, tn), lambda j, i: (0, j)),
                  pl.BlockSpec((k, tn), lambda j, i: (0, j + nb))],
        out_specs=pl.BlockSpec((tm, tn), lambda j, i: (i, j)),
        out_shape=jax.ShapeDtypeStruct((m, f), BF16),
        scratch_shapes=[pltpu.VMEM((k, tn), BF16), pltpu.VMEM((k, tn), BF16)],
        compiler_params=_cparams("arbitrary", "arbitrary"),
        name="ffn_in_swiglu",
    )(a, w, w)


def _ffn(u, w_in, w_out):
    bsz, n, d = u.shape
    a = _swiglu_in(u.reshape(bsz * n, d), w_in)
    y = _matmul(a, w_out, n_out=d, tn=512, tm=512, out_dtype=F32, name="ffn_out")
    return y.reshape(bsz, n, d)


def _ret_kernel(draw_ref, q_ref, k_ref, v_ref, g_ref, cos_ref, sin_ref, o_ref,
                qs_ref, ks_ref, oacc_ref, sf_ref, sb_ref, *, n_chunks, ctx_chunks, skip_chunks, dk):
    C = RET_CHUNK
    h = pl.program_id(1)
    half = dk // 2
    lg_f = -jnp.exp(jnp.full((1, 1), draw_ref[0, h], F32))
    lg_b = -jnp.exp(jnp.full((1, 1), draw_ref[1, h], F32))

    ri = lax.broadcasted_iota(jnp.int32, (C, C), 0)
    ci = lax.broadcasted_iota(jnp.int32, (C, C), 1)
    diff = (ri - ci).astype(F32)
    lower = diff >= 0
    dmat = jnp.where(lower, jnp.exp(lg_f * jnp.where(lower, diff, 0.0)),
                     jnp.exp(lg_b * jnp.where(lower, 0.0, -diff)))
    rc = lax.broadcasted_iota(jnp.int32, (C, 1), 0).astype(F32)
    qdec_f = jnp.exp(lg_f * (rc + 1.0))
    kdec_f = jnp.exp(lg_f * (C - 1.0 - rc))
    cdec_f = jnp.exp(lg_f * C)
    qdec_b = jnp.exp(lg_b * (C - rc))
    kdec_b = jnp.exp(lg_b * rc)
    cdec_b = jnp.exp(lg_b * C)

    sf_ref[...] = jnp.zeros_like(sf_ref)
    sb_ref[...] = jnp.zeros_like(sb_ref)
    qscale = dk ** -0.5

    def rope(x, cs, sn):
        x1, x2 = x[:, :half], x[:, half:]
        return jnp.concatenate([x1 * cs - x2 * sn, x1 * sn + x2 * cs], axis=-1)

    def fwd(c, carry):
        rows = pl.ds(pl.multiple_of(c * C, C), C)
        cs, sn = cos_ref[rows, :], sin_ref[rows, :]
        q = rope(q_ref[rows, :].astype(F32), cs, sn) * qscale
        k = rope(k_ref[rows, :].astype(F32), cs, sn)
        qb, kb = q.astype(BF16), k.astype(BF16)
        qs_ref[rows, :] = qb
        ks_ref[rows, :] = kb
        v = v_ref[rows, :]
        s = (_dot_nt(qb, kb) * dmat).astype(BF16)
        state = sf_ref[...]
        oacc_ref[rows, :] = _dot(s, v) + _dot(qb, state.astype(BF16)) * qdec_f
        kd_t = jnp.transpose(k * kdec_f).astype(BF16)
        sf_ref[...] = cdec_f * state + _dot(kd_t, v)
        return carry

    lax.fori_loop(0, n_chunks, fwd, 0)

    def bwd(t, carry):
        c = jnp.where(t < ctx_chunks, ctx_chunks - 1 - t, n_chunks - 1 + ctx_chunks - t)
        rows = pl.ds(pl.multiple_of(c * C, C), C)
        qb, kb, v = qs_ref[rows, :], ks_ref[rows, :], v_ref[rows, :]
        state = sb_ref[...]
        o = oacc_ref[rows, :] + _dot(qb, state.astype(BF16)) * qdec_b
        kd_t = jnp.transpose(kb.astype(F32) * kdec_b).astype(BF16)
        sb_ref[...] = cdec_b * state + _dot(kd_t, v)

        def emit():
            mu = jnp.mean(o, axis=-1, keepdims=True)
            oc = o - mu
            var = jnp.mean(oc * oc, axis=-1, keepdims=True)
            on = oc * lax.rsqrt(var + LN_EPS)
            out_rows = pl.ds(pl.multiple_of((c - skip_chunks) * C, C), C)
            o_ref[out_rows, :] = (on * jax.nn.silu(g_ref[rows, :].astype(F32))).astype(o_ref.dtype)

        if skip_chunks:
            pl.when(c >= skip_chunks)(emit)
        else:
            emit()
        return carry

    lax.fori_loop(0, n_chunks, bwd, 0)


def _retention(qkvg, decay_raw, cos, sin, *, n_ctx, need_ctx):
    bsz, n, width = qkvg.shape
    H = RET_HEADS
    dk = width // (6 * H)
    dv = 2 * dk
    C = RET_CHUNK
    assert n % C == 0 and n_ctx % C == 0
    skip = 0 if need_ctx else n_ctx // C
    n_out = n - skip * C
    kern = functools.partial(_ret_kernel, n_chunks=n // C, ctx_chunks=n_ctx // C, skip_chunks=skip, dk=dk)
    return pl.pallas_call(
        kern,
        grid=(bsz, H),
        in_specs=[
            pl.BlockSpec(memory_space=pltpu.SMEM),
            pl.BlockSpec((None, n, dk), lambda b, h: (b, 0, h)),
            pl.BlockSpec((None, n, dk), lambda b, h: (b, 0, H + h)),
            pl.BlockSpec((None, n, dv), lambda b, h: (b, 0, H + h)),
            pl.BlockSpec((None, n, dv), lambda b, h: (b, 0, 2 * H + h)),
            pl.BlockSpec((n, dk // 2), lambda b, h: (0, 0)),
            pl.BlockSpec((n, dk // 2), lambda b, h: (0, 0)),
        ],
        out_specs=pl.BlockSpec((None, n_out, dv), lambda b, h: (b, 0, h)),
        out_shape=jax.ShapeDtypeStruct((bsz, n_out, H * dv), BF16),
        scratch_shapes=[pltpu.VMEM((n, dk), BF16), pltpu.VMEM((n, dk), BF16), pltpu.VMEM((n, dv), F32),
                        pltpu.VMEM((dk, dv), F32), pltpu.VMEM((dk, dv), F32)],
        compiler_params=_cparams("arbitrary", "arbitrary"),
        name="retention",
    )(decay_raw.astype(F32), qkvg, qkvg, qkvg, qkvg, cos, sin)


def _rope_tables(n_ctx, n_lat, dim):
    rows = n_lat // GRID_W
    row = jnp.repeat(jnp.arange(rows, dtype=F32), GRID_W)
    col = jnp.tile(jnp.arange(GRID_W, dtype=F32), rows)
    n_freq = dim // 4
    inv_freq = ROPE_BASE ** (-jnp.arange(n_freq, dtype=F32) / n_freq)
    ang = jnp.concatenate([row[:, None] * inv_freq, col[:, None] * inv_freq], axis=-1)
    cos = jnp.concatenate([jnp.ones((n_ctx, dim // 2), F32), jnp.cos(ang)], axis=0)
    sin = jnp.concatenate([jnp.zeros((n_ctx, dim // 2), F32), jnp.sin(ang)], axis=0)
    return cos, sin


def _conv_kernel(x_ref, w_ref, o_ref, *, n_ctx, n_q_blocks, n_qk_blocks, qscale):
    j = pl.program_id(1)
    x = x_ref[...].astype(F32)
    n, tc = x.shape
    w = w_ref[...]
    pad = GDN_CONV // 2
    t = lax.broadcasted_iota(jnp.int32, (n, 1), 0)
    lo = jnp.where(t < n_ctx, 0, n_ctx)
    hi = jnp.where(t < n_ctx, n_ctx, n)
    acc = x * w[pad:pad + 1]
    for d in range(-pad, pad + 1):
        if d == 0:
            continue
        xs = pltpu.roll(x, (n - d) % n, 0)
        valid = (t + d >= lo) & (t + d < hi)
        acc = acc + jnp.where(valid, xs, 0.0) * w[d + pad:d + pad + 1]
    y = jax.nn.silu(acc)
    do_norm = j < n_qk_blocks
    scale = jnp.where(j < n_q_blocks, qscale, 1.0).astype(F32)
    for g0 in range(0, tc, GDN_DK):
        yg = y[:, g0:g0 + GDN_DK]
        ss = jnp.sum(yg * yg, axis=-1, keepdims=True)
        yn = yg * (lax.rsqrt(ss + 1e-6) * scale)
        o_ref[:, g0:g0 + GDN_DK] = jnp.where(do_norm, yn, yg).astype(o_ref.dtype)


def _gdn_conv(proj, conv_w, *, n_ctx, n_qk_heads, tc=512):
    bsz, n, _ = proj.shape
    width = conv_w.shape[1]
    qw = n_qk_heads * GDN_DK
    assert qw % tc == 0 and width % tc == 0
    kern = functools.partial(_conv_kernel, n_ctx=n_ctx, n_q_blocks=qw // tc, n_qk_blocks=2 * qw // tc,
                             qscale=GDN_DK ** -0.5)
    return pl.pallas_call(
        kern,
        grid=(bsz, width // tc),
        in_specs=[pl.BlockSpec((None, n, tc), lambda b, j: (b, 0, j)),
                  pl.BlockSpec((GDN_CONV, tc), lambda b, j: (0, j))],
        out_specs=pl.BlockSpec((None, n, tc), lambda b, j: (b, 0, j)),
        out_shape=jax.ShapeDtypeStruct((bsz, n, width), BF16),
        compiler_params=_cparams("arbitrary", "arbitrary"),
        name="gdn_conv",
    )(proj, conv_w)


def _gates_kernel(x_ref, alog_ref, dt_ref, o_ref, *, n_heads):
    C = GDN_CHUNK
    x = x_ref[...]
    n = x.shape[0]
    beta = jax.nn.sigmoid(x)
    z = x + dt_ref[...]
    softplus = jnp.maximum(z, 0.0) + jnp.log1p(jnp.exp(-jnp.abs(z)))
    la = -jnp.exp(alog_ref[...]) * softplus
    tm = lax.broadcasted_iota(jnp.int32, (n, 1), 0) % C
    pre = la
    suf = la
    s = 1
    while s < C:
        pre = pre + jnp.where(tm >= s, pltpu.roll(pre, s, 0), 0.0)
        suf = suf + jnp.where(tm < C - s, pltpu.roll(suf, n - s, 0), 0.0)
        s *= 2
    lane = lax.broadcasted_iota(jnp.int32, (1, x.shape[1]), 1)
    o_ref[...] = jnp.where(lane < 2 * n_heads, beta, jnp.where(lane < 3 * n_heads, pre, suf))


def _gdn_gates(ba, a_log, dt_bias):
    bsz, n, w = ba.shape
    n_heads = w // 4
    zeros = jnp.zeros((2 * n_heads,), F32)
    alog = jnp.concatenate([zeros, a_log.astype(F32).reshape(-1)]).reshape(1, w)
    dt = jnp.concatenate([zeros, dt_bias.astype(F32).reshape(-1)]).reshape(1, w)
    return pl.pallas_call(
        functools.partial(_gates_kernel, n_heads=n_heads),
        grid=(bsz,),
        in_specs=[pl.BlockSpec((None, n, w), lambda b: (b, 0, 0)),
                  pl.BlockSpec((1, w), lambda b: (0, 0)), pl.BlockSpec((1, w), lambda b: (0, 0))],
        out_specs=pl.BlockSpec((None, n, w), lambda b: (b, 0, 0)),
        out_shape=jax.ShapeDtypeStruct((bsz, n, w), F32),
        compiler_params=_cparams("arbitrary"),
        name="gdn_gates",
    )(ba, alog, dt)


def _unit_tri_inverse(a, r, c):
    def mm(x, y):
        return _dot(x.astype(BF16), y.astype(BF16))

    same16 = (r // 16) == (c // 16)
    same32 = (r // 32) == (c // 32)
    eye = (r == c).astype(F32)
    ad = jnp.where(same16, a, 0.0)
    x = eye - ad
    p = mm(ad, ad)
    x = x + mm(x, p)
    p = mm(p, p)
    x = x + mm(x, p)
    p = mm(p, p)
    x = x + mm(x, p)
    a32 = jnp.where(same32 & jnp.logical_not(same16), a, 0.0)
    x = x - mm(mm(x, a32), x)
    a64 = jnp.where(same32, 0.0, a)
    x = x - mm(mm(x, a64), x)
    return x


def _gdn_kernel(q_ref, k_ref, v_ref, z_ref, gcol_ref, grow_ref, nw_ref, o_ref,
                u0_ref, w_ref, qk_ref, of_ref, ob_ref, m_ref, *, n_chunks, ctx_chunks, skip_chunks):
    C = GDN_CHUNK
    dv = GDN_DV
    r = lax.broadcasted_iota(jnp.int32, (C, C), 0)
    c = lax.broadcasted_iota(jnp.int32, (C, C), 1)
    incl = (r >= c, c >= r)
    strict = (r > c, c > r)
    last_row = (C - 1, 0)

    def gate_cols(rows, inst):
        vj, d = inst
        g = gcol_ref[rows, :]
        beta = g[:, 2 * d + vj:2 * d + vj + 1]
        gc = g[:, 4 + 2 * d + vj:4 + 2 * d + vj + 1]
        return beta, gc

    instances = [(vj, d) for vj in range(2) for d in range(2)]

    def prep(ci, carry):
        rows = pl.ds(pl.multiple_of(ci * C, C), C)
        qb, kb = q_ref[rows, :], k_ref[rows, :]
        kf = kb.astype(F32)
        kk = _dot_nt(kb, kb)
        qk = _dot_nt(qb, kb)
        grow = grow_ref[pl.ds(ci, 1)]
        for idx, (vj, d) in enumerate(instances):
            beta, gc = gate_cols(rows, (vj, d))
            gr = grow[0, 4 + 2 * d + vj:4 + 2 * d + vj + 1, :]
            dec = jnp.where(incl[d], jnp.exp(jnp.where(incl[d], gc - gr, 0.0)), 0.0)
            a = jnp.where(strict[d], beta * kk * dec, 0.0)
            x = _unit_tri_inverse(a, r, c)
            vf = v_ref[rows, vj * dv:(vj + 1) * dv].astype(F32)
            rhs = jnp.concatenate([vf * beta, kf * (beta * jnp.exp(gc))], axis=-1).astype(BF16)
            sol = _dot(x.astype(BF16), rhs)
            u0_ref[idx, rows, :] = sol[:, :dv]
            w_ref[idx, rows, :] = sol[:, dv:].astype(BF16)
            qk_ref[idx, rows, :] = (qk * dec).astype(BF16)
        return carry

    lax.fori_loop(0, n_chunks, prep, 0)

    m_ref[...] = jnp.zeros_like(m_ref)

    def step(t, carry):
        cf = t
        cb = jnp.where(t < ctx_chunks, ctx_chunks - 1 - t, n_chunks - 1 + ctx_chunks - t)
        for idx, (vj, d) in enumerate(instances):
            ci = cb if d else cf
            rows = pl.ds(pl.multiple_of(ci * C, C), C)
            beta, gc = gate_cols(rows, (vj, d))
            g_last = gc[last_row[d]:last_row[d] + 1, :]
            qf = q_ref[rows, :].astype(F32)
            kf = k_ref[rows, :].astype(F32)
            m = m_ref[idx]
            mb = m.astype(BF16)
            u = u0_ref[idx, rows, :] - _dot(w_ref[idx, rows, :], mb)
            ub = u.astype(BF16)
            o = _dot((qf * jnp.exp(gc)).astype(BF16), mb) + _dot(qk_ref[idx, rows, :], ub)
            kd_t = jnp.transpose(kf * jnp.exp(g_last - gc)).astype(BF16)
            m_ref[idx] = jnp.exp(g_last) * m + _dot(kd_t, ub)
            if d:
                ob_ref[vj, rows, :] = o
            else:
                of_ref[vj, rows, :] = o
        return carry

    lax.fori_loop(0, n_chunks, step, 0)

    nw = nw_ref[...]

    def emit(ci, carry):
        rows = pl.ds(pl.multiple_of(ci * C, C), C)
        out_rows = pl.ds(pl.multiple_of((ci - skip_chunks) * C, C), C)
        for vj in range(2):
            o = of_ref[vj, rows, :] + ob_ref[vj, rows, :]
            zf = z_ref[rows, vj * dv:(vj + 1) * dv].astype(F32)
            on = o * lax.rsqrt(jnp.mean(o * o, axis=-1, keepdims=True) + 1e-6) * nw * jax.nn.silu(zf)
            o_ref[out_rows, vj * dv:(vj + 1) * dv] = on.astype(o_ref.dtype)
        return carry

    lax.fori_loop(skip_chunks, n_chunks, emit, 0)


def _gdn_scan(qkv, proj, gates, norm_w, *, n_ctx, need_ctx, n_qk_heads, z_col_off):
    bsz, n, _ = qkv.shape
    C = GDN_CHUNK
    HQ = n_qk_heads
    HV = gates.shape[-1] // 4
    assert HV == 2 * HQ and n % C == 0 and n_ctx % C == 0
    nch = n // C
    skip = 0 if need_ctx else n_ctx // C
    n_out = n - skip * C
    g5 = gates.reshape(bsz, n, 4, HQ, 2)
    gcol = g5.transpose(0, 3, 1, 2, 4).reshape(bsz, HQ, n, 8)
    grow = g5.reshape(bsz, nch, C, 4, HQ, 2).transpose(0, 4, 1, 3, 5, 2).reshape(bsz, HQ, nch, 8, C)
    pair = 2 * GDN_DV
    zb = z_col_off // pair
    kern = functools.partial(_gdn_kernel, n_chunks=nch, ctx_chunks=n_ctx // C, skip_chunks=skip)
    return pl.pallas_call(
        kern,
        grid=(bsz, HQ),
        in_specs=[
            pl.BlockSpec((None, n, GDN_DK), lambda b, h: (b, 0, h)),
            pl.BlockSpec((None, n, GDN_DK), lambda b, h: (b, 0, HQ + h)),
            pl.BlockSpec((None, n, pair), lambda b, h: (b, 0, HQ + h)),
            pl.BlockSpec((None, n, pair), lambda b, h: (b, 0, zb + h)),
            pl.BlockSpec((None, None, n, 8), lambda b, h: (b, h, 0, 0)),
            pl.BlockSpec((None, None, nch, 8, C), lambda b, h: (b, h, 0, 0, 0)),
            pl.BlockSpec((1, GDN_DV), lambda b, h: (0, 0)),
        ],
        out_specs=pl.BlockSpec((None, n_out, pair), lambda b, h: (b, 0, h)),
        out_shape=jax.ShapeDtypeStruct((bsz, n_out, HV * GDN_DV), BF16),
        scratch_shapes=[pltpu.VMEM((4, n, GDN_DV), F32), pltpu.VMEM((4, n, GDN_DK), BF16),
                        pltpu.VMEM((4, n, C), BF16), pltpu.VMEM((2, n, GDN_DV), F32),
                        pltpu.VMEM((2, n, GDN_DV), F32), pltpu.VMEM((4, GDN_DK, GDN_DV), F32)],
        compiler_params=_cparams("arbitrary", "arbitrary"),
        name="gdn_scan",
    )(qkv, qkv, qkv, proj, gcol, grow, norm_w.astype(F32).reshape(1, GDN_DV))


def kernel(x, c, ctx, c_ctx, ada_w, ada_b, ln_g, ln_b, ret_w_in, ret_decay, ret_w_out,
           gdn_w_in, gdn_conv, gdn_a_log, gdn_dt_bias, gdn_norm, gdn_w_out, ffn_w_in, ffn_w_out):
    bsz, n_lat, d = x.shape
    n_ctx = ctx.shape[1]
    depth = ada_w.shape[0]
    n_tot = n_ctx + n_lat
    alpha = (2 * depth) ** 0.25

    pad_rows = (-(bsz + 1)) % 16
    cc = jnp.concatenate([c, c_ctx[None], jnp.zeros((pad_rows, d), F32)], axis=0)
    mods = _ada_mod(cc, ada_w, ada_b)

    def layer_params(i):
        lat = mods[i, :bsz].reshape(bsz, 1, 6, d)
        cx = jnp.broadcast_to(mods[i, bsz].reshape(1, 1, 6, d), (bsz, 1, 6, d))
        return jnp.concatenate([cx, lat], axis=1)

    params = [layer_params(i) for i in range(depth)]
    h = jnp.concatenate([ctx, x], axis=1)
    u = _modulate(h, params[0], shift=0, scale=1, n_ctx=n_ctx)
    ctx_rows = n_ctx

    for i in range(depth):
        need_ctx = i < depth - 1
        j = i // 2
        u2 = u.reshape(bsz * n_tot, d)
        if i % 2 == 0:
            qkvg = _matmul(u2, ret_w_in[j], n_out=ret_w_in.shape[-1], tn=1024, out_dtype=BF16,
                           name="ret_in_proj").reshape(bsz, n_tot, -1)
            dk = qkvg.shape[-1] // (6 * RET_HEADS)
            cos, sin = _rope_tables(n_ctx, n_lat, dk)
            o = _retention(qkvg, ret_decay[j], cos, sin, n_ctx=n_ctx, need_ctx=need_ctx)
            w_out = ret_w_out[j]
        else:
            w_in = gdn_w_in[j]
            qkv_w = gdn_conv.shape[-1]
            hv = gdn_a_log.shape[-1]
            main_w = qkv_w + hv * GDN_DV
            proj = _matmul(u2, w_in, n_out=main_w, tn=1024, out_dtype=BF16,
                           name="gdn_in_proj").reshape(bsz, n_tot, main_w)
            ba = _matmul(u2, w_in, n_out=4 * hv, col_off=main_w, tn=4 * hv, out_dtype=F32,
                         name="gdn_gate_proj").reshape(bsz, n_tot, 4 * hv)
            gates = _gdn_gates(ba, gdn_a_log[j], gdn_dt_bias[j])
            qkv = _gdn_conv(proj, gdn_conv[j], n_ctx=n_ctx, n_qk_heads=hv // 2)
            o = _gdn_scan(qkv, proj, gates, gdn_norm[j], n_ctx=n_ctx, need_ctx=need_ctx,
                          n_qk_heads=hv // 2, z_col_off=qkv_w)
            w_out = gdn_w_out[j]

        n_rows = o.shape[1]
        out_ctx = n_ctx if need_ctx else 0
        y = _matmul(o.reshape(bsz * n_rows, -1), w_out, n_out=d, tn=512, out_dtype=F32,
                    name="mixer_out_proj").reshape(bsz, n_rows, d)
        h, u = _resid_ln(h, y, params[i], ln_g[i, 0], ln_b[i, 0], params[i], alpha=alpha, gate=2,
                         shift=3, scale=4, n_ctx=out_ctx, h_row_offset=ctx_rows - out_ctx)
        ctx_rows = out_ctx
        y = _ffn(u, ffn_w_in[i], ffn_w_out[i])
        if need_ctx:
            h, u = _resid_ln(h, y, params[i], ln_g[i, 1], ln_b[i, 1], params[i + 1], alpha=alpha, gate=5,
                             shift=0, scale=1, n_ctx=out_ctx)
        else:
            h = _resid_ln(h, y, params[i], ln_g[i, 1], ln_b[i, 1], params[i], alpha=alpha, gate=5,
                          shift=0, scale=1, n_ctx=out_ctx, with_u=False)
    return h
```

```python
import functools
import math

import jax
import jax.numpy as jnp
from jax import lax
from jax.experimental import pallas as pl
from jax.experimental.pallas import tpu as pltpu

F32 = jnp.float32
BF16 = jnp.bfloat16

GRID_W = 64
RET_HEADS = 8
RET_CHUNK = 128
ROPE_BASE = 10000.0
GDN_DK = 128
GDN_DV = 128
GDN_CONV = 5
GDN_CHUNK = 64
LN_EPS = 1e-5

VMEM_LIMIT_BYTES = 56 * 1024 * 1024
ROW_TILE = 256
MM_ROW_TILE = 1024


def _cparams(*sem):
    return pltpu.CompilerParams(dimension_semantics=sem, vmem_limit_bytes=VMEM_LIMIT_BYTES)


def _dot(a, b):
    return jnp.dot(a, b, preferred_element_type=F32)


def _dot_nt(a, b):
    return lax.dot_general(a, b, (((1,), (1,)), ((), ())), preferred_element_type=F32)


def _ada_kernel(c_ref, w_ref, b_ref, o_ref):
    a = jax.nn.silu(c_ref[...]).astype(BF16)
    o_ref[...] = _dot(a, w_ref[...].astype(BF16)) + b_ref[...]


def _ada_mod(cc, ada_w, ada_b, tn=1024):
    depth, d, n = ada_w.shape
    rows = cc.shape[0]
    return pl.pallas_call(
        _ada_kernel,
        grid=(depth, n // tn),
        in_specs=[
            pl.BlockSpec((rows, d), lambda l, j: (0, 0)),
            pl.BlockSpec((None, d, tn), lambda l, j: (l, 0, j)),
            pl.BlockSpec((None, 1, tn), lambda l, j: (l, 0, j)),
        ],
        out_specs=pl.BlockSpec((None, rows, tn), lambda l, j: (l, 0, j)),
        out_shape=jax.ShapeDtypeStruct((depth, rows, n), F32),
        compiler_params=_cparams("arbitrary", "arbitrary"),
        name="ada_mod",
    )(cc, ada_w, ada_b.reshape(depth, 1, n))


def _mod_kernel(h_ref, p_ref, u_ref, *, shift, scale):
    p = p_ref[...]
    u_ref[...] = (h_ref[...] * (1.0 + p[scale:scale + 1]) + p[shift:shift + 1]).astype(u_ref.dtype)


def _param_spec(d, seg_of_tile):
    return pl.BlockSpec((None, None, 6, d), lambda b, r: (b, seg_of_tile(r), 0, 0))


def _modulate(h, p, *, shift, scale, n_ctx):
    bsz, n, d = h.shape
    ctx_tiles = n_ctx // ROW_TILE
    seg = lambda r: jnp.where(r < ctx_tiles, 0, 1)
    return pl.pallas_call(
        functools.partial(_mod_kernel, shift=shift, scale=scale),
        grid=(bsz, n // ROW_TILE),
        in_specs=[pl.BlockSpec((None, ROW_TILE, d), lambda b, r: (b, r, 0)), _param_spec(d, seg)],
        out_specs=pl.BlockSpec((None, ROW_TILE, d), lambda b, r: (b, r, 0)),
        out_shape=jax.ShapeDtypeStruct((bsz, n, d), BF16),
        compiler_params=_cparams("arbitrary", "arbitrary"),
        name="modulate",
    )(h, p)


def _resid_ln_kernel(h_ref, y_ref, p_ref, g_ref, b_ref, p2_ref, h_out, *u_out, alpha, gate, shift, scale):
    p = p_ref[...]
    x = alpha * h_ref[...] + p[gate:gate + 1] * y_ref[...].astype(F32)
    mu = jnp.mean(x, axis=-1, keepdims=True)
    xc = x - mu
    var = jnp.mean(xc * xc, axis=-1, keepdims=True)
    hn = xc * lax.rsqrt(var + LN_EPS) * g_ref[...] + b_ref[...]
    h_out[...] = hn
    if u_out:
        p2 = p2_ref[...]
        u_out[0][...] = (hn * (1.0 + p2[scale:scale + 1]) + p2[shift:shift + 1]).astype(BF16)


def _resid_ln(h, y, p, ln_g, ln_b, p2, *, alpha, gate, shift, scale, n_ctx, h_row_offset=0, with_u=True):
    bsz, n, d = y.shape
    ctx_tiles = n_ctx // ROW_TILE
    off = h_row_offset // ROW_TILE
    seg = lambda r: jnp.where(r < ctx_tiles, 0, 1)
    row = pl.BlockSpec((None, ROW_TILE, d), lambda b, r: (b, r, 0))
    vec = pl.BlockSpec((1, d), lambda b, r: (0, 0))
    out_shape = [jax.ShapeDtypeStruct((bsz, n, d), F32)]
    out_specs = [row]
    if with_u:
        out_shape.append(jax.ShapeDtypeStruct((bsz, n, d), BF16))
        out_specs.append(row)
    outs = pl.pallas_call(
        functools.partial(_resid_ln_kernel, alpha=alpha, gate=gate, shift=shift, scale=scale),
        grid=(bsz, n // ROW_TILE),
        in_specs=[pl.BlockSpec((None, ROW_TILE, d), lambda b, r: (b, r + off, 0)), row,
                  _param_spec(d, seg), vec, vec, _param_spec(d, seg)],
        out_specs=out_specs,
        out_shape=out_shape,
        compiler_params=_cparams("arbitrary", "arbitrary"),
        name="resid_ln",
    )(h, y, p, ln_g.reshape(1, d), ln_b.reshape(1, d), p2)
    return outs if with_u else outs[0]


def _mm_kernel(a_ref, w_ref, o_ref, wb_ref):
    @pl.when(pl.program_id(1) == 0)
    def _():
        wb_ref[...] = w_ref[...].astype(BF16)

    o_ref[...] = _dot(a_ref[...], wb_ref[...]).astype(o_ref.dtype)


def _matmul(a, w, *, n_out, col_off=0, tn, tm=MM_ROW_TILE, out_dtype, name):
    m, k = a.shape
    assert m % tm == 0 and n_out % tn == 0 and col_off % tn == 0
    off = col_off // tn
    return pl.pallas_call(
        _mm_kernel,
        grid=(n_out // tn, m // tm),
        in_specs=[pl.BlockSpec((tm, k), lambda j, i: (i, 0)),
                  pl.BlockSpec((k, tn), lambda j, i: (0, j + off))],
        out_specs=pl.BlockSpec((tm, tn), lambda j, i: (i, j)),
        out_shape=jax.ShapeDtypeStruct((m, n_out), out_dtype),
        scratch_shapes=[pltpu.VMEM((k, tn), BF16)],
        compiler_params=_cparams("arbitrary", "arbitrary"),
        name=name,
    )(a, w)


def _swiglu_kernel(a_ref, wg_ref, wu_ref, o_ref, wgb_ref, wub_ref):
    @pl.when(pl.program_id(1) == 0)
    def _():
        wgb_ref[...] = wg_ref[...].astype(BF16)
        wub_ref[...] = wu_ref[...].astype(BF16)

    a = a_ref[...]
    gate = _dot(a, wgb_ref[...])
    up = _dot(a, wub_ref[...])
    o_ref[...] = (jax.nn.silu(gate) * up).astype(o_ref.dtype)


def _swiglu_in(a, w, *, tn=512, tm=MM_ROW_TILE):
    m, k = a.shape
    f = w.shape[1] // 2
    assert m % tm == 0 and f % tn == 0
    nb = f // tn
    return pl.pallas_call(
        _swiglu_kernel,
        grid=(nb, m // tm),
        in_specs=[pl.BlockSpec((tm, k), lambda j, i: (i, 0)),
                  pl.BlockSpec((k, tn), lambda j, i: (0, j)),
                  pl.BlockSpec((k, tn), lambda j, i: (0, j + nb))],
        out_specs=pl.BlockSpec((tm, tn), lambda j, i: (i, j)),
        out_shape=jax.ShapeDtypeStruct((m, f), BF16),
        scratch_shapes=[pltpu.VMEM((k, tn), BF16), pltpu.VMEM((k, tn), BF16)],
        compiler_params=_cparams("arbitrary", "arbitrary"),
        name="ffn_in_swiglu",
    )(a, w, w)


def _ffn(u, w_in, w_out):
    bsz, n, d = u.shape
    a = _swiglu_in(u.reshape(bsz * n, d), w_in)
    y = _matmul(a, w_out, n_out=d, tn=512, tm=512, out_dtype=F32, name="ffn_out")
    return y.reshape(bsz, n, d)


def _ret_kernel(draw_ref, q_ref, k_ref, v_ref, g_ref, cos_ref, sin_ref, o_ref,
                qs_ref, ks_ref, oacc_ref, sf_ref, sb_ref, *, n_chunks, ctx_chunks, skip_chunks, dk):
    C = RET_CHUNK
    h = pl.program_id(1)
    half = dk // 2
    lg_f = -jnp.exp(jnp.full((1, 1), draw_ref[0, h], F32))
    lg_b = -jnp.exp(jnp.full((1, 1), draw_ref[1, h], F32))

    ri = lax.broadcasted_iota(jnp.int32, (C, C), 0)
    ci = lax.broadcasted_iota(jnp.int32, (C, C), 1)
    diff = (ri - ci).astype(F32)
    lower = diff >= 0
    dmat = jnp.where(lower, jnp.exp(lg_f * jnp.where(lower, diff, 0.0)),
                     jnp.exp(lg_b * jnp.where(lower, 0.0, -diff)))
    rc = lax.broadcasted_iota(jnp.int32, (C, 1), 0).astype(F32)
    qdec_f = jnp.exp(lg_f * (rc + 1.0))
    kdec_f = jnp.exp(lg_f * (C - 1.0 - rc))
    cdec_f = jnp.exp(lg_f * C)
    qdec_b = jnp.exp(lg_b * (C - rc))
    kdec_b = jnp.exp(lg_b * rc)
    cdec_b = jnp.exp(lg_b * C)

    sf_ref[...] = jnp.zeros_like(sf_ref)
    sb_ref[...] = jnp.zeros_like(sb_ref)
    qscale = dk ** -0.5

    def rope(x, cs, sn):
        x1, x2 = x[:, :half], x[:, half:]
        return jnp.concatenate([x1 * cs - x2 * sn, x1 * sn + x2 * cs], axis=-1)

    def fwd(c, carry):
        rows = pl.ds(pl.multiple_of(c * C, C), C)
        cs, sn = cos_ref[rows, :], sin_ref[rows, :]
        q = rope(q_ref[rows, :].astype(F32), cs, sn) * qscale
        k = rope(k_ref[rows, :].astype(F32), cs, sn)
        qb, kb = q.astype(BF16), k.astype(BF16)
        qs_ref[rows, :] = qb
        ks_ref[rows, :] = kb
        v = v_ref[rows, :]
        s = (_dot_nt(qb, kb) * dmat).astype(BF16)
        state = sf_ref[...]
        oacc_ref[rows, :] = _dot(s, v) + _dot(qb, state.astype(BF16)) * qdec_f
        kd_t = jnp.transpose(k * kdec_f).astype(BF16)
        sf_ref[...] = cdec_f * state + _dot(kd_t, v)
        return carry

    lax.fori_loop(0, n_chunks, fwd, 0)

    def bwd(t, carry):
        c = jnp.where(t < ctx_chunks, ctx_chunks - 1 - t, n_chunks - 1 + ctx_chunks - t)
        rows = pl.ds(pl.multiple_of(c * C, C), C)
        qb, kb, v = qs_ref[rows, :], ks_ref[rows, :], v_ref[rows, :]
        state = sb_ref[...]
        o = oacc_ref[rows, :] + _dot(qb, state.astype(BF16)) * qdec_b
        kd_t = jnp.transpose(kb.astype(F32) * kdec_b).astype(BF16)
        sb_ref[...] = cdec_b * state + _dot(kd_t, v)

        def emit():
            mu = jnp.mean(o, axis=-1, keepdims=True)
            oc = o - mu
            var = jnp.mean(oc * oc, axis=-1, keepdims=True)
            on = oc * lax.rsqrt(var + LN_EPS)
            out_rows = pl.ds(pl.multiple_of((c - skip_chunks) * C, C), C)
            o_ref[out_rows, :] = (on * jax.nn.silu(g_ref[rows, :].astype(F32))).astype(o_ref.dtype)

        if skip_chunks:
            pl.when(c >= skip_chunks)(emit)
        else:
            emit()
        return carry

    lax.fori_loop(0, n_chunks, bwd, 0)


def _retention(qkvg, decay_raw, cos, sin, *, n_ctx, need_ctx):
    bsz, n, width = qkvg.shape
    H = RET_HEADS
    dk = width // (6 * H)
    dv = 2 * dk
    C = RET_CHUNK
    assert n % C == 0 and n_ctx % C == 0
    skip = 0 if need_ctx else n_ctx // C
    n_out = n - skip * C
    kern = functools.partial(_ret_kernel, n_chunks=n // C, ctx_chunks=n_ctx // C, skip_chunks=skip, dk=dk)
    return pl.pallas_call(
        kern,
        grid=(bsz, H),
        in_specs=[
            pl.BlockSpec(memory_space=pltpu.SMEM),
            pl.BlockSpec((None, n, dk), lambda b, h: (b, 0, h)),
            pl.BlockSpec((None, n, dk), lambda b, h: (b, 0, H + h)),
            pl.BlockSpec((None, n, dv), lambda b, h: (b, 0, H + h)),
            pl.BlockSpec((None, n, dv), lambda b, h: (b, 0, 2 * H + h)),
            pl.BlockSpec((n, dk // 2), lambda b, h: (0, 0)),
            pl.BlockSpec((n, dk // 2), lambda b, h: (0, 0)),
        ],
        out_specs=pl.BlockSpec((None, n_out, dv), lambda b, h: (b, 0, h)),
        out_shape=jax.ShapeDtypeStruct((bsz, n_out, H * dv), BF16),
        scratch_shapes=[pltpu.VMEM((n, dk), BF16), pltpu.VMEM((n, dk), BF16), pltpu.VMEM((n, dv), F32),
                        pltpu.VMEM((dk, dv), F32), pltpu.VMEM((dk, dv), F32)],
        compiler_params=_cparams("arbitrary", "arbitrary"),
        name="retention",
    )(decay_raw.astype(F32), qkvg, qkvg, qkvg, qkvg, cos, sin)


def _rope_tables(n_ctx, n_lat, dim):
    rows = n_lat // GRID_W
    row = jnp.repeat(jnp.arange(rows, dtype=F32), GRID_W)
    col = jnp.tile(jnp.arange(GRID_W, dtype=F32), rows)
    n_freq = dim // 4
    inv_freq = ROPE_BASE ** (-jnp.arange(n_freq, dtype=F32) / n_freq)
    ang = jnp.concatenate([row[:, None] * inv_freq, col[:, None] * inv_freq], axis=-1)
    cos = jnp.concatenate([jnp.ones((n_ctx, dim // 2), F32), jnp.cos(ang)], axis=0)
    sin = jnp.concatenate([jnp.zeros((n_ctx, dim // 2), F32), jnp.sin(ang)], axis=0)
    return cos, sin


def _conv_kernel(x_ref, w_ref, o_ref, *, n_ctx, n_q_blocks, n_qk_blocks, qscale):
    j = pl.program_id(1)
    x = x_ref[...].astype(F32)
    n, tc = x.shape
    w = w_ref[...]
    pad = GDN_CONV // 2
    t = lax.broadcasted_iota(jnp.int32, (n, 1), 0)
    lo = jnp.where(t < n_ctx, 0, n_ctx)
    hi = jnp.where(t < n_ctx, n_ctx, n)
    acc = x * w[pad:pad + 1]
    for d in range(-pad, pad + 1):
        if d == 0:
            continue
        xs = pltpu.roll(x, (n - d) % n, 0)
        valid = (t + d >= lo) & (t + d < hi)
        acc = acc + jnp.where(valid, xs, 0.0) * w[d + pad:d + pad + 1]
    y = jax.nn.silu(acc)
    do_norm = j < n_qk_blocks
    scale = jnp.where(j < n_q_blocks, qscale, 1.0).astype(F32)
    for g0 in range(0, tc, GDN_DK):
        yg = y[:, g0:g0 + GDN_DK]
        ss = jnp.sum(yg * yg, axis=-1, keepdims=True)
        yn = yg * (lax.rsqrt(ss + 1e-6) * scale)
        o_ref[:, g0:g0 + GDN_DK] = jnp.where(do_norm, yn, yg).astype(o_ref.dtype)


def _gdn_conv(proj, conv_w, *, n_ctx, n_qk_heads, tc=512):
    bsz, n, _ = proj.shape
    width = conv_w.shape[1]
    qw = n_qk_heads * GDN_DK
    assert qw % tc == 0 and width % tc == 0
    kern = functools.partial(_conv_kernel, n_ctx=n_ctx, n_q_blocks=qw // tc, n_qk_blocks=2 * qw // tc,
                             qscale=GDN_DK ** -0.5)
    return pl.pallas_call(
        kern,
        grid=(bsz, width // tc),
        in_specs=[pl.BlockSpec((None, n, tc), lambda b, j: (b, 0, j)),
                  pl.BlockSpec((GDN_CONV, tc), lambda b, j: (0, j))],
        out_specs=pl.BlockSpec((None, n, tc), lambda b, j: (b, 0, j)),
        out_shape=jax.ShapeDtypeStruct((bsz, n, width), BF16),
        compiler_params=_cparams("arbitrary", "arbitrary"),
        name="gdn_conv",
    )(proj, conv_w)


def _gates_kernel(x_ref, alog_ref, dt_ref, o_ref, *, n_heads):
    C = GDN_CHUNK
    x = x_ref[...]
    n = x.shape[0]
    beta = jax.nn.sigmoid(x)
    z = x + dt_ref[...]
    softplus = jnp.maximum(z, 0.0) + jnp.log1p(jnp.exp(-jnp.abs(z)))
    la = -jnp.exp(alog_ref[...]) * softplus
    tm = lax.broadcasted_iota(jnp.int32, (n, 1), 0) % C
    pre = la
    suf = la
    s = 1
    while s < C:
        pre = pre + jnp.where(tm >= s, pltpu.roll(pre, s, 0), 0.0)
        suf = suf + jnp.where(tm < C - s, pltpu.roll(suf, n - s, 0), 0.0)
        s *= 2
    lane = lax.broadcasted_iota(jnp.int32, (1, x.shape[1]), 1)
    o_ref[...] = jnp.where(lane < 2 * n_heads, beta, jnp.where(lane < 3 * n_heads, pre, suf))


def _gdn_gates(ba, a_log, dt_bias):
    bsz, n, w = ba.shape
    n_heads = w // 4
    zeros = jnp.zeros((2 * n_heads,), F32)
    alog = jnp.concatenate([zeros, a_log.astype(F32).reshape(-1)]).reshape(1, w)
    dt = jnp.concatenate([zeros, dt_bias.astype(F32).reshape(-1)]).reshape(1, w)
    return pl.pallas_call(
        functools.partial(_gates_kernel, n_heads=n_heads),
        grid=(bsz,),
        in_specs=[pl.BlockSpec((None, n, w), lambda b: (b, 0, 0)),
                  pl.BlockSpec((1, w), lambda b: (0, 0)), pl.BlockSpec((1, w), lambda b: (0, 0))],
        out_specs=pl.BlockSpec((None, n, w), lambda b: (b, 0, 0)),
        out_shape=jax.ShapeDtypeStruct((bsz, n, w), F32),
        compiler_params=_cparams("arbitrary"),
        name="gdn_gates",
    )(ba, alog, dt)


GDN_INST = 4
GDN_STACK = GDN_INST * GDN_CHUNK


def _unit_tri_inverse(a_list, same16, off32, off64, eye):
    def mm(x, y):
        return _dot(x.astype(BF16), y.astype(BF16))

    ad = [a * same16 for a in a_list]
    x = [eye - d for d in ad]
    p = [mm(d, d) for d in ad]
    for level in range(3):
        xp = [mm(xi, pi) for xi, pi in zip(x, p)]
        if level < 2:
            p = [mm(pi, pi) for pi in p]
        x = [xi + xpi for xi, xpi in zip(x, xp)]
    for off in (off32, off64):
        n = [mm(xi, a * off) for xi, a in zip(x, a_list)]
        x = [xi - mm(ni, xi) for xi, ni in zip(x, n)]
    return x


def _gdn_kernel(q_ref, k_ref, v_ref, z_ref, gcol_ref, grow_ref, nw_ref, o_ref,
                mask_ref, kt_ref, gq_ref, r_ref, o0_ref, of_ref, ob_ref, m_ref,
                *, n_chunks, ctx_chunks, skip_chunks):
    C = GDN_CHUNK
    S = GDN_INST
    W = GDN_STACK
    dk = GDN_DK
    dv = GDN_DV

    r = lax.broadcasted_iota(jnp.int32, (W, W), 0)
    c = lax.broadcasted_iota(jnp.int32, (W, W), 1)
    same = (r // C) == (c // C)
    fwd = r < 2 * C
    ri, cj = r % C, c % C
    bwd = jnp.logical_not(fwd)
    mask_ref[0] = (same & ((fwd & (ri >= cj)) | (bwd & (cj >= ri)))).astype(F32)
    mask_ref[1] = (same & ((fwd & (ri > cj)) | (bwd & (cj > ri)))).astype(F32)
    mask_ref[2] = ((r // 16) == (c // 16)).astype(F32)
    mask_ref[3] = (((r // 32) == (c // 32)) & ((r // 16) != (c // 16))).astype(F32)
    mask_ref[4] = ((r // 32) != (c // 32)).astype(F32)
    mask_ref[5] = (r == c).astype(F32)

    def xpose(p, carry):
        rows = pl.ds(pl.multiple_of(p * 2 * C, 2 * C), 2 * C)
        kt = jnp.transpose(k_ref[rows, :].astype(F32)).astype(BF16)
        kt_ref[2 * p] = kt[:, :C]
        kt_ref[2 * p + 1] = kt[:, C:]
        return carry

    lax.fori_loop(0, n_chunks // 2, xpose, 0)

    def stack4(x):
        return jnp.concatenate([x] * S, axis=0)

    def prep(p, carry):
        incl, strict = mask_ref[0], mask_ref[1]
        chunks = [2 * p, 2 * p + 1]
        st = []
        for ci in chunks:
            rows = pl.ds(pl.multiple_of(ci * C, C), C)
            qb, kb = q_ref[rows, :], k_ref[rows, :]
            kst = stack4(kb)
            prod = _dot_nt(jnp.concatenate([kst, stack4(qb)], axis=0), kst)
            g8 = gcol_ref[rows, :]
            beta = jnp.concatenate([g8[:, s:s + 1] for s in range(S)], axis=0)
            gc = jnp.concatenate([g8[:, S + s:S + s + 1] for s in range(S)], axis=0)
            g_row = grow_ref[ci][1:2, :]
            e = jnp.exp(jnp.minimum(gc - g_row, 0.0))
            a = beta * prod[:W] * (e * strict)
            qkm = (prod[W:] * (e * incl)).astype(BF16)
            st.append(dict(ci=ci, rows=rows, qb=qb, kb=kb, beta=beta, gc=gc, g_row=g_row, a=a, qkm=qkm))

        xs = _unit_tri_inverse([d["a"] for d in st], mask_ref[2], mask_ref[3], mask_ref[4], mask_ref[5])

        for d, x in zip(st, xs):
            ci, rows, beta, gc, g_row = d["ci"], d["rows"], d["beta"], d["gc"], d["g_row"]
            vf = v_ref[rows, :].astype(F32)
            v0, v1 = vf[:, :dv], vf[:, dv:]
            vst = jnp.concatenate([v0, v1, v0, v1], axis=0)
            eg = jnp.exp(gc)
            rhs = jnp.concatenate([vst * beta, stack4(d["kb"].astype(F32)) * (beta * eg)], axis=-1)
            sol = _dot(x.astype(BF16), rhs.astype(BF16)).astype(BF16)
            ow = _dot(d["qkm"], sol)
            qp = stack4(d["qb"].astype(F32)) * eg - ow[:, dv:]
            kt = kt_ref[ci].astype(F32)
            for s in range(S):
                blk = slice(s * C, (s + 1) * C)
                g_s = g_row[:, blk]
                g_last = g_s[:, C - 1:C] if s < 2 else g_s[:, 0:1]
                kdt = (kt * jnp.exp(g_last - g_s)).astype(BF16)
                rg = _dot(kdt, sol[blk, :])
                gq_ref[s, ci, 0:dk, :] = rg[:, dv:].astype(BF16)
                gq_ref[s, ci, dk:dk + C, :] = qp[blk].astype(BF16)
                r_ref[s, ci] = rg[:, :dv].astype(BF16)
                o0_ref[s, ci] = ow[blk, :dv].astype(BF16)
        return carry

    lax.fori_loop(0, n_chunks // 2, prep, 0)

    m_ref[...] = jnp.zeros_like(m_ref)

    def step(t, carry):
        cf = t
        cb = jnp.where(t < ctx_chunks, ctx_chunks - 1 - t, n_chunks - 1 + ctx_chunks - t)
        cis = [cf, cf, cb, cb]
        ms = [m_ref[s] for s in range(S)]
        pq = [_dot(gq_ref[s, cis[s]], ms[s].astype(BF16)) for s in range(S)]
        for s in range(S):
            ci = cis[s]
            g_s = grow_ref[ci][1:2, s * C:(s + 1) * C]
            g_last = g_s[:, C - 1:C] if s < 2 else g_s[:, 0:1]
            m_ref[s] = jnp.exp(g_last) * ms[s] - pq[s][:dk] + r_ref[s, ci].astype(F32)
            o = pq[s][dk:] + o0_ref[s, ci].astype(F32)
            rows = pl.ds(pl.multiple_of(ci * C, C), C)
            if s < 2:
                of_ref[s, rows, :] = o
            else:
                ob_ref[s - 2, rows, :] = o
        return carry

    lax.fori_loop(0, n_chunks, step, 0)

    nw = nw_ref[...]

    def emit(ci, carry):
        rows = pl.ds(pl.multiple_of(ci * C, C), C)
        out_rows = pl.ds(pl.multiple_of((ci - skip_chunks) * C, C), C)
        for vj in range(2):
            o = of_ref[vj, rows, :] + ob_ref[vj, rows, :]
            zf = z_ref[rows, vj * dv:(vj + 1) * dv].astype(F32)
            on = o * lax.rsqrt(jnp.mean(o * o, axis=-1, keepdims=True) + 1e-6) * nw * jax.nn.silu(zf)
            o_ref[out_rows, vj * dv:(vj + 1) * dv] = on.astype(o_ref.dtype)
        return carry

    lax.fori_loop(skip_chunks, n_chunks, emit, 0)


def _gdn_scan(qkv, proj, gates, norm_w, *, n_ctx, need_ctx, n_qk_heads, z_col_off):
    bsz, n, _ = qkv.shape
    C = GDN_CHUNK
    HQ = n_qk_heads
    HV = gates.shape[-1] // 4
    assert HV == 2 * HQ and n % (2 * C) == 0 and n_ctx % C == 0
    nch = n // C
    skip = 0 if need_ctx else n_ctx // C
    n_out = n - skip * C
    g5 = gates.reshape(bsz, n, 4, HQ, 2)
    gcol = g5.transpose(0, 3, 1, 2, 4).reshape(bsz, HQ, n, 2 * GDN_INST)
    grow = gates.reshape(bsz, nch, C, 2, 2, HQ, 2).transpose(0, 5, 1, 3, 4, 6, 2).reshape(bsz, HQ, nch, 2, GDN_STACK)
    pair = 2 * GDN_DV
    zb = z_col_off // pair
    kern = functools.partial(_gdn_kernel, n_chunks=nch, ctx_chunks=n_ctx // C, skip_chunks=skip)
    return pl.pallas_call(
        kern,
        grid=(bsz, HQ),
        in_specs=[
            pl.BlockSpec((None, n, GDN_DK), lambda b, h: (b, 0, h)),
            pl.BlockSpec((None, n, GDN_DK), lambda b, h: (b, 0, HQ + h)),
            pl.BlockSpec((None, n, pair), lambda b, h: (b, 0, HQ + h)),
            pl.BlockSpec((None, n, pair), lambda b, h: (b, 0, zb + h)),
            pl.BlockSpec((None, None, n, 2 * GDN_INST), lambda b, h: (b, h, 0, 0)),
            pl.BlockSpec((None, None, nch, 2, GDN_STACK), lambda b, h: (b, h, 0, 0, 0)),
            pl.BlockSpec((1, GDN_DV), lambda b, h: (0, 0)),
        ],
        out_specs=pl.BlockSpec((None, n_out, pair), lambda b, h: (b, 0, h)),
        out_shape=jax.ShapeDtypeStruct((bsz, n_out, HV * GDN_DV), BF16),
        scratch_shapes=[pltpu.VMEM((6, GDN_STACK, GDN_STACK), F32),
                        pltpu.VMEM((nch, GDN_DK, C), BF16),
                        pltpu.VMEM((GDN_INST, nch, GDN_DK + C, GDN_DV), BF16),
                        pltpu.VMEM((GDN_INST, nch, GDN_DK, GDN_DV), BF16),
                        pltpu.VMEM((GDN_INST, nch, C, GDN_DV), BF16),
                        pltpu.VMEM((2, n, GDN_DV), F32), pltpu.VMEM((2, n, GDN_DV), F32),
                        pltpu.VMEM((GDN_INST, GDN_DK, GDN_DV), F32)],
        compiler_params=_cparams("arbitrary", "arbitrary"),
        name="gdn_scan",
    )(qkv, qkv, qkv, proj, gcol, grow, norm_w.astype(F32).reshape(1, GDN_DV))


def kernel(x, c, ctx, c_ctx, ada_w, ada_b, ln_g, ln_b, ret_w_in, ret_decay, ret_w_out,
           gdn_w_in, gdn_conv, gdn_a_log, gdn_dt_bias, gdn_norm, gdn_w_out, ffn_w_in, ffn_w_out):
    bsz, n_lat, d = x.shape
    n_ctx = ctx.shape[1]
    depth = ada_w.shape[0]
    n_tot = n_ctx + n_lat
    alpha = (2 * depth) ** 0.25

    pad_rows = (-(bsz + 1)) % 16
    cc = jnp.concatenate([c, c_ctx[None], jnp.zeros((pad_rows, d), F32)], axis=0)
    mods = _ada_mod(cc, ada_w, ada_b)

    def layer_params(i):
        lat = mods[i, :bsz].reshape(bsz, 1, 6, d)
        cx = jnp.broadcast_to(mods[i, bsz].reshape(1, 1, 6, d), (bsz, 1, 6, d))
        return jnp.concatenate([cx, lat], axis=1)

    params = [layer_params(i) for i in range(depth)]
    h = jnp.concatenate([ctx, x], axis=1)
    u = _modulate(h, params[0], shift=0, scale=1, n_ctx=n_ctx)
    ctx_rows = n_ctx

    for i in range(depth):
        need_ctx = i < depth - 1
        j = i // 2
        u2 = u.reshape(bsz * n_tot, d)
        if i % 2 == 0:
            qkvg = _matmul(u2, ret_w_in[j], n_out=ret_w_in.shape[-1], tn=1024, out_dtype=BF16,
                           name="ret_in_proj").reshape(bsz, n_tot, -1)
            dk = qkvg.shape[-1] // (6 * RET_HEADS)
            cos, sin = _rope_tables(n_ctx, n_lat, dk)
            o = _retention(qkvg, ret_decay[j], cos, sin, n_ctx=n_ctx, need_ctx=need_ctx)
            w_out = ret_w_out[j]
        else:
            w_in = gdn_w_in[j]
            qkv_w = gdn_conv.shape[-1]
            hv = gdn_a_log.shape[-1]
            main_w = qkv_w + hv * GDN_DV
            proj = _matmul(u2, w_in, n_out=main_w, tn=1024, out_dtype=BF16,
                           name="gdn_in_proj").reshape(bsz, n_tot, main_w)
            ba = _matmul(u2, w_in, n_out=4 * hv, col_off=main_w, tn=4 * hv, out_dtype=F32,
                         name="gdn_gate_proj").reshape(bsz, n_tot, 4 * hv)
            gates = _gdn_gates(ba, gdn_a_log[j], gdn_dt_bias[j])
            qkv = _gdn_conv(proj, gdn_conv[j], n_ctx=n_ctx, n_qk_heads=hv // 2)
            o = _gdn_scan(qkv, proj, gates, gdn_norm[j], n_ctx=n_ctx, need_ctx=need_ctx,
                          n_qk_heads=hv // 2, z_col_off=qkv_w)
            w_out = gdn_w_out[j]

        n_rows = o.shape[1]
        out_ctx = n_ctx if need_ctx else 0
        y = _matmul(o.reshape(bsz * n_rows, -1), w_out, n_out=d, tn=512, out_dtype=F32,
                    name="mixer_out_proj").reshape(bsz, n_rows, d)
        h, u = _resid_ln(h, y, params[i], ln_g[i, 0], ln_b[i, 0], params[i], alpha=alpha, gate=2,
                         shift=3, scale=4, n_ctx=out_ctx, h_row_offset=ctx_rows - out_ctx)
        ctx_rows = out_ctx
        y = _ffn(u, ffn_w_in[i], ffn_w_out[i])
        if need_ctx:
            h, u = _resid_ln(h, y, params[i], ln_g[i, 1], ln_b[i, 1], params[i + 1], alpha=alpha, gate=5,
                             shift=0, scale=1, n_ctx=out_ctx)
        else:
            h = _resid_ln(h, y, params[i], ln_g[i, 1], ln_b[i, 1], params[i], alpha=alpha, gate=5,
                          shift=0, scale=1, n_ctx=out_ctx, with_u=False)
    return h
```

```python
import functools
import math

import jax
import jax.numpy as jnp
from jax import lax
from jax.experimental import pallas as pl
from jax.experimental.pallas import tpu as pltpu

F32 = jnp.float32
BF16 = jnp.bfloat16

GRID_W = 64
RET_HEADS = 8
RET_CHUNK = 128
ROPE_BASE = 10000.0
GDN_DK = 128
GDN_DV = 128
GDN_CONV = 5
GDN_CHUNK = 64
LN_EPS = 1e-5

VMEM_LIMIT_BYTES = 56 * 1024 * 1024
ROW_TILE = 256
MM_ROW_TILE = 1024


def _cparams(*sem):
    return pltpu.CompilerParams(dimension_semantics=sem, vmem_limit_bytes=VMEM_LIMIT_BYTES)


def _dot(a, b):
    return jnp.dot(a, b, preferred_element_type=F32)


def _dot_nt(a, b):
    return lax.dot_general(a, b, (((1,), (1,)), ((), ())), preferred_element_type=F32)


def _ada_kernel(c_ref, w_ref, b_ref, o_ref):
    a = jax.nn.silu(c_ref[...]).astype(BF16)
    o_ref[...] = _dot(a, w_ref[...].astype(BF16)) + b_ref[...]


def _ada_mod(cc, ada_w, ada_b, tn=1024):
    depth, d, n = ada_w.shape
    rows = cc.shape[0]
    return pl.pallas_call(
        _ada_kernel,
        grid=(depth, n // tn),
        in_specs=[
            pl.BlockSpec((rows, d), lambda l, j: (0, 0)),
            pl.BlockSpec((None, d, tn), lambda l, j: (l, 0, j)),
            pl.BlockSpec((None, 1, tn), lambda l, j: (l, 0, j)),
        ],
        out_specs=pl.BlockSpec((None, rows, tn), lambda l, j: (l, 0, j)),
        out_shape=jax.ShapeDtypeStruct((depth, rows, n), F32),
        compiler_params=_cparams("arbitrary", "arbitrary"),
        name="ada_mod",
    )(cc, ada_w, ada_b.reshape(depth, 1, n))


def _mod_kernel(h_ref, p_ref, u_ref, *, shift, scale):
    p = p_ref[...]
    u_ref[...] = (h_ref[...] * (1.0 + p[scale:scale + 1]) + p[shift:shift + 1]).astype(u_ref.dtype)


def _param_spec(d, seg_of_tile):
    return pl.BlockSpec((None, None, 6, d), lambda b, r: (b, seg_of_tile(r), 0, 0))


def _modulate(h, p, *, shift, scale, n_ctx):
    bsz, n, d = h.shape
    ctx_tiles = n_ctx // ROW_TILE
    seg = lambda r: jnp.where(r < ctx_tiles, 0, 1)
    return pl.pallas_call(
        functools.partial(_mod_kernel, shift=shift, scale=scale),
        grid=(bsz, n // ROW_TILE),
        in_specs=[pl.BlockSpec((None, ROW_TILE, d), lambda b, r: (b, r, 0)), _param_spec(d, seg)],
        out_specs=pl.BlockSpec((None, ROW_TILE, d), lambda b, r: (b, r, 0)),
        out_shape=jax.ShapeDtypeStruct((bsz, n, d), BF16),
        compiler_params=_cparams("arbitrary", "arbitrary"),
        name="modulate",
    )(h, p)


def _resid_ln_kernel(h_ref, y_ref, p_ref, g_ref, b_ref, p2_ref, h_out, *u_out, alpha, gate, shift, scale):
    p = p_ref[...]
    x = alpha * h_ref[...] + p[gate:gate + 1] * y_ref[...].astype(F32)
    mu = jnp.mean(x, axis=-1, keepdims=True)
    xc = x - mu
    var = jnp.mean(xc * xc, axis=-1, keepdims=True)
    hn = xc * lax.rsqrt(var + LN_EPS) * g_ref[...] + b_ref[...]
    h_out[...] = hn
    if u_out:
        p2 = p2_ref[...]
        u_out[0][...] = (hn * (1.0 + p2[scale:scale + 1]) + p2[shift:shift + 1]).astype(BF16)


def _resid_ln(h, y, p, ln_g, ln_b, p2, *, alpha, gate, shift, scale, n_ctx, h_row_offset=0, with_u=True):
    bsz, n, d = y.shape
    ctx_tiles = n_ctx // ROW_TILE
    off = h_row_offset // ROW_TILE
    seg = lambda r: jnp.where(r < ctx_tiles, 0, 1)
    row = pl.BlockSpec((None, ROW_TILE, d), lambda b, r: (b, r, 0))
    vec = pl.BlockSpec((1, d), lambda b, r: (0, 0))
    out_shape = [jax.ShapeDtypeStruct((bsz, n, d), F32)]
    out_specs = [row]
    if with_u:
        out_shape.append(jax.ShapeDtypeStruct((bsz, n, d), BF16))
        out_specs.append(row)
    outs = pl.pallas_call(
        functools.partial(_resid_ln_kernel, alpha=alpha, gate=gate, shift=shift, scale=scale),
        grid=(bsz, n // ROW_TILE),
        in_specs=[pl.BlockSpec((None, ROW_TILE, d), lambda b, r: (b, r + off, 0)), row,
                  _param_spec(d, seg), vec, vec, _param_spec(d, seg)],
        out_specs=out_specs,
        out_shape=out_shape,
        compiler_params=_cparams("arbitrary", "arbitrary"),
        name="resid_ln",
    )(h, y, p, ln_g.reshape(1, d), ln_b.reshape(1, d), p2)
    return outs if with_u else outs[0]


def _mm_kernel(a_ref, w_ref, o_ref, wb_ref):
    @pl.when(pl.program_id(1) == 0)
    def _():
        wb_ref[...] = w_ref[...].astype(BF16)

    o_ref[...] = _dot(a_ref[...], wb_ref[...]).astype(o_ref.dtype)


def _matmul(a, w, layer, *, n_out, col_off=0, tn, tm=MM_ROW_TILE, out_dtype, name):
    m, k = a.shape
    assert m % tm == 0 and n_out % tn == 0 and col_off % tn == 0
    off = col_off // tn
    return pl.pallas_call(
        _mm_kernel,
        grid=(n_out // tn, m // tm),
        in_specs=[pl.BlockSpec((tm, k), lambda j, i: (i, 0)),
                  pl.BlockSpec((None, k, tn), lambda j, i: (layer, 0, j + off))],
        out_specs=pl.BlockSpec((tm, tn), lambda j, i: (i, j)),
        out_shape=jax.ShapeDtypeStruct((m, n_out), out_dtype),
        scratch_shapes=[pltpu.VMEM((k, tn), BF16)],
        compiler_params=_cparams("arbitrary", "arbitrary"),
        name=name,
    )(a, w)


def _swiglu_kernel(a_ref, wg_ref, wu_ref, o_ref, wgb_ref, wub_ref):
    @pl.when(pl.program_id(1) == 0)
    def _():
        wgb_ref[...] = wg_ref[...].astype(BF16)
        wub_ref[...] = wu_ref[...].astype(BF16)

    a = a_ref[...]
    gate = _dot(a, wgb_ref[...])
    up = _dot(a, wub_ref[...])
    o_ref[...] = (jax.nn.silu(gate) * up).astype(o_ref.dtype)


def _swiglu_in(a, w, layer, *, tn=512, tm=MM_ROW_TILE):
    m, k = a.shape
    f = w.shape[2] // 2
    assert m % tm == 0 and f % tn == 0
    nb = f // tn
    return pl.pallas_call(
        _swiglu_kernel,
        grid=(nb, m // tm),
        in_specs=[pl.BlockSpec((tm, k), lambda j, i: (i, 0)),
                  pl.BlockSpec((None, k, tn), lambda j, i: (layer, 0, j)),
                  pl.BlockSpec((None, k, tn), lambda j, i: (layer, 0, j + nb))],
        out_specs=pl.BlockSpec((tm, tn), lambda j, i: (i, j)),
        out_shape=jax.ShapeDtypeStruct((m, f), BF16),
        scratch_shapes=[pltpu.VMEM((k, tn), BF16), pltpu.VMEM((k, tn), BF16)],
        compiler_params=_cparams("arbitrary", "arbitrary"),
        name="ffn_in_swiglu",
    )(a, w, w)


def _ffn(u, w_in, w_out, layer):
    bsz, n, d = u.shape
    a = _swiglu_in(u.reshape(bsz * n, d), w_in, layer)
    y = _matmul(a, w_out, layer, n_out=d, tn=512, tm=512, out_dtype=F32, name="ffn_out")
    return y.reshape(bsz, n, d)


def _ret_kernel(draw_ref, q_ref, k_ref, v_ref, g_ref, cos_ref, sin_ref, o_ref,
                qs_ref, ks_ref, oacc_ref, sf_ref, sb_ref, *, n_chunks, ctx_chunks, skip_chunks, dk):
    C = RET_CHUNK
    h = pl.program_id(1)
    half = dk // 2
    lg_f = -jnp.exp(jnp.full((1, 1), draw_ref[0, h], F32))
    lg_b = -jnp.exp(jnp.full((1, 1), draw_ref[1, h], F32))

    ri = lax.broadcasted_iota(jnp.int32, (C, C), 0)
    ci = lax.broadcasted_iota(jnp.int32, (C, C), 1)
    diff = (ri - ci).astype(F32)
    lower = diff >= 0
    dmat = jnp.where(lower, jnp.exp(lg_f * jnp.where(lower, diff, 0.0)),
                     jnp.exp(lg_b * jnp.where(lower, 0.0, -diff)))
    rc = lax.broadcasted_iota(jnp.int32, (C, 1), 0).astype(F32)
    qdec_f = jnp.exp(lg_f * (rc + 1.0))
    kdec_f = jnp.exp(lg_f * (C - 1.0 - rc))
    cdec_f = jnp.exp(lg_f * C)
    qdec_b = jnp.exp(lg_b * (C - rc))
    kdec_b = jnp.exp(lg_b * rc)
    cdec_b = jnp.exp(lg_b * C)

    sf_ref[...] = jnp.zeros_like(sf_ref)
    sb_ref[...] = jnp.zeros_like(sb_ref)
    qscale = dk ** -0.5

    def rope(x, cs, sn):
        x1, x2 = x[:, :half], x[:, half:]
        return jnp.concatenate([x1 * cs - x2 * sn, x1 * sn + x2 * cs], axis=-1)

    def fwd(c, carry):
        rows = pl.ds(pl.multiple_of(c * C, C), C)
        cs, sn = cos_ref[rows, :], sin_ref[rows, :]
        q = rope(q_ref[rows, :].astype(F32), cs, sn) * qscale
        k = rope(k_ref[rows, :].astype(F32), cs, sn)
        qb, kb = q.astype(BF16), k.astype(BF16)
        qs_ref[rows, :] = qb
        ks_ref[rows, :] = kb
        v = v_ref[rows, :]
        s = (_dot_nt(qb, kb) * dmat).astype(BF16)
        state = sf_ref[...]
        oacc_ref[rows, :] = _dot(s, v) + _dot(qb, state.astype(BF16)) * qdec_f
        kd_t = jnp.transpose(k * kdec_f).astype(BF16)
        sf_ref[...] = cdec_f * state + _dot(kd_t, v)
        return carry

    lax.fori_loop(0, n_chunks, fwd, 0)

    def bwd(t, carry):
        c = jnp.where(t < ctx_chunks, ctx_chunks - 1 - t, n_chunks - 1 + ctx_chunks - t)
        rows = pl.ds(pl.multiple_of(c * C, C), C)
        qb, kb, v = qs_ref[rows, :], ks_ref[rows, :], v_ref[rows, :]
        state = sb_ref[...]
        o = oacc_ref[rows, :] + _dot(qb, state.astype(BF16)) * qdec_b
        kd_t = jnp.transpose(kb.astype(F32) * kdec_b).astype(BF16)
        sb_ref[...] = cdec_b * state + _dot(kd_t, v)

        def emit():
            mu = jnp.mean(o, axis=-1, keepdims=True)
            oc = o - mu
            var = jnp.mean(oc * oc, axis=-1, keepdims=True)
            on = oc * lax.rsqrt(var + LN_EPS)
            out_rows = pl.ds(pl.multiple_of((c - skip_chunks) * C, C), C)
            o_ref[out_rows, :] = (on * jax.nn.silu(g_ref[rows, :].astype(F32))).astype(o_ref.dtype)

        if skip_chunks:
            pl.when(c >= skip_chunks)(emit)
        else:
            emit()
        return carry

    lax.fori_loop(0, n_chunks, bwd, 0)


def _retention(qkvg, decay_raw, cos, sin, *, n_ctx, need_ctx):
    bsz, n, width = qkvg.shape
    H = RET_HEADS
    dk = width // (6 * H)
    dv = 2 * dk
    C = RET_CHUNK
    assert n % C == 0 and n_ctx % C == 0
    skip = 0 if need_ctx else n_ctx // C
    n_out = n - skip * C
    kern = functools.partial(_ret_kernel, n_chunks=n // C, ctx_chunks=n_ctx // C, skip_chunks=skip, dk=dk)
    return pl.pallas_call(
        kern,
        grid=(bsz, H),
        in_specs=[
            pl.BlockSpec(memory_space=pltpu.SMEM),
            pl.BlockSpec((None, n, dk), lambda b, h: (b, 0, h)),
            pl.BlockSpec((None, n, dk), lambda b, h: (b, 0, H + h)),
            pl.BlockSpec((None, n, dv), lambda b, h: (b, 0, H + h)),
            pl.BlockSpec((None, n, dv), lambda b, h: (b, 0, 2 * H + h)),
            pl.BlockSpec((n, dk // 2), lambda b, h: (0, 0)),
            pl.BlockSpec((n, dk // 2), lambda b, h: (0, 0)),
        ],
        out_specs=pl.BlockSpec((None, n_out, dv), lambda b, h: (b, 0, h)),
        out_shape=jax.ShapeDtypeStruct((bsz, n_out, H * dv), BF16),
        scratch_shapes=[pltpu.VMEM((n, dk), BF16), pltpu.VMEM((n, dk), BF16), pltpu.VMEM((n, dv), F32),
                        pltpu.VMEM((dk, dv), F32), pltpu.VMEM((dk, dv), F32)],
        compiler_params=_cparams("arbitrary", "arbitrary"),
        name="retention",
    )(decay_raw.astype(F32), qkvg, qkvg, qkvg, qkvg, cos, sin)


def _rope_tables(n_ctx, n_lat, dim):
    rows = n_lat // GRID_W
    row = jnp.repeat(jnp.arange(rows, dtype=F32), GRID_W)
    col = jnp.tile(jnp.arange(GRID_W, dtype=F32), rows)
    n_freq = dim // 4
    inv_freq = ROPE_BASE ** (-jnp.arange(n_freq, dtype=F32) / n_freq)
    ang = jnp.concatenate([row[:, None] * inv_freq, col[:, None] * inv_freq], axis=-1)
    cos = jnp.concatenate([jnp.ones((n_ctx, dim // 2), F32), jnp.cos(ang)], axis=0)
    sin = jnp.concatenate([jnp.zeros((n_ctx, dim // 2), F32), jnp.sin(ang)], axis=0)
    return cos, sin


def _conv_kernel(x_ref, w_ref, o_ref, *, n_ctx, n_q_blocks, n_qk_blocks, qscale):
    j = pl.program_id(1)
    x = x_ref[...].astype(F32)
    n, tc = x.shape
    w = w_ref[...]
    pad = GDN_CONV // 2
    t = lax.broadcasted_iota(jnp.int32, (n, 1), 0)
    lo = jnp.where(t < n_ctx, 0, n_ctx)
    hi = jnp.where(t < n_ctx, n_ctx, n)
    acc = x * w[pad:pad + 1]
    for d in range(-pad, pad + 1):
        if d == 0:
            continue
        xs = pltpu.roll(x, (n - d) % n, 0)
        valid = (t + d >= lo) & (t + d < hi)
        acc = acc + jnp.where(valid, xs, 0.0) * w[d + pad:d + pad + 1]
    y = jax.nn.silu(acc)
    do_norm = j < n_qk_blocks
    scale = jnp.where(j < n_q_blocks, qscale, 1.0).astype(F32)
    for g0 in range(0, tc, GDN_DK):
        yg = y[:, g0:g0 + GDN_DK]
        ss = jnp.sum(yg * yg, axis=-1, keepdims=True)
        yn = yg * (lax.rsqrt(ss + 1e-6) * scale)
        o_ref[:, g0:g0 + GDN_DK] = jnp.where(do_norm, yn, yg).astype(o_ref.dtype)


def _gdn_conv(proj, conv_w, *, n_ctx, n_qk_heads, tc=512):
    bsz, n, _ = proj.shape
    width = conv_w.shape[1]
    qw = n_qk_heads * GDN_DK
    assert qw % tc == 0 and width % tc == 0
    kern = functools.partial(_conv_kernel, n_ctx=n_ctx, n_q_blocks=qw // tc, n_qk_blocks=2 * qw // tc,
                             qscale=GDN_DK ** -0.5)
    return pl.pallas_call(
        kern,
        grid=(bsz, width // tc),
        in_specs=[pl.BlockSpec((None, n, tc), lambda b, j: (b, 0, j)),
                  pl.BlockSpec((GDN_CONV, tc), lambda b, j: (0, j))],
        out_specs=pl.BlockSpec((None, n, tc), lambda b, j: (b, 0, j)),
        out_shape=jax.ShapeDtypeStruct((bsz, n, width), BF16),
        compiler_params=_cparams("arbitrary", "arbitrary"),
        name="gdn_conv",
    )(proj, conv_w)


def _gates_kernel(x_ref, alog_ref, dt_ref, o_ref, *, n_heads):
    C = GDN_CHUNK
    x = x_ref[...]
    n = x.shape[0]
    beta = jax.nn.sigmoid(x)
    z = x + dt_ref[...]
    softplus = jnp.maximum(z, 0.0) + jnp.log1p(jnp.exp(-jnp.abs(z)))
    la = -jnp.exp(alog_ref[...]) * softplus
    tm = lax.broadcasted_iota(jnp.int32, (n, 1), 0) % C
    pre = la
    suf = la
    s = 1
    while s < C:
        pre = pre + jnp.where(tm >= s, pltpu.roll(pre, s, 0), 0.0)
        suf = suf + jnp.where(tm < C - s, pltpu.roll(suf, n - s, 0), 0.0)
        s *= 2
    lane = lax.broadcasted_iota(jnp.int32, (1, x.shape[1]), 1)
    o_ref[...] = jnp.where(lane < 2 * n_heads, beta, jnp.where(lane < 3 * n_heads, pre, suf))


def _gdn_gates(ba, a_log, dt_bias):
    bsz, n, w = ba.shape
    n_heads = w // 4
    zeros = jnp.zeros((2 * n_heads,), F32)
    alog = jnp.concatenate([zeros, a_log.astype(F32).reshape(-1)]).reshape(1, w)
    dt = jnp.concatenate([zeros, dt_bias.astype(F32).reshape(-1)]).reshape(1, w)
    return pl.pallas_call(
        functools.partial(_gates_kernel, n_heads=n_heads),
        grid=(bsz,),
        in_specs=[pl.BlockSpec((None, n, w), lambda b: (b, 0, 0)),
                  pl.BlockSpec((1, w), lambda b: (0, 0)), pl.BlockSpec((1, w), lambda b: (0, 0))],
        out_specs=pl.BlockSpec((None, n, w), lambda b: (b, 0, 0)),
        out_shape=jax.ShapeDtypeStruct((bsz, n, w), F32),
        compiler_params=_cparams("arbitrary"),
        name="gdn_gates",
    )(ba, alog, dt)


GDN_INST = 4
GDN_STACK = GDN_INST * GDN_CHUNK
GDN_PREP_CHUNKS = 6
GDN_EMIT_ROWS = 256


GDN_INV_BASE = 8
GDN_INV_LEVELS = (16, 32, 64)


def _merge_rows(g, lower):
    h = g // 2
    return [(r0 + h, r0 + g) if lower else (r0, r0 + h) for r0 in range(0, GDN_CHUNK, g)]


def _unit_tri_inverse(a_list, lower_list, same_base, off_masks, eye):
    def mm(x, y):
        return _dot(x.astype(BF16), y.astype(BF16))

    ad = [a * same_base for a in a_list]
    x = [eye - d for d in ad]
    p = [mm(d, d) for d in ad]
    xp = [mm(xi, pi) for xi, pi in zip(x, p)]
    p = [mm(pi, pi) for pi in p]
    x = [xi + xpi for xi, xpi in zip(x, xp)]
    x = [xi + mm(xi, pi) for xi, pi in zip(x, p)]
    for g, off in zip(GDN_INV_LEVELS, off_masks):
        h = g // 2
        sels = [_merge_rows(g, lower) for lower in lower_list]
        top = [jnp.concatenate([xi[a:b] for a, b in sel], axis=0) for xi, sel in zip(x, sels)]
        n = [mm(ti, a * off) for ti, a in zip(top, a_list)]
        upd = [ti - mm(ni, xi) for ti, ni, xi in zip(top, n, x)]
        merged = []
        for xi, ui, sel in zip(x, upd, sels):
            pieces = []
            for idx, (a, b) in enumerate(sel):
                new = ui[idx * h:(idx + 1) * h]
                r0 = (a // g) * g
                pieces += [xi[r0:r0 + h], new] if a != r0 else [new, xi[r0 + h:r0 + g]]
            merged.append(jnp.concatenate(pieces, axis=0))
        x = merged
    return x


def _gdn_kernel(q_ref, k_ref, v_ref, z_ref, gcol_ref, grow_ref, nw_ref, o_ref,
                mask_ref, kt_ref, gq_ref, r_ref, o0_ref, of_ref, ob_ref, m_ref,
                *, n_chunks, ctx_chunks, skip_chunks):
    C = GDN_CHUNK
    S = GDN_INST
    dk = GDN_DK
    dv = GDN_DV
    n_lvl = len(GDN_INV_LEVELS)
    M_INCL, M_STRICT, M_BASE, M_OFF, M_EYE = 0, 2, 4, 5, 5 + n_lvl

    r = lax.broadcasted_iota(jnp.int32, (C, C), 0)
    c = lax.broadcasted_iota(jnp.int32, (C, C), 1)
    mask_ref[M_INCL] = (r >= c).astype(F32)
    mask_ref[M_INCL + 1] = (c >= r).astype(F32)
    mask_ref[M_STRICT] = (r > c).astype(F32)
    mask_ref[M_STRICT + 1] = (c > r).astype(F32)
    mask_ref[M_BASE] = ((r // GDN_INV_BASE) == (c // GDN_INV_BASE)).astype(F32)
    for lvl, g in enumerate(GDN_INV_LEVELS):
        h = g // 2
        mask_ref[M_OFF + lvl] = (((r // g) == (c // g)) & ((r // h) != (c // h))).astype(F32)
    mask_ref[M_EYE] = (r == c).astype(F32)

    def xpose(p, carry):
        rows = pl.ds(pl.multiple_of(p * 2 * C, 2 * C), 2 * C)
        kt = jnp.transpose(k_ref[rows, :].astype(F32)).astype(BF16)
        kt_ref[2 * p] = kt[:, :C]
        kt_ref[2 * p + 1] = kt[:, C:]
        return carry

    lax.fori_loop(0, n_chunks // 2, xpose, 0, unroll=3)

    def prep(p, carry):
        items = []
        for i in range(GDN_PREP_CHUNKS):
            ci = GDN_PREP_CHUNKS * p + i
            rows = pl.ds(pl.multiple_of(ci * C, C), C)
            qb, kb = q_ref[rows, :], k_ref[rows, :]
            pk = _dot_nt(jnp.concatenate([kb, qb], axis=0), kb)
            kk, qk = pk[:C], pk[C:]
            g8 = gcol_ref[rows, :]
            grow8 = grow_ref[ci]
            qf, kf, vf = qb.astype(F32), kb.astype(F32), v_ref[rows, :].astype(F32)
            kt = kt_ref[ci].astype(F32)
            for s in range(S):
                d, vj = s // 2, s % 2
                beta, gc = g8[:, s:s + 1], g8[:, S + s:S + s + 1]
                g_row = grow8[S + s:S + s + 1, :]
                e = jnp.exp(jnp.minimum(gc - g_row, 0.0))
                a = beta * kk * (e * mask_ref[M_STRICT + d])
                qkm = (qk * (e * mask_ref[M_INCL + d])).astype(BF16)
                eg = jnp.exp(gc)
                rhs = jnp.concatenate([vf[:, vj * dv:(vj + 1) * dv] * beta, kf * (beta * eg)], axis=-1)
                g_last = g_row[:, 0:1] if d else g_row[:, C - 1:C]
                kdt = (kt * jnp.exp(g_last - g_row)).astype(BF16)
                items.append(dict(ci=ci, s=s, a=a, qkm=qkm, rhs=rhs.astype(BF16), qd=qf * eg, kdt=kdt))

        xs = _unit_tri_inverse([it["a"] for it in items], [it["s"] < 2 for it in items], mask_ref[M_BASE],
                               [mask_ref[M_OFF + i] for i in range(n_lvl)], mask_ref[M_EYE])
        sols = [_dot(x.astype(BF16), it["rhs"]).astype(BF16) for x, it in zip(xs, items)]
        ows = [_dot(it["qkm"], sol) for it, sol in zip(items, sols)]
        rgs = [_dot(it["kdt"], sol) for it, sol in zip(items, sols)]
        for it, ow, rg in zip(items, ows, rgs):
            s, ci = it["s"], it["ci"]
            gq_ref[s, ci, 0:dk, :] = rg[:, dv:].astype(BF16)
            gq_ref[s, ci, dk:dk + C, :] = (it["qd"] - ow[:, dv:]).astype(BF16)
            r_ref[s, ci] = rg[:, :dv].astype(BF16)
            o0_ref[s, ci] = ow[:, :dv].astype(BF16)
        return carry

    lax.fori_loop(0, n_chunks // GDN_PREP_CHUNKS, prep, 0)

    m_ref[...] = jnp.zeros_like(m_ref)

    def step(t, carry):
        cf = t
        cb = jnp.where(t < ctx_chunks, ctx_chunks - 1 - t, n_chunks - 1 + ctx_chunks - t)
        cis = [cf, cf, cb, cb]
        ms = [m_ref[s] for s in range(S)]
        pq = [_dot(gq_ref[s, cis[s]], ms[s].astype(BF16)) for s in range(S)]
        for s in range(S):
            ci = cis[s]
            g_s = grow_ref[ci][S + s:S + s + 1, :]
            g_last = g_s[:, C - 1:C] if s < 2 else g_s[:, 0:1]
            m_ref[s] = jnp.exp(g_last) * ms[s] - pq[s][:dk] + r_ref[s, ci].astype(F32)
            o = pq[s][dk:] + o0_ref[s, ci].astype(F32)
            rows = pl.ds(pl.multiple_of(ci * C, C), C)
            if s < 2:
                of_ref[s, rows, :] = o
            else:
                ob_ref[s - 2, rows, :] = o
        return carry

    lax.fori_loop(0, n_chunks, step, 0, unroll=2)

    nw = nw_ref[...]

    E = GDN_EMIT_ROWS

    def emit(i, carry):
        rows = pl.ds(pl.multiple_of(i * E, E), E)
        out_rows = pl.ds(pl.multiple_of(i * E - skip_chunks * C, E), E)
        for vj in range(2):
            o = of_ref[vj, rows, :] + ob_ref[vj, rows, :]
            zf = z_ref[rows, vj * dv:(vj + 1) * dv].astype(F32)
            on = o * lax.rsqrt(jnp.mean(o * o, axis=-1, keepdims=True) + 1e-6) * nw * jax.nn.silu(zf)
            o_ref[out_rows, vj * dv:(vj + 1) * dv] = on.astype(o_ref.dtype)
        return carry

    lax.fori_loop(skip_chunks * C // E, n_chunks * C // E, emit, 0)


def _gdn_scan(qkv, proj, gates, norm_w, *, n_ctx, need_ctx, n_qk_heads, z_col_off):
    bsz, n, _ = qkv.shape
    C = GDN_CHUNK
    HQ = n_qk_heads
    HV = gates.shape[-1] // 4
    assert HV == 2 * HQ and n % (GDN_PREP_CHUNKS * C) == 0 and n_ctx % C == 0
    assert n % GDN_EMIT_ROWS == 0 and n_ctx % GDN_EMIT_ROWS == 0 and (n // C) % 2 == 0
    nch = n // C
    skip = 0 if need_ctx else n_ctx // C
    n_out = n - skip * C
    g5 = gates.reshape(bsz, n, 4, HQ, 2)
    gcol = g5.transpose(0, 3, 1, 2, 4).reshape(bsz, HQ, n, 2 * GDN_INST)
    grow = gates.reshape(bsz, nch, C, 4, HQ, 2).transpose(0, 4, 1, 3, 5, 2).reshape(bsz, HQ, nch, 2 * GDN_INST, C)
    pair = 2 * GDN_DV
    zb = z_col_off // pair
    kern = functools.partial(_gdn_kernel, n_chunks=nch, ctx_chunks=n_ctx // C, skip_chunks=skip)
    return pl.pallas_call(
        kern,
        grid=(bsz, HQ),
        in_specs=[
            pl.BlockSpec((None, n, GDN_DK), lambda b, h: (b, 0, h)),
            pl.BlockSpec((None, n, GDN_DK), lambda b, h: (b, 0, HQ + h)),
            pl.BlockSpec((None, n, pair), lambda b, h: (b, 0, HQ + h)),
            pl.BlockSpec((None, n, pair), lambda b, h: (b, 0, zb + h)),
            pl.BlockSpec((None, None, n, 2 * GDN_INST), lambda b, h: (b, h, 0, 0)),
            pl.BlockSpec((None, None, nch, 2 * GDN_INST, C), lambda b, h: (b, h, 0, 0, 0)),
            pl.BlockSpec((1, GDN_DV), lambda b, h: (0, 0)),
        ],
        out_specs=pl.BlockSpec((None, n_out, pair), lambda b, h: (b, 0, h)),
        out_shape=jax.ShapeDtypeStruct((bsz, n_out, HV * GDN_DV), BF16),
        scratch_shapes=[pltpu.VMEM((6 + len(GDN_INV_LEVELS), C, C), F32),
                        pltpu.VMEM((nch, GDN_DK, C), BF16),
                        pltpu.VMEM((GDN_INST, nch, GDN_DK + C, GDN_DV), BF16),
                        pltpu.VMEM((GDN_INST, nch, GDN_DK, GDN_DV), BF16),
                        pltpu.VMEM((GDN_INST, nch, C, GDN_DV), BF16),
                        pltpu.VMEM((2, n, GDN_DV), F32), pltpu.VMEM((2, n, GDN_DV), F32),
                        pltpu.VMEM((GDN_INST, GDN_DK, GDN_DV), F32)],
        compiler_params=_cparams("arbitrary", "arbitrary"),
        name="gdn_scan",
    )(qkv, qkv, qkv, proj, gcol, grow, norm_w.astype(F32).reshape(1, GDN_DV))


def kernel(x, c, ctx, c_ctx, ada_w, ada_b, ln_g, ln_b, ret_w_in, ret_decay, ret_w_out,
           gdn_w_in, gdn_conv, gdn_a_log, gdn_dt_bias, gdn_norm, gdn_w_out, ffn_w_in, ffn_w_out):
    bsz, n_lat, d = x.shape
    n_ctx = ctx.shape[1]
    depth = ada_w.shape[0]
    n_tot = n_ctx + n_lat
    alpha = (2 * depth) ** 0.25

    pad_rows = (-(bsz + 1)) % 16
    cc = jnp.concatenate([c, c_ctx[None], jnp.zeros((pad_rows, d), F32)], axis=0)
    mods = _ada_mod(cc, ada_w, ada_b)

    def layer_params(i):
        lat = mods[i, :bsz].reshape(bsz, 1, 6, d)
        cx = jnp.broadcast_to(mods[i, bsz].reshape(1, 1, 6, d), (bsz, 1, 6, d))
        return jnp.concatenate([cx, lat], axis=1)

    params = [layer_params(i) for i in range(depth)]
    h = jnp.concatenate([ctx, x], axis=1)
    u = _modulate(h, params[0], shift=0, scale=1, n_ctx=n_ctx)
    ctx_rows = n_ctx

    for i in range(depth):
        need_ctx = i < depth - 1
        j = i // 2
        u2 = u.reshape(bsz * n_tot, d)
        if i % 2 == 0:
            qkvg = _matmul(u2, ret_w_in, j, n_out=ret_w_in.shape[-1], tn=1024, out_dtype=BF16,
                           name="ret_in_proj").reshape(bsz, n_tot, -1)
            dk = qkvg.shape[-1] // (6 * RET_HEADS)
            cos, sin = _rope_tables(n_ctx, n_lat, dk)
            o = _retention(qkvg, ret_decay[j], cos, sin, n_ctx=n_ctx, need_ctx=need_ctx)
            w_out = ret_w_out
        else:
            qkv_w = gdn_conv.shape[-1]
            hv = gdn_a_log.shape[-1]
            main_w = qkv_w + hv * GDN_DV
            proj = _matmul(u2, gdn_w_in, j, n_out=main_w, tn=1024, out_dtype=BF16,
                           name="gdn_in_proj").reshape(bsz, n_tot, main_w)
            ba = _matmul(u2, gdn_w_in, j, n_out=4 * hv, col_off=main_w, tn=4 * hv, out_dtype=F32,
                         name="gdn_gate_proj").reshape(bsz, n_tot, 4 * hv)
            gates = _gdn_gates(ba, gdn_a_log[j], gdn_dt_bias[j])
            qkv = _gdn_conv(proj, gdn_conv[j], n_ctx=n_ctx, n_qk_heads=hv // 2)
            o = _gdn_scan(qkv, proj, gates, gdn_norm[j], n_ctx=n_ctx, need_ctx=need_ctx,
                          n_qk_heads=hv // 2, z_col_off=qkv_w)
            w_out = gdn_w_out

        n_rows = o.shape[1]
        out_ctx = n_ctx if need_ctx else 0
        y = _matmul(o.reshape(bsz * n_rows, -1), w_out, j, n_out=d, tn=512, out_dtype=F32,
                    name="mixer_out_proj").reshape(bsz, n_rows, d)
        h, u = _resid_ln(h, y, params[i], ln_g[i, 0], ln_b[i, 0], params[i], alpha=alpha, gate=2,
                         shift=3, scale=4, n_ctx=out_ctx, h_row_offset=ctx_rows - out_ctx)
        ctx_rows = out_ctx
        y = _ffn(u, ffn_w_in, ffn_w_out, i)
        if need_ctx:
            h, u = _resid_ln(h, y, params[i], ln_g[i, 1], ln_b[i, 1], params[i + 1], alpha=alpha, gate=5,
                             shift=0, scale=1, n_ctx=out_ctx)
        else:
            h = _resid_ln(h, y, params[i], ln_g[i, 1], ln_b[i, 1], params[i], alpha=alpha, gate=5,
                          shift=0, scale=1, n_ctx=out_ctx, with_u=False)
    return h
```

```python
import functools
import math

import jax
import jax.numpy as jnp
from jax import lax
from jax.experimental import pallas as pl
from jax.experimental.pallas import tpu as pltpu

F32 = jnp.float32
BF16 = jnp.bfloat16

GRID_W = 64
RET_HEADS = 8
RET_CHUNK = 128
RET_UNROLL = 3
ROPE_BASE = 10000.0
GDN_DK = 128
GDN_DV = 128
GDN_CONV = 5
GDN_CHUNK = 64
LN_EPS = 1e-5

VMEM_LIMIT_BYTES = 56 * 1024 * 1024
ROW_TILE = 256
MM_ROW_TILE = 1024


def _cparams(*sem):
    return pltpu.CompilerParams(dimension_semantics=sem, vmem_limit_bytes=VMEM_LIMIT_BYTES)


def _dot(a, b):
    return jnp.dot(a, b, preferred_element_type=F32)


def _dot_nt(a, b):
    return lax.dot_general(a, b, (((1,), (1,)), ((), ())), preferred_element_type=F32)


def _ada_kernel(c_ref, w_ref, b_ref, o_ref):
    a = jax.nn.silu(c_ref[...]).astype(BF16)
    o_ref[...] = _dot(a, w_ref[...].astype(BF16)) + b_ref[...]


def _ada_mod(cc, ada_w, ada_b, tn=1024):
    depth, d, n = ada_w.shape
    rows = cc.shape[0]
    return pl.pallas_call(
        _ada_kernel,
        grid=(depth, n // tn),
        in_specs=[
            pl.BlockSpec((rows, d), lambda l, j: (0, 0)),
            pl.BlockSpec((None, d, tn), lambda l, j: (l, 0, j)),
            pl.BlockSpec((None, 1, tn), lambda l, j: (l, 0, j)),
        ],
        out_specs=pl.BlockSpec((None, rows, tn), lambda l, j: (l, 0, j)),
        out_shape=jax.ShapeDtypeStruct((depth, rows, n), F32),
        compiler_params=_cparams("arbitrary", "arbitrary"),
        name="ada_mod",
    )(cc, ada_w, ada_b.reshape(depth, 1, n))


def _load_h(h_refs, ctx_tiles):
    if len(h_refs) == 1:
        return h_refs[0][...]
    return jnp.where(pl.program_id(1) < ctx_tiles, h_refs[0][...], h_refs[1][...])


def _h_operands(h, d, ctx_tiles, off=0):
    if not isinstance(h, tuple):
        return [h], [pl.BlockSpec((None, ROW_TILE, d), lambda b, r: (b, r + off, 0))]
    assert off == 0
    return list(h), [
        pl.BlockSpec((None, ROW_TILE, d), lambda b, r: (b, jnp.minimum(r, ctx_tiles - 1), 0)),
        pl.BlockSpec((None, ROW_TILE, d), lambda b, r: (b, jnp.maximum(r - ctx_tiles, 0), 0))]


def _mod_kernel(*refs, shift, scale, ctx_tiles):
    *h_refs, p_ref, u_ref = refs
    p = p_ref[...]
    h = _load_h(h_refs, ctx_tiles)
    u_ref[...] = (h * (1.0 + p[scale:scale + 1]) + p[shift:shift + 1]).astype(u_ref.dtype)


def _param_spec(d, seg_of_tile):
    return pl.BlockSpec((None, None, 6, d), lambda b, r: (b, seg_of_tile(r), 0, 0))


def _modulate(h, p, *, shift, scale, n_ctx, n):
    bsz, _, _, d = p.shape
    ctx_tiles = n_ctx // ROW_TILE
    seg = lambda r: jnp.where(r < ctx_tiles, 0, 1)
    h_args, h_specs = _h_operands(h, d, ctx_tiles)
    return pl.pallas_call(
        functools.partial(_mod_kernel, shift=shift, scale=scale, ctx_tiles=ctx_tiles),
        grid=(bsz, n // ROW_TILE),
        in_specs=h_specs + [_param_spec(d, seg)],
        out_specs=pl.BlockSpec((None, ROW_TILE, d), lambda b, r: (b, r, 0)),
        out_shape=jax.ShapeDtypeStruct((bsz, n, d), BF16),
        compiler_params=_cparams("arbitrary", "arbitrary"),
        name="modulate",
    )(*h_args, p)


def _resid_ln_kernel(*refs, alpha, gate, shift, scale, ctx_tiles, n_h):
    h_refs, (y_ref, p_ref, g_ref, b_ref, p2_ref, h_out, *u_out) = refs[:n_h], refs[n_h:]
    p = p_ref[...]
    x = alpha * _load_h(h_refs, ctx_tiles) + p[gate:gate + 1] * y_ref[...].astype(F32)
    mu = jnp.mean(x, axis=-1, keepdims=True)
    xc = x - mu
    var = jnp.mean(xc * xc, axis=-1, keepdims=True)
    hn = xc * lax.rsqrt(var + LN_EPS) * g_ref[...] + b_ref[...]
    h_out[...] = hn
    if u_out:
        p2 = p2_ref[...]
        u_out[0][...] = (hn * (1.0 + p2[scale:scale + 1]) + p2[shift:shift + 1]).astype(BF16)


def _resid_ln(h, y, p, ln_g, ln_b, p2, *, alpha, gate, shift, scale, n_ctx, h_row_offset=0, with_u=True):
    bsz, n, d = y.shape
    ctx_tiles = n_ctx // ROW_TILE
    seg = lambda r: jnp.where(r < ctx_tiles, 0, 1)
    h_args, h_specs = _h_operands(h, d, ctx_tiles, h_row_offset // ROW_TILE)
    row = pl.BlockSpec((None, ROW_TILE, d), lambda b, r: (b, r, 0))
    vec = pl.BlockSpec((1, d), lambda b, r: (0, 0))
    out_shape = [jax.ShapeDtypeStruct((bsz, n, d), F32)]
    out_specs = [row]
    if with_u:
        out_shape.append(jax.ShapeDtypeStruct((bsz, n, d), BF16))
        out_specs.append(row)
    outs = pl.pallas_call(
        functools.partial(_resid_ln_kernel, alpha=alpha, gate=gate, shift=shift, scale=scale,
                          ctx_tiles=ctx_tiles, n_h=len(h_args)),
        grid=(bsz, n // ROW_TILE),
        in_specs=h_specs + [row, _param_spec(d, seg), vec, vec, _param_spec(d, seg)],
        out_specs=out_specs,
        out_shape=out_shape,
        compiler_params=_cparams("arbitrary", "arbitrary"),
        name="resid_ln",
    )(*h_args, y, p, ln_g.reshape(1, d), ln_b.reshape(1, d), p2)
    return outs if with_u else outs[0]


def _mm_kernel(a_ref, w_ref, o_ref, wb_ref):
    @pl.when(pl.program_id(1) == 0)
    def _():
        wb_ref[...] = w_ref[...].astype(BF16)

    o_ref[...] = _dot(a_ref[...], wb_ref[...]).astype(o_ref.dtype)


def _matmul(a, w, layer, *, n_out, col_off=0, tn, tm=MM_ROW_TILE, out_dtype, name):
    m, k = a.shape
    assert m % tm == 0 and n_out % tn == 0 and col_off % tn == 0
    off = col_off // tn
    return pl.pallas_call(
        _mm_kernel,
        grid=(n_out // tn, m // tm),
        in_specs=[pl.BlockSpec((tm, k), lambda j, i: (i, 0)),
                  pl.BlockSpec((None, k, tn), lambda j, i: (layer, 0, j + off))],
        out_specs=pl.BlockSpec((tm, tn), lambda j, i: (i, j)),
        out_shape=jax.ShapeDtypeStruct((m, n_out), out_dtype),
        scratch_shapes=[pltpu.VMEM((k, tn), BF16)],
        compiler_params=_cparams("arbitrary", "arbitrary"),
        name=name,
    )(a, w)


def _swiglu_kernel(a_ref, wg_ref, wu_ref, o_ref, wgb_ref, wub_ref):
    @pl.when(pl.program_id(1) == 0)
    def _():
        wgb_ref[...] = wg_ref[...].astype(BF16)
        wub_ref[...] = wu_ref[...].astype(BF16)

    a = a_ref[...]
    gate = _dot(a, wgb_ref[...])
    up = _dot(a, wub_ref[...])
    o_ref[...] = (jax.nn.silu(gate) * up).astype(o_ref.dtype)


def _swiglu_in(a, w, layer, *, tn=512, tm=MM_ROW_TILE):
    m, k = a.shape
    f = w.shape[2] // 2
    assert m % tm == 0 and f % tn == 0
    nb = f // tn
    return pl.pallas_call(
        _swiglu_kernel,
        grid=(nb, m // tm),
        in_specs=[pl.BlockSpec((tm, k), lambda j, i: (i, 0)),
                  pl.BlockSpec((None, k, tn), lambda j, i: (layer, 0, j)),
                  pl.BlockSpec((None, k, tn), lambda j, i: (layer, 0, j + nb))],
        out_specs=pl.BlockSpec((tm, tn), lambda j, i: (i, j)),
        out_shape=jax.ShapeDtypeStruct((m, f), BF16),
        scratch_shapes=[pltpu.VMEM((k, tn), BF16), pltpu.VMEM((k, tn), BF16)],
        compiler_params=_cparams("arbitrary", "arbitrary"),
        name="ffn_in_swiglu",
    )(a, w, w)


def _ffn(u, w_in, w_out, layer):
    bsz, n, d = u.shape
    a = _swiglu_in(u.reshape(bsz * n, d), w_in, layer)
    y = _matmul(a, w_out, layer, n_out=d, tn=512, tm=512, out_dtype=F32, name="ffn_out")
    return y.reshape(bsz, n, d)


def _ret_kernel(draw_ref, q_ref, k_ref, v_ref, g_ref, cos_ref, sin_ref, o_ref,
                qs_ref, ks_ref, oacc_ref, sf_ref, sb_ref, *, n_chunks, ctx_chunks, skip_chunks, dk):
    C = RET_CHUNK
    h = pl.program_id(1)
    half = dk // 2
    lg_f = -jnp.exp(jnp.full((1, 1), draw_ref[0, h], F32))
    lg_b = -jnp.exp(jnp.full((1, 1), draw_ref[1, h], F32))

    ri = lax.broadcasted_iota(jnp.int32, (C, C), 0)
    ci = lax.broadcasted_iota(jnp.int32, (C, C), 1)
    diff = (ri - ci).astype(F32)
    lower = diff >= 0
    dmat = jnp.where(lower, jnp.exp(lg_f * jnp.where(lower, diff, 0.0)),
                     jnp.exp(lg_b * jnp.where(lower, 0.0, -diff)))
    rc = lax.broadcasted_iota(jnp.int32, (C, 1), 0).astype(F32)
    qdec_f = jnp.exp(lg_f * (rc + 1.0))
    kdec_f = jnp.exp(lg_f * (C - 1.0 - rc))
    cdec_f = jnp.exp(lg_f * C)
    qdec_b = jnp.exp(lg_b * (C - rc))
    kdec_b = jnp.exp(lg_b * rc)
    cdec_b = jnp.exp(lg_b * C)

    sf_ref[...] = jnp.zeros_like(sf_ref)
    sb_ref[...] = jnp.zeros_like(sb_ref)
    qscale = dk ** -0.5

    def rope(x, cs, sn):
        x1, x2 = x[:, :half], x[:, half:]
        return jnp.concatenate([x1 * cs - x2 * sn, x1 * sn + x2 * cs], axis=-1)

    def fwd(c, carry):
        rows = pl.ds(pl.multiple_of(c * C, C), C)
        cs, sn = cos_ref[rows, :], sin_ref[rows, :]
        q = rope(q_ref[rows, :].astype(F32), cs, sn) * qscale
        k = rope(k_ref[rows, :].astype(F32), cs, sn)
        qb, kb = q.astype(BF16), k.astype(BF16)
        qs_ref[rows, :] = qb
        ks_ref[rows, :] = kb
        v = v_ref[rows, :]
        s = (_dot_nt(qb, kb) * dmat).astype(BF16)
        state = sf_ref[...]
        oacc_ref[rows, :] = _dot(s, v) + _dot(qb, state.astype(BF16)) * qdec_f
        kd_t = jnp.transpose(k * kdec_f).astype(BF16)
        sf_ref[...] = cdec_f * state + _dot(kd_t, v)
        return carry

    lax.fori_loop(0, n_chunks, fwd, 0, unroll=RET_UNROLL)

    def bwd(t, carry):
        c = jnp.where(t < ctx_chunks, ctx_chunks - 1 - t, n_chunks - 1 + ctx_chunks - t)
        rows = pl.ds(pl.multiple_of(c * C, C), C)
        qb, kb, v = qs_ref[rows, :], ks_ref[rows, :], v_ref[rows, :]
        state = sb_ref[...]
        o = oacc_ref[rows, :] + _dot(qb, state.astype(BF16)) * qdec_b
        kd_t = jnp.transpose(kb.astype(F32) * kdec_b).astype(BF16)
        sb_ref[...] = cdec_b * state + _dot(kd_t, v)

        def emit():
            mu = jnp.mean(o, axis=-1, keepdims=True)
            oc = o - mu
            var = jnp.mean(oc * oc, axis=-1, keepdims=True)
            on = oc * lax.rsqrt(var + LN_EPS)
            out_rows = pl.ds(pl.multiple_of((c - skip_chunks) * C, C), C)
            o_ref[out_rows, :] = (on * jax.nn.silu(g_ref[rows, :].astype(F32))).astype(o_ref.dtype)

        if skip_chunks:
            pl.when(c >= skip_chunks)(emit)
        else:
            emit()
        return carry

    lax.fori_loop(0, n_chunks, bwd, 0, unroll=RET_UNROLL)


def _retention(qkvg, decay_raw, cos, sin, *, n_ctx, need_ctx):
    bsz, n, width = qkvg.shape
    H = RET_HEADS
    dk = width // (6 * H)
    dv = 2 * dk
    C = RET_CHUNK
    assert n % C == 0 and n_ctx % C == 0
    skip = 0 if need_ctx else n_ctx // C
    n_out = n - skip * C
    kern = functools.partial(_ret_kernel, n_chunks=n // C, ctx_chunks=n_ctx // C, skip_chunks=skip, dk=dk)
    return pl.pallas_call(
        kern,
        grid=(bsz, H),
        in_specs=[
            pl.BlockSpec(memory_space=pltpu.SMEM),
            pl.BlockSpec((None, n, dk), lambda b, h: (b, 0, h)),
            pl.BlockSpec((None, n, dk), lambda b, h: (b, 0, H + h)),
            pl.BlockSpec((None, n, dv), lambda b, h: (b, 0, H + h)),
            pl.BlockSpec((None, n, dv), lambda b, h: (b, 0, 2 * H + h)),
            pl.BlockSpec((n, dk // 2), lambda b, h: (0, 0)),
            pl.BlockSpec((n, dk // 2), lambda b, h: (0, 0)),
        ],
        out_specs=pl.BlockSpec((None, n_out, dv), lambda b, h: (b, 0, h)),
        out_shape=jax.ShapeDtypeStruct((bsz, n_out, H * dv), BF16),
        scratch_shapes=[pltpu.VMEM((n, dk), BF16), pltpu.VMEM((n, dk), BF16), pltpu.VMEM((n, dv), F32),
                        pltpu.VMEM((dk, dv), F32), pltpu.VMEM((dk, dv), F32)],
        compiler_params=_cparams("arbitrary", "arbitrary"),
        name="retention",
    )(decay_raw.astype(F32), qkvg, qkvg, qkvg, qkvg, cos, sin)


def _rope_tables(n_ctx, n_lat, dim):
    rows = n_lat // GRID_W
    row = jnp.repeat(jnp.arange(rows, dtype=F32), GRID_W)
    col = jnp.tile(jnp.arange(GRID_W, dtype=F32), rows)
    n_freq = dim // 4
    inv_freq = ROPE_BASE ** (-jnp.arange(n_freq, dtype=F32) / n_freq)
    ang = jnp.concatenate([row[:, None] * inv_freq, col[:, None] * inv_freq], axis=-1)
    cos = jnp.concatenate([jnp.ones((n_ctx, dim // 2), F32), jnp.cos(ang)], axis=0)
    sin = jnp.concatenate([jnp.zeros((n_ctx, dim // 2), F32), jnp.sin(ang)], axis=0)
    return cos, sin


def _conv_kernel(x_ref, w_ref, o_ref, *, n_ctx, n_q_blocks, n_qk_blocks, qscale):
    j = pl.program_id(1)
    x = x_ref[...].astype(F32)
    n, tc = x.shape
    w = w_ref[...]
    pad = GDN_CONV // 2
    t = lax.broadcasted_iota(jnp.int32, (n, 1), 0)
    lo = jnp.where(t < n_ctx, 0, n_ctx)
    hi = jnp.where(t < n_ctx, n_ctx, n)
    acc = x * w[pad:pad + 1]
    for d in range(-pad, pad + 1):
        if d == 0:
            continue
        xs = pltpu.roll(x, (n - d) % n, 0)
        valid = (t + d >= lo) & (t + d < hi)
        acc = acc + jnp.where(valid, xs, 0.0) * w[d + pad:d + pad + 1]
    y = jax.nn.silu(acc)
    do_norm = j < n_qk_blocks
    scale = jnp.where(j < n_q_blocks, qscale, 1.0).astype(F32)
    for g0 in range(0, tc, GDN_DK):
        yg = y[:, g0:g0 + GDN_DK]
        ss = jnp.sum(yg * yg, axis=-1, keepdims=True)
        yn = yg * (lax.rsqrt(ss + 1e-6) * scale)
        o_ref[:, g0:g0 + GDN_DK] = jnp.where(do_norm, yn, yg).astype(o_ref.dtype)


def _gdn_conv(proj, conv_w, *, n_ctx, n_qk_heads, tc=512):
    bsz, n, _ = proj.shape
    width = conv_w.shape[1]
    qw = n_qk_heads * GDN_DK
    assert qw % tc == 0 and width % tc == 0
    kern = functools.partial(_conv_kernel, n_ctx=n_ctx, n_q_blocks=qw // tc, n_qk_blocks=2 * qw // tc,
                             qscale=GDN_DK ** -0.5)
    return pl.pallas_call(
        kern,
        grid=(bsz, width // tc),
        in_specs=[pl.BlockSpec((None, n, tc), lambda b, j: (b, 0, j)),
                  pl.BlockSpec((GDN_CONV, tc), lambda b, j: (0, j))],
        out_specs=pl.BlockSpec((None, n, tc), lambda b, j: (b, 0, j)),
        out_shape=jax.ShapeDtypeStruct((bsz, n, width), BF16),
        compiler_params=_cparams("arbitrary", "arbitrary"),
        name="gdn_conv",
    )(proj, conv_w)


def _gates_kernel(x_ref, alog_ref, dt_ref, o_ref, *, n_heads):
    C = GDN_CHUNK
    x = x_ref[...]
    n = x.shape[0]
    beta = jax.nn.sigmoid(x)
    z = x + dt_ref[...]
    softplus = jnp.maximum(z, 0.0) + jnp.log1p(jnp.exp(-jnp.abs(z)))
    la = -jnp.exp(alog_ref[...]) * softplus
    tm = lax.broadcasted_iota(jnp.int32, (n, 1), 0) % C
    pre = la
    suf = la
    s = 1
    while s < C:
        pre = pre + jnp.where(tm >= s, pltpu.roll(pre, s, 0), 0.0)
        suf = suf + jnp.where(tm < C - s, pltpu.roll(suf, n - s, 0), 0.0)
        s *= 2
    lane = lax.broadcasted_iota(jnp.int32, (1, x.shape[1]), 1)
    o_ref[...] = jnp.where(lane < 2 * n_heads, beta, jnp.where(lane < 3 * n_heads, pre, suf))


def _gdn_gates(ba, a_log, dt_bias):
    bsz, n, w = ba.shape
    n_heads = w // 4
    zeros = jnp.zeros((2 * n_heads,), F32)
    alog = jnp.concatenate([zeros, a_log.astype(F32).reshape(-1)]).reshape(1, w)
    dt = jnp.concatenate([zeros, dt_bias.astype(F32).reshape(-1)]).reshape(1, w)
    return pl.pallas_call(
        functools.partial(_gates_kernel, n_heads=n_heads),
        grid=(bsz,),
        in_specs=[pl.BlockSpec((None, n, w), lambda b: (b, 0, 0)),
                  pl.BlockSpec((1, w), lambda b: (0, 0)), pl.BlockSpec((1, w), lambda b: (0, 0))],
        out_specs=pl.BlockSpec((None, n, w), lambda b: (b, 0, 0)),
        out_shape=jax.ShapeDtypeStruct((bsz, n, w), F32),
        compiler_params=_cparams("arbitrary"),
        name="gdn_gates",
    )(ba, alog, dt)


GDN_INST = 4
GDN_STACK = GDN_INST * GDN_CHUNK
GDN_PREP_CHUNKS = 6
GDN_EMIT_ROWS = 256


GDN_INV_BASE = 8
GDN_INV_LEVELS = (16, 32, 64)


def _merge_rows(g, lower):
    h = g // 2
    return [(r0 + h, r0 + g) if lower else (r0, r0 + h) for r0 in range(0, GDN_CHUNK, g)]


def _unit_tri_inverse(a_list, lower_list, same_base, off_masks, eye, between):
    def mm(x, y):
        return _dot(x.astype(BF16), y.astype(BF16))

    ad = [a * same_base for a in a_list]
    x = [eye - d for d in ad]
    p = [mm(d, d) for d in ad]
    between()
    xp = [mm(xi, pi) for xi, pi in zip(x, p)]
    p = [mm(pi, pi) for pi in p]
    between()
    x = [xi + xpi for xi, xpi in zip(x, xp)]
    x = [xi + mm(xi, pi) for xi, pi in zip(x, p)]
    between()
    for g, off in zip(GDN_INV_LEVELS, off_masks):
        h = g // 2
        sels = [_merge_rows(g, lower) for lower in lower_list]
        top = [jnp.concatenate([xi[a:b] for a, b in sel], axis=0) for xi, sel in zip(x, sels)]
        n = [mm(ti, a * off) for ti, a in zip(top, a_list)]
        between()
        upd = [ti - mm(ni, xi) for ti, ni, xi in zip(top, n, x)]
        between()
        merged = []
        for xi, ui, sel in zip(x, upd, sels):
            pieces = []
            for idx, (a, b) in enumerate(sel):
                new = ui[idx * h:(idx + 1) * h]
                r0 = (a // g) * g
                pieces += [xi[r0:r0 + h], new] if a != r0 else [new, xi[r0 + h:r0 + g]]
            merged.append(jnp.concatenate(pieces, axis=0))
        x = merged
    return x


def _gdn_kernel(q_ref, k_ref, v_ref, z_ref, gcol_ref, grow_ref, nw_ref, o_ref,
                mask_ref, kt_ref, gq_ref, r_ref, o0_ref, of_ref, ob_ref, m_ref,
                *, n_chunks, ctx_chunks, skip_chunks):
    C = GDN_CHUNK
    S = GDN_INST
    dk = GDN_DK
    dv = GDN_DV
    n_lvl = len(GDN_INV_LEVELS)
    M_INCL, M_STRICT, M_BASE, M_OFF, M_EYE = 0, 2, 4, 5, 5 + n_lvl

    r = lax.broadcasted_iota(jnp.int32, (C, C), 0)
    c = lax.broadcasted_iota(jnp.int32, (C, C), 1)
    mask_ref[M_INCL] = (r >= c).astype(F32)
    mask_ref[M_INCL + 1] = (c >= r).astype(F32)
    mask_ref[M_STRICT] = (r > c).astype(F32)
    mask_ref[M_STRICT + 1] = (c > r).astype(F32)
    mask_ref[M_BASE] = ((r // GDN_INV_BASE) == (c // GDN_INV_BASE)).astype(F32)
    for lvl, g in enumerate(GDN_INV_LEVELS):
        h = g // 2
        mask_ref[M_OFF + lvl] = (((r // g) == (c // g)) & ((r // h) != (c // h))).astype(F32)
    mask_ref[M_EYE] = (r == c).astype(F32)

    def xpose(p, carry):
        rows = pl.ds(pl.multiple_of(p * 2 * C, 2 * C), 2 * C)
        kt = jnp.transpose(k_ref[rows, :].astype(F32)).astype(BF16)
        kt_ref[2 * p] = kt[:, :C]
        kt_ref[2 * p + 1] = kt[:, C:]
        return carry

    lax.fori_loop(0, n_chunks // 2, xpose, 0, unroll=3)

    NB = GDN_PREP_CHUNKS
    n_blocks = n_chunks // NB

    def scan_chunks(t):
        cb = jnp.where(t < ctx_chunks, ctx_chunks - 1 - t, n_chunks - 1 + ctx_chunks - t)
        return t, cb

    def prep_block(j, buf, between):
        items = []
        for i in range(NB):
            for d, ci in enumerate(scan_chunks(j * NB + i)):
                rows = pl.ds(pl.multiple_of(ci * C, C), C)
                qb, kb = q_ref[rows, :], k_ref[rows, :]
                pk = _dot_nt(jnp.concatenate([kb, qb], axis=0), kb)
                kk, qk = pk[:C], pk[C:]
                g8 = gcol_ref[rows, :]
                grow8 = grow_ref[ci]
                qf, kf, vf = qb.astype(F32), kb.astype(F32), v_ref[rows, :].astype(F32)
                kt = kt_ref[ci].astype(F32)
                for vj in range(2):
                    s = 2 * d + vj
                    beta, gc = g8[:, s:s + 1], g8[:, S + s:S + s + 1]
                    g_row = grow8[S + s:S + s + 1, :]
                    e = jnp.exp(jnp.minimum(gc - g_row, 0.0))
                    a = beta * kk * (e * mask_ref[M_STRICT + d])
                    qkm = (qk * (e * mask_ref[M_INCL + d])).astype(BF16)
                    eg = jnp.exp(gc)
                    rhs = jnp.concatenate([vf[:, vj * dv:(vj + 1) * dv] * beta, kf * (beta * eg)], axis=-1)
                    g_last = g_row[:, 0:1] if d else g_row[:, C - 1:C]
                    kdt = (kt * jnp.exp(g_last - g_row)).astype(BF16)
                    items.append(dict(slot=i, s=s, a=a, qkm=qkm, rhs=rhs.astype(BF16), qd=qf * eg, kdt=kdt))
        between()

        xs = _unit_tri_inverse([it["a"] for it in items], [it["s"] < 2 for it in items], mask_ref[M_BASE],
                               [mask_ref[M_OFF + i] for i in range(n_lvl)], mask_ref[M_EYE], between)
        sols = [_dot(x.astype(BF16), it["rhs"]).astype(BF16) for x, it in zip(xs, items)]
        between()
        ows = [_dot(it["qkm"], sol) for it, sol in zip(items, sols)]
        rgs = [_dot(it["kdt"], sol) for it, sol in zip(items, sols)]
        between()
        for it, ow, rg in zip(items, ows, rgs):
            s, slot = it["s"], it["slot"]
            gq_ref[buf, s, slot, 0:dk, :] = rg[:, dv:].astype(BF16)
            gq_ref[buf, s, slot, dk:dk + C, :] = (it["qd"] - ow[:, dv:]).astype(BF16)
            r_ref[buf, s, slot] = rg[:, :dv].astype(BF16)
            o0_ref[buf, s, slot] = ow[:, :dv].astype(BF16)

    def scan_step(j, buf, i):
        cf, cb = scan_chunks(j * NB + i)
        cis = [cf, cf, cb, cb]
        ms = [m_ref[s] for s in range(S)]
        pq = [_dot(gq_ref[buf, s, i], ms[s].astype(BF16)) for s in range(S)]
        for s in range(S):
            ci = cis[s]
            g_s = grow_ref[ci][S + s:S + s + 1, :]
            g_last = g_s[:, C - 1:C] if s < 2 else g_s[:, 0:1]
            m_ref[s] = jnp.exp(g_last) * ms[s] - pq[s][:dk] + r_ref[buf, s, i].astype(F32)
            o = pq[s][dk:] + o0_ref[buf, s, i].astype(F32)
            rows = pl.ds(pl.multiple_of(ci * C, C), C)
            if s < 2:
                of_ref[s, rows, :] = o
            else:
                ob_ref[s - 2, rows, :] = o

    m_ref[...] = jnp.zeros_like(m_ref)
    prep_block(0, 0, lambda: None)

    def pipelined(j, carry):
        buf = j % 2
        pending = [functools.partial(scan_step, j, buf, i) for i in range(NB)]

        def between():
            if pending:
                pending.pop(0)()

        prep_block(j + 1, 1 - buf, between)
        while pending:
            pending.pop(0)()
        return carry

    lax.fori_loop(0, n_blocks - 1, pipelined, 0)
    for i in range(NB):
        scan_step(n_blocks - 1, (n_blocks - 1) % 2, i)

    nw = nw_ref[...]

    E = GDN_EMIT_ROWS

    def emit(i, carry):
        rows = pl.ds(pl.multiple_of(i * E, E), E)
        out_rows = pl.ds(pl.multiple_of(i * E - skip_chunks * C, E), E)
        for vj in range(2):
            o = of_ref[vj, rows, :] + ob_ref[vj, rows, :]
            zf = z_ref[rows, vj * dv:(vj + 1) * dv].astype(F32)
            on = o * lax.rsqrt(jnp.mean(o * o, axis=-1, keepdims=True) + 1e-6) * nw * jax.nn.silu(zf)
            o_ref[out_rows, vj * dv:(vj + 1) * dv] = on.astype(o_ref.dtype)
        return carry

    lax.fori_loop(skip_chunks * C // E, n_chunks * C // E, emit, 0)


def _gdn_scan(qkv, proj, gates, norm_w, *, n_ctx, need_ctx, n_qk_heads, z_col_off):
    bsz, n, _ = qkv.shape
    C = GDN_CHUNK
    HQ = n_qk_heads
    HV = gates.shape[-1] // 4
    assert HV == 2 * HQ and n % (GDN_PREP_CHUNKS * C) == 0 and n_ctx % C == 0
    assert n % GDN_EMIT_ROWS == 0 and n_ctx % GDN_EMIT_ROWS == 0 and (n // C) % 2 == 0
    nch = n // C
    skip = 0 if need_ctx else n_ctx // C
    n_out = n - skip * C
    g5 = gates.reshape(bsz, n, 4, HQ, 2)
    gcol = g5.transpose(0, 3, 1, 2, 4).reshape(bsz, HQ, n, 2 * GDN_INST)
    grow = gates.reshape(bsz, nch, C, 4, HQ, 2).transpose(0, 4, 1, 3, 5, 2).reshape(bsz, HQ, nch, 2 * GDN_INST, C)
    pair = 2 * GDN_DV
    zb = z_col_off // pair
    kern = functools.partial(_gdn_kernel, n_chunks=nch, ctx_chunks=n_ctx // C, skip_chunks=skip)
    return pl.pallas_call(
        kern,
        grid=(bsz, HQ),
        in_specs=[
            pl.BlockSpec((None, n, GDN_DK), lambda b, h: (b, 0, h)),
            pl.BlockSpec((None, n, GDN_DK), lambda b, h: (b, 0, HQ + h)),
            pl.BlockSpec((None, n, pair), lambda b, h: (b, 0, HQ + h)),
            pl.BlockSpec((None, n, pair), lambda b, h: (b, 0, zb + h)),
            pl.BlockSpec((None, None, n, 2 * GDN_INST), lambda b, h: (b, h, 0, 0)),
            pl.BlockSpec((None, None, nch, 2 * GDN_INST, C), lambda b, h: (b, h, 0, 0, 0)),
            pl.BlockSpec((1, GDN_DV), lambda b, h: (0, 0)),
        ],
        out_specs=pl.BlockSpec((None, n_out, pair), lambda b, h: (b, 0, h)),
        out_shape=jax.ShapeDtypeStruct((bsz, n_out, HV * GDN_DV), BF16),
        scratch_shapes=[pltpu.VMEM((6 + len(GDN_INV_LEVELS), C, C), F32),
                        pltpu.VMEM((nch, GDN_DK, C), BF16),
                        pltpu.VMEM((2, GDN_INST, GDN_PREP_CHUNKS, GDN_DK + C, GDN_DV), BF16),
                        pltpu.VMEM((2, GDN_INST, GDN_PREP_CHUNKS, GDN_DK, GDN_DV), BF16),
                        pltpu.VMEM((2, GDN_INST, GDN_PREP_CHUNKS, C, GDN_DV), BF16),
                        pltpu.VMEM((2, n, GDN_DV), F32), pltpu.VMEM((2, n, GDN_DV), F32),
                        pltpu.VMEM((GDN_INST, GDN_DK, GDN_DV), F32)],
        compiler_params=_cparams("arbitrary", "arbitrary"),
        name="gdn_scan",
    )(qkv, qkv, qkv, proj, gcol, grow, norm_w.astype(F32).reshape(1, GDN_DV))


def kernel(x, c, ctx, c_ctx, ada_w, ada_b, ln_g, ln_b, ret_w_in, ret_decay, ret_w_out,
           gdn_w_in, gdn_conv, gdn_a_log, gdn_dt_bias, gdn_norm, gdn_w_out, ffn_w_in, ffn_w_out):
    bsz, n_lat, d = x.shape
    n_ctx = ctx.shape[1]
    depth = ada_w.shape[0]
    n_tot = n_ctx + n_lat
    alpha = (2 * depth) ** 0.25

    pad_rows = (-(bsz + 1)) % 16
    cc = jnp.concatenate([c, c_ctx[None], jnp.zeros((pad_rows, d), F32)], axis=0)
    mods = _ada_mod(cc, ada_w, ada_b)

    def layer_params(i):
        lat = mods[i, :bsz].reshape(bsz, 1, 6, d)
        cx = jnp.broadcast_to(mods[i, bsz].reshape(1, 1, 6, d), (bsz, 1, 6, d))
        return jnp.concatenate([cx, lat], axis=1)

    params = [layer_params(i) for i in range(depth)]
    h = (ctx, x)
    u = _modulate(h, params[0], shift=0, scale=1, n_ctx=n_ctx, n=n_tot)
    ctx_rows = n_ctx

    for i in range(depth):
        need_ctx = i < depth - 1
        j = i // 2
        u2 = u.reshape(bsz * n_tot, d)
        if i % 2 == 0:
            qkvg = _matmul(u2, ret_w_in, j, n_out=ret_w_in.shape[-1], tn=1024, out_dtype=BF16,
                           name="ret_in_proj").reshape(bsz, n_tot, -1)
            dk = qkvg.shape[-1] // (6 * RET_HEADS)
            cos, sin = _rope_tables(n_ctx, n_lat, dk)
            o = _retention(qkvg, ret_decay[j], cos, sin, n_ctx=n_ctx, need_ctx=need_ctx)
            w_out = ret_w_out
        else:
            qkv_w = gdn_conv.shape[-1]
            hv = gdn_a_log.shape[-1]
            main_w = qkv_w + hv * GDN_DV
            proj = _matmul(u2, gdn_w_in, j, n_out=main_w, tn=1024, out_dtype=BF16,
                           name="gdn_in_proj").reshape(bsz, n_tot, main_w)
            ba = _matmul(u2, gdn_w_in, j, n_out=4 * hv, col_off=main_w, tn=4 * hv, out_dtype=F32,
                         name="gdn_gate_proj").reshape(bsz, n_tot, 4 * hv)
            gates = _gdn_gates(ba, gdn_a_log[j], gdn_dt_bias[j])
            qkv = _gdn_conv(proj, gdn_conv[j], n_ctx=n_ctx, n_qk_heads=hv // 2)
            o = _gdn_scan(qkv, proj, gates, gdn_norm[j], n_ctx=n_ctx, need_ctx=need_ctx,
                          n_qk_heads=hv // 2, z_col_off=qkv_w)
            w_out = gdn_w_out

        n_rows = o.shape[1]
        out_ctx = n_ctx if need_ctx else 0
        y = _matmul(o.reshape(bsz * n_rows, -1), w_out, j, n_out=d, tn=512, out_dtype=F32,
                    name="mixer_out_proj").reshape(bsz, n_rows, d)
        h_off = ctx_rows - out_ctx
        if isinstance(h, tuple) and h_off:
            h, h_off = h[1], 0
        h, u = _resid_ln(h, y, params[i], ln_g[i, 0], ln_b[i, 0], params[i], alpha=alpha, gate=2,
                         shift=3, scale=4, n_ctx=out_ctx, h_row_offset=h_off)
        ctx_rows = out_ctx
        y = _ffn(u, ffn_w_in, ffn_w_out, i)
        if need_ctx:
            h, u = _resid_ln(h, y, params[i], ln_g[i, 1], ln_b[i, 1], params[i + 1], alpha=alpha, gate=5,
                             shift=0, scale=1, n_ctx=out_ctx)
        else:
            h = _resid_ln(h, y, params[i], ln_g[i, 1], ln_b[i, 1], params[i], alpha=alpha, gate=5,
                          shift=0, scale=1, n_ctx=out_ctx, with_u=False)
    return h
```

```python
import functools
import math

import jax
import jax.numpy as jnp
from jax import lax
from jax.experimental import pallas as pl
from jax.experimental.pallas import tpu as pltpu

F32 = jnp.float32
BF16 = jnp.bfloat16

GRID_W = 64
RET_HEADS = 8
RET_CHUNK = 256
RET_UNROLL = 3
ROPE_BASE = 10000.0
GDN_DK = 128
GDN_DV = 128
GDN_CONV = 5
GDN_CHUNK = 64
LN_EPS = 1e-5

VMEM_LIMIT_BYTES = 56 * 1024 * 1024
ROW_TILE = 256
MM_ROW_TILE = 1024


def _cparams(*sem):
    return pltpu.CompilerParams(dimension_semantics=sem, vmem_limit_bytes=VMEM_LIMIT_BYTES)


def _dot(a, b):
    return jnp.dot(a, b, preferred_element_type=F32)


def _dot_nt(a, b):
    return lax.dot_general(a, b, (((1,), (1,)), ((), ())), preferred_element_type=F32)


def _ada_kernel(c_ref, w_ref, b_ref, o_ref):
    a = jax.nn.silu(c_ref[...]).astype(BF16)
    o_ref[...] = _dot(a, w_ref[...].astype(BF16)) + b_ref[...]


def _ada_mod(cc, ada_w, ada_b, tn=1024):
    depth, d, n = ada_w.shape
    rows = cc.shape[0]
    return pl.pallas_call(
        _ada_kernel,
        grid=(depth, n // tn),
        in_specs=[
            pl.BlockSpec((rows, d), lambda l, j: (0, 0)),
            pl.BlockSpec((None, d, tn), lambda l, j: (l, 0, j)),
            pl.BlockSpec((None, 1, tn), lambda l, j: (l, 0, j)),
        ],
        out_specs=pl.BlockSpec((None, rows, tn), lambda l, j: (l, 0, j)),
        out_shape=jax.ShapeDtypeStruct((depth, rows, n), F32),
        compiler_params=_cparams("arbitrary", "arbitrary"),
        name="ada_mod",
    )(cc, ada_w, ada_b.reshape(depth, 1, n))


def _load_h(h_refs, ctx_tiles):
    if len(h_refs) == 1:
        return h_refs[0][...]
    return jnp.where(pl.program_id(1) < ctx_tiles, h_refs[0][...], h_refs[1][...])


def _h_operands(h, d, ctx_tiles, off=0):
    if not isinstance(h, tuple):
        return [h], [pl.BlockSpec((None, ROW_TILE, d), lambda b, r: (b, r + off, 0))]
    assert off == 0
    return list(h), [
        pl.BlockSpec((None, ROW_TILE, d), lambda b, r: (b, jnp.minimum(r, ctx_tiles - 1), 0)),
        pl.BlockSpec((None, ROW_TILE, d), lambda b, r: (b, jnp.maximum(r - ctx_tiles, 0), 0))]


def _mod_kernel(*refs, shift, scale, ctx_tiles):
    *h_refs, p_ref, u_ref = refs
    p = p_ref[...]
    h = _load_h(h_refs, ctx_tiles)
    u_ref[...] = (h * (1.0 + p[scale:scale + 1]) + p[shift:shift + 1]).astype(u_ref.dtype)


def _param_spec(d, seg_of_tile):
    return pl.BlockSpec((None, None, 6, d), lambda b, r: (b, seg_of_tile(r), 0, 0))


def _modulate(h, p, *, shift, scale, n_ctx, n):
    bsz, _, _, d = p.shape
    ctx_tiles = n_ctx // ROW_TILE
    seg = lambda r: jnp.where(r < ctx_tiles, 0, 1)
    h_args, h_specs = _h_operands(h, d, ctx_tiles)
    return pl.pallas_call(
        functools.partial(_mod_kernel, shift=shift, scale=scale, ctx_tiles=ctx_tiles),
        grid=(bsz, n // ROW_TILE),
        in_specs=h_specs + [_param_spec(d, seg)],
        out_specs=pl.BlockSpec((None, ROW_TILE, d), lambda b, r: (b, r, 0)),
        out_shape=jax.ShapeDtypeStruct((bsz, n, d), BF16),
        compiler_params=_cparams("arbitrary", "arbitrary"),
        name="modulate",
    )(*h_args, p)


def _resid_ln_kernel(*refs, alpha, gate, shift, scale, ctx_tiles, n_h):
    h_refs, (y_ref, p_ref, g_ref, b_ref, p2_ref, h_out, *u_out) = refs[:n_h], refs[n_h:]
    p = p_ref[...]
    x = alpha * _load_h(h_refs, ctx_tiles) + p[gate:gate + 1] * y_ref[...].astype(F32)
    mu = jnp.mean(x, axis=-1, keepdims=True)
    xc = x - mu
    var = jnp.mean(xc * xc, axis=-1, keepdims=True)
    hn = xc * lax.rsqrt(var + LN_EPS) * g_ref[...] + b_ref[...]
    h_out[...] = hn
    if u_out:
        p2 = p2_ref[...]
        u_out[0][...] = (hn * (1.0 + p2[scale:scale + 1]) + p2[shift:shift + 1]).astype(BF16)


def _resid_ln(h, y, p, ln_g, ln_b, p2, *, alpha, gate, shift, scale, n_ctx, h_row_offset=0, with_u=True):
    bsz, n, d = y.shape
    ctx_tiles = n_ctx // ROW_TILE
    seg = lambda r: jnp.where(r < ctx_tiles, 0, 1)
    h_args, h_specs = _h_operands(h, d, ctx_tiles, h_row_offset // ROW_TILE)
    row = pl.BlockSpec((None, ROW_TILE, d), lambda b, r: (b, r, 0))
    vec = pl.BlockSpec((1, d), lambda b, r: (0, 0))
    out_shape = [jax.ShapeDtypeStruct((bsz, n, d), F32)]
    out_specs = [row]
    if with_u:
        out_shape.append(jax.ShapeDtypeStruct((bsz, n, d), BF16))
        out_specs.append(row)
    outs = pl.pallas_call(
        functools.partial(_resid_ln_kernel, alpha=alpha, gate=gate, shift=shift, scale=scale,
                          ctx_tiles=ctx_tiles, n_h=len(h_args)),
        grid=(bsz, n // ROW_TILE),
        in_specs=h_specs + [row, _param_spec(d, seg), vec, vec, _param_spec(d, seg)],
        out_specs=out_specs,
        out_shape=out_shape,
        compiler_params=_cparams("arbitrary", "arbitrary"),
        name="resid_ln",
    )(*h_args, y, p, ln_g.reshape(1, d), ln_b.reshape(1, d), p2)
    return outs if with_u else outs[0]


def _mm_kernel(a_ref, w_ref, o_ref, wb_ref):
    @pl.when(pl.program_id(1) == 0)
    def _():
        wb_ref[...] = w_ref[...].astype(BF16)

    o_ref[...] = _dot(a_ref[...], wb_ref[...]).astype(o_ref.dtype)


def _matmul(a, w, layer, *, n_out, col_off=0, tn, tm=MM_ROW_TILE, out_dtype, name):
    m, k = a.shape
    assert m % tm == 0 and n_out % tn == 0 and col_off % tn == 0
    off = col_off // tn
    return pl.pallas_call(
        _mm_kernel,
        grid=(n_out // tn, m // tm),
        in_specs=[pl.BlockSpec((tm, k), lambda j, i: (i, 0)),
                  pl.BlockSpec((None, k, tn), lambda j, i: (layer, 0, j + off))],
        out_specs=pl.BlockSpec((tm, tn), lambda j, i: (i, j)),
        out_shape=jax.ShapeDtypeStruct((m, n_out), out_dtype),
        scratch_shapes=[pltpu.VMEM((k, tn), BF16)],
        compiler_params=_cparams("arbitrary", "arbitrary"),
        name=name,
    )(a, w)


def _swiglu_kernel(a_ref, wg_ref, wu_ref, o_ref, wgb_ref, wub_ref):
    @pl.when(pl.program_id(1) == 0)
    def _():
        wgb_ref[...] = wg_ref[...].astype(BF16)
        wub_ref[...] = wu_ref[...].astype(BF16)

    a = a_ref[...]
    gate = _dot(a, wgb_ref[...])
    up = _dot(a, wub_ref[...])
    o_ref[...] = (jax.nn.silu(gate) * up).astype(o_ref.dtype)


def _swiglu_in(a, w, layer, *, tn=512, tm=MM_ROW_TILE):
    m, k = a.shape
    f = w.shape[2] // 2
    assert m % tm == 0 and f % tn == 0
    nb = f // tn
    return pl.pallas_call(
        _swiglu_kernel,
        grid=(nb, m // tm),
        in_specs=[pl.BlockSpec((tm, k), lambda j, i: (i, 0)),
                  pl.BlockSpec((None, k, tn), lambda j, i: (layer, 0, j)),
                  pl.BlockSpec((None, k, tn), lambda j, i: (layer, 0, j + nb))],
        out_specs=pl.BlockSpec((tm, tn), lambda j, i: (i, j)),
        out_shape=jax.ShapeDtypeStruct((m, f), BF16),
        scratch_shapes=[pltpu.VMEM((k, tn), BF16), pltpu.VMEM((k, tn), BF16)],
        compiler_params=_cparams("arbitrary", "arbitrary"),
        name="ffn_in_swiglu",
    )(a, w, w)


def _ffn(u, w_in, w_out, layer):
    bsz, n, d = u.shape
    a = _swiglu_in(u.reshape(bsz * n, d), w_in, layer)
    y = _matmul(a, w_out, layer, n_out=d, tn=512, tm=512, out_dtype=F32, name="ffn_out")
    return y.reshape(bsz, n, d)


def _ret_kernel(draw_ref, q_ref, k_ref, v_ref, g_ref, cos_ref, sin_ref, o_ref,
                qs_ref, ks_ref, oacc_ref, sf_ref, sb_ref, *, n_chunks, ctx_chunks, skip_chunks, dk):
    C = RET_CHUNK
    h = pl.program_id(1)
    half = dk // 2
    lg_f = -jnp.exp(jnp.full((1, 1), draw_ref[0, h], F32))
    lg_b = -jnp.exp(jnp.full((1, 1), draw_ref[1, h], F32))

    ri = lax.broadcasted_iota(jnp.int32, (C, C), 0)
    ci = lax.broadcasted_iota(jnp.int32, (C, C), 1)
    diff = (ri - ci).astype(F32)
    lower = diff >= 0
    dmat = jnp.where(lower, jnp.exp(lg_f * jnp.where(lower, diff, 0.0)),
                     jnp.exp(lg_b * jnp.where(lower, 0.0, -diff)))
    rc = lax.broadcasted_iota(jnp.int32, (C, 1), 0).astype(F32)
    qdec_f = jnp.exp(lg_f * (rc + 1.0))
    kdec_f = jnp.exp(lg_f * (C - 1.0 - rc))
    cdec_f = jnp.exp(lg_f * C)
    qdec_b = jnp.exp(lg_b * (C - rc))
    kdec_b = jnp.exp(lg_b * rc)
    cdec_b = jnp.exp(lg_b * C)

    sf_ref[...] = jnp.zeros_like(sf_ref)
    sb_ref[...] = jnp.zeros_like(sb_ref)
    qscale = dk ** -0.5

    def rope(x, cs, sn):
        x1, x2 = x[:, :half], x[:, half:]
        return jnp.concatenate([x1 * cs - x2 * sn, x1 * sn + x2 * cs], axis=-1)

    def fwd(c, carry):
        rows = pl.ds(pl.multiple_of(c * C, C), C)
        cs, sn = cos_ref[rows, :], sin_ref[rows, :]
        q = rope(q_ref[rows, :].astype(F32), cs, sn) * qscale
        k = rope(k_ref[rows, :].astype(F32), cs, sn)
        qb, kb = q.astype(BF16), k.astype(BF16)
        qs_ref[rows, :] = qb
        ks_ref[rows, :] = kb
        v = v_ref[rows, :]
        s = (_dot_nt(qb, kb) * dmat).astype(BF16)
        state = sf_ref[...]
        oacc_ref[rows, :] = _dot(s, v) + _dot(qb, state.astype(BF16)) * qdec_f
        kd_t = jnp.transpose(k * kdec_f).astype(BF16)
        sf_ref[...] = cdec_f * state + _dot(kd_t, v)
        return carry

    lax.fori_loop(0, n_chunks, fwd, 0, unroll=RET_UNROLL)

    def bwd(t, carry):
        c = jnp.where(t < ctx_chunks, ctx_chunks - 1 - t, n_chunks - 1 + ctx_chunks - t)
        rows = pl.ds(pl.multiple_of(c * C, C), C)
        qb, kb, v = qs_ref[rows, :], ks_ref[rows, :], v_ref[rows, :]
        state = sb_ref[...]
        o = oacc_ref[rows, :] + _dot(qb, state.astype(BF16)) * qdec_b
        kd_t = jnp.transpose(kb.astype(F32) * kdec_b).astype(BF16)
        sb_ref[...] = cdec_b * state + _dot(kd_t, v)

        def emit():
            mu = jnp.mean(o, axis=-1, keepdims=True)
            oc = o - mu
            var = jnp.mean(oc * oc, axis=-1, keepdims=True)
            on = oc * lax.rsqrt(var + LN_EPS)
            out_rows = pl.ds(pl.multiple_of((c - skip_chunks) * C, C), C)
            o_ref[out_rows, :] = (on * jax.nn.silu(g_ref[rows, :].astype(F32))).astype(o_ref.dtype)

        if skip_chunks:
            pl.when(c >= skip_chunks)(emit)
        else:
            emit()
        return carry

    lax.fori_loop(0, n_chunks, bwd, 0, unroll=RET_UNROLL)


def _retention(qkvg, decay_raw, cos, sin, *, n_ctx, need_ctx):
    bsz, n, width = qkvg.shape
    H = RET_HEADS
    dk = width // (6 * H)
    dv = 2 * dk
    C = RET_CHUNK
    assert n % C == 0 and n_ctx % C == 0
    skip = 0 if need_ctx else n_ctx // C
    n_out = n - skip * C
    kern = functools.partial(_ret_kernel, n_chunks=n // C, ctx_chunks=n_ctx // C, skip_chunks=skip, dk=dk)
    return pl.pallas_call(
        kern,
        grid=(bsz, H),
        in_specs=[
            pl.BlockSpec(memory_space=pltpu.SMEM),
            pl.BlockSpec((None, n, dk), lambda b, h: (b, 0, h)),
            pl.BlockSpec((None, n, dk), lambda b, h: (b, 0, H + h)),
            pl.BlockSpec((None, n, dv), lambda b, h: (b, 0, H + h)),
            pl.BlockSpec((None, n, dv), lambda b, h: (b, 0, 2 * H + h)),
            pl.BlockSpec((n, dk // 2), lambda b, h: (0, 0)),
            pl.BlockSpec((n, dk // 2), lambda b, h: (0, 0)),
        ],
        out_specs=pl.BlockSpec((None, n_out, dv), lambda b, h: (b, 0, h)),
        out_shape=jax.ShapeDtypeStruct((bsz, n_out, H * dv), BF16),
        scratch_shapes=[pltpu.VMEM((n, dk), BF16), pltpu.VMEM((n, dk), BF16), pltpu.VMEM((n, dv), F32),
                        pltpu.VMEM((dk, dv), F32), pltpu.VMEM((dk, dv), F32)],
        compiler_params=_cparams("arbitrary", "arbitrary"),
        name="retention",
    )(decay_raw.astype(F32), qkvg, qkvg, qkvg, qkvg, cos, sin)


def _rope_tables(n_ctx, n_lat, dim):
    rows = n_lat // GRID_W
    row = jnp.repeat(jnp.arange(rows, dtype=F32), GRID_W)
    col = jnp.tile(jnp.arange(GRID_W, dtype=F32), rows)
    n_freq = dim // 4
    inv_freq = ROPE_BASE ** (-jnp.arange(n_freq, dtype=F32) / n_freq)
    ang = jnp.concatenate([row[:, None] * inv_freq, col[:, None] * inv_freq], axis=-1)
    cos = jnp.concatenate([jnp.ones((n_ctx, dim // 2), F32), jnp.cos(ang)], axis=0)
    sin = jnp.concatenate([jnp.zeros((n_ctx, dim // 2), F32), jnp.sin(ang)], axis=0)
    return cos, sin


def _conv_kernel(x_ref, w_ref, o_ref, *, n_ctx, n_q_blocks, n_qk_blocks, qscale):
    j = pl.program_id(1)
    x = x_ref[...].astype(F32)
    n, tc = x.shape
    w = w_ref[...]
    pad = GDN_CONV // 2
    t = lax.broadcasted_iota(jnp.int32, (n, 1), 0)
    lo = jnp.where(t < n_ctx, 0, n_ctx)
    hi = jnp.where(t < n_ctx, n_ctx, n)
    acc = x * w[pad:pad + 1]
    for d in range(-pad, pad + 1):
        if d == 0:
            continue
        xs = pltpu.roll(x, (n - d) % n, 0)
        valid = (t + d >= lo) & (t + d < hi)
        acc = acc + jnp.where(valid, xs, 0.0) * w[d + pad:d + pad + 1]
    y = jax.nn.silu(acc)
    do_norm = j < n_qk_blocks
    scale = jnp.where(j < n_q_blocks, qscale, 1.0).astype(F32)
    for g0 in range(0, tc, GDN_DK):
        yg = y[:, g0:g0 + GDN_DK]
        ss = jnp.sum(yg * yg, axis=-1, keepdims=True)
        yn = yg * (lax.rsqrt(ss + 1e-6) * scale)
        o_ref[:, g0:g0 + GDN_DK] = jnp.where(do_norm, yn, yg).astype(o_ref.dtype)


def _gdn_conv(proj, conv_w, *, n_ctx, n_qk_heads, tc=512):
    bsz, n, _ = proj.shape
    width = conv_w.shape[1]
    qw = n_qk_heads * GDN_DK
    assert qw % tc == 0 and width % tc == 0
    kern = functools.partial(_conv_kernel, n_ctx=n_ctx, n_q_blocks=qw // tc, n_qk_blocks=2 * qw // tc,
                             qscale=GDN_DK ** -0.5)
    return pl.pallas_call(
        kern,
        grid=(bsz, width // tc),
        in_specs=[pl.BlockSpec((None, n, tc), lambda b, j: (b, 0, j)),
                  pl.BlockSpec((GDN_CONV, tc), lambda b, j: (0, j))],
        out_specs=pl.BlockSpec((None, n, tc), lambda b, j: (b, 0, j)),
        out_shape=jax.ShapeDtypeStruct((bsz, n, width), BF16),
        compiler_params=_cparams("arbitrary", "arbitrary"),
        name="gdn_conv",
    )(proj, conv_w)


def _gates_kernel(x_ref, alog_ref, dt_ref, o_ref, *, n_heads):
    C = GDN_CHUNK
    x = x_ref[...]
    n = x.shape[0]
    beta = jax.nn.sigmoid(x)
    z = x + dt_ref[...]
    softplus = jnp.maximum(z, 0.0) + jnp.log1p(jnp.exp(-jnp.abs(z)))
    la = -jnp.exp(alog_ref[...]) * softplus
    tm = lax.broadcasted_iota(jnp.int32, (n, 1), 0) % C
    pre = la
    suf = la
    s = 1
    while s < C:
        pre = pre + jnp.where(tm >= s, pltpu.roll(pre, s, 0), 0.0)
        suf = suf + jnp.where(tm < C - s, pltpu.roll(suf, n - s, 0), 0.0)
        s *= 2
    lane = lax.broadcasted_iota(jnp.int32, (1, x.shape[1]), 1)
    o_ref[...] = jnp.where(lane < 2 * n_heads, beta, jnp.where(lane < 3 * n_heads, pre, suf))


def _gdn_gates(ba, a_log, dt_bias):
    bsz, n, w = ba.shape
    n_heads = w // 4
    zeros = jnp.zeros((2 * n_heads,), F32)
    alog = jnp.concatenate([zeros, a_log.astype(F32).reshape(-1)]).reshape(1, w)
    dt = jnp.concatenate([zeros, dt_bias.astype(F32).reshape(-1)]).reshape(1, w)
    return pl.pallas_call(
        functools.partial(_gates_kernel, n_heads=n_heads),
        grid=(bsz,),
        in_specs=[pl.BlockSpec((None, n, w), lambda b: (b, 0, 0)),
                  pl.BlockSpec((1, w), lambda b: (0, 0)), pl.BlockSpec((1, w), lambda b: (0, 0))],
        out_specs=pl.BlockSpec((None, n, w), lambda b: (b, 0, 0)),
        out_shape=jax.ShapeDtypeStruct((bsz, n, w), F32),
        compiler_params=_cparams("arbitrary"),
        name="gdn_gates",
    )(ba, alog, dt)


GDN_INST = 4
GDN_STACK = GDN_INST * GDN_CHUNK
GDN_PREP_CHUNKS = 6
GDN_EMIT_ROWS = 256
GDN_HEADS_PER_STEP = 2


GDN_INV_BASE = 8
GDN_INV_LEVELS = (16, 32, 64)


def _merge_rows(g, lower):
    h = g // 2
    return [(r0 + h, r0 + g) if lower else (r0, r0 + h) for r0 in range(0, GDN_CHUNK, g)]


def _unit_tri_inverse(a_list, lower_list, same_base, off_masks, eye, between):
    def mm(x, y):
        return _dot(x.astype(BF16), y.astype(BF16))

    ad = [a * same_base for a in a_list]
    x = [eye - d for d in ad]
    p = [mm(d, d) for d in ad]
    between()
    xp = [mm(xi, pi) for xi, pi in zip(x, p)]
    p = [mm(pi, pi) for pi in p]
    between()
    x = [xi + xpi for xi, xpi in zip(x, xp)]
    x = [xi + mm(xi, pi) for xi, pi in zip(x, p)]
    between()
    for g, off in zip(GDN_INV_LEVELS, off_masks):
        h = g // 2
        sels = [_merge_rows(g, lower) for lower in lower_list]
        top = [jnp.concatenate([xi[a:b] for a, b in sel], axis=0) for xi, sel in zip(x, sels)]
        n = [mm(ti, a * off) for ti, a in zip(top, a_list)]
        between()
        upd = [ti - mm(ni, xi) for ti, ni, xi in zip(top, n, x)]
        between()
        merged = []
        for xi, ui, sel in zip(x, upd, sels):
            pieces = []
            for idx, (a, b) in enumerate(sel):
                new = ui[idx * h:(idx + 1) * h]
                r0 = (a // g) * g
                pieces += [xi[r0:r0 + h], new] if a != r0 else [new, xi[r0 + h:r0 + g]]
            merged.append(jnp.concatenate(pieces, axis=0))
        x = merged
    return x


def _gdn_kernel(q_ref, k_ref, v_ref, z_ref, gcol_ref, grow_ref, nw_ref, o_ref,
                mask_ref, kt_ref, gq_ref, r_ref, o0_ref, of_ref, ob_ref, m_ref,
                *, n_chunks, ctx_chunks, skip_chunks):
    C = GDN_CHUNK
    S = GDN_INST
    dk = GDN_DK
    dv = GDN_DV
    n_lvl = len(GDN_INV_LEVELS)
    M_INCL, M_STRICT, M_BASE, M_OFF, M_EYE = 0, 2, 4, 5, 5 + n_lvl

    r = lax.broadcasted_iota(jnp.int32, (C, C), 0)
    c = lax.broadcasted_iota(jnp.int32, (C, C), 1)
    mask_ref[M_INCL] = (r >= c).astype(F32)
    mask_ref[M_INCL + 1] = (c >= r).astype(F32)
    mask_ref[M_STRICT] = (r > c).astype(F32)
    mask_ref[M_STRICT + 1] = (c > r).astype(F32)
    mask_ref[M_BASE] = ((r // GDN_INV_BASE) == (c // GDN_INV_BASE)).astype(F32)
    for lvl, g in enumerate(GDN_INV_LEVELS):
        h = g // 2
        mask_ref[M_OFF + lvl] = (((r // g) == (c // g)) & ((r // h) != (c // h))).astype(F32)
    mask_ref[M_EYE] = (r == c).astype(F32)

    P = GDN_HEADS_PER_STEP

    def lanes(start):
        return pl.ds(pl.multiple_of(start, dk), dk)

    def xpose(p, carry):
        rows = pl.ds(pl.multiple_of(p * 2 * C, 2 * C), 2 * C)
        for hh in range(P):
            kt = jnp.transpose(k_ref[rows, hh * dk:(hh + 1) * dk].astype(F32)).astype(BF16)
            kt_ref[hh, 2 * p] = kt[:, :C]
            kt_ref[hh, 2 * p + 1] = kt[:, C:]
        return carry

    lax.fori_loop(0, n_chunks // 2, xpose, 0, unroll=3)

    NB = GDN_PREP_CHUNKS
    n_blocks = n_chunks // NB

    def scan_chunks(t):
        cb = jnp.where(t < ctx_chunks, ctx_chunks - 1 - t, n_chunks - 1 + ctx_chunks - t)
        return t, cb

    def prep_block(g, buf, between):
        hh, j = g // n_blocks, g % n_blocks
        items = []
        for i in range(NB):
            for d, ci in enumerate(scan_chunks(j * NB + i)):
                rows = pl.ds(pl.multiple_of(ci * C, C), C)
                qb, kb = q_ref[rows, lanes(hh * dk)], k_ref[rows, lanes(hh * dk)]
                pk = _dot_nt(jnp.concatenate([kb, qb], axis=0), kb)
                kk, qk = pk[:C], pk[C:]
                g8 = gcol_ref[hh, rows, :]
                grow8 = grow_ref[hh, ci]
                qf, kf = qb.astype(F32), kb.astype(F32)
                kt = kt_ref[hh, ci].astype(F32)
                for vj in range(2):
                    s = 2 * d + vj
                    vf = v_ref[rows, lanes((2 * hh + vj) * dv)].astype(F32)
                    beta, gc = g8[:, s:s + 1], g8[:, S + s:S + s + 1]
                    g_row = grow8[S + s:S + s + 1, :]
                    e = jnp.exp(jnp.minimum(gc - g_row, 0.0))
                    a = beta * kk * (e * mask_ref[M_STRICT + d])
                    qkm = (qk * (e * mask_ref[M_INCL + d])).astype(BF16)
                    eg = jnp.exp(gc)
                    rhs = jnp.concatenate([vf * beta, kf * (beta * eg)], axis=-1)
                    g_last = g_row[:, 0:1] if d else g_row[:, C - 1:C]
                    kdt = (kt * jnp.exp(g_last - g_row)).astype(BF16)
                    items.append(dict(slot=i, s=s, a=a, qkm=qkm, rhs=rhs.astype(BF16), qd=qf * eg, kdt=kdt))
        between()

        xs = _unit_tri_inverse([it["a"] for it in items], [it["s"] < 2 for it in items], mask_ref[M_BASE],
                               [mask_ref[M_OFF + i] for i in range(n_lvl)], mask_ref[M_EYE], between)
        sols = [_dot(x.astype(BF16), it["rhs"]).astype(BF16) for x, it in zip(xs, items)]
        between()
        ows = [_dot(it["qkm"], sol) for it, sol in zip(items, sols)]
        rgs = [_dot(it["kdt"], sol) for it, sol in zip(items, sols)]
        between()
        for it, ow, rg in zip(items, ows, rgs):
            s, slot = it["s"], it["slot"]
            gq_ref[buf, s, slot, 0:dk, :] = rg[:, dv:].astype(BF16)
            gq_ref[buf, s, slot, dk:dk + C, :] = (it["qd"] - ow[:, dv:]).astype(BF16)
            r_ref[buf, s, slot] = rg[:, :dv].astype(BF16)
            o0_ref[buf, s, slot] = ow[:, :dv].astype(BF16)

    def scan_step(g, buf, i):
        hh, j = g // n_blocks, g % n_blocks
        cf, cb = scan_chunks(j * NB + i)
        cis = [cf, cf, cb, cb]
        ms = [m_ref[s] for s in range(S)]
        if i == 0:
            keep = jnp.where(j == 0, 0.0, 1.0).astype(F32)
            ms = [m * keep for m in ms]
        pq = [_dot(gq_ref[buf, s, i], ms[s].astype(BF16)) for s in range(S)]
        for s in range(S):
            ci = cis[s]
            g_s = grow_ref[hh, ci][S + s:S + s + 1, :]
            g_last = g_s[:, C - 1:C] if s < 2 else g_s[:, 0:1]
            m_ref[s] = jnp.exp(g_last) * ms[s] - pq[s][:dk] + r_ref[buf, s, i].astype(F32)
            o = pq[s][dk:] + o0_ref[buf, s, i].astype(F32)
            rows = pl.ds(pl.multiple_of(ci * C, C), C)
            if s < 2:
                of_ref[hh, s, rows, :] = o
            else:
                ob_ref[hh, s - 2, rows, :] = o

    m_ref[...] = jnp.zeros_like(m_ref)
    prep_block(0, 0, lambda: None)

    def pipelined(g, carry):
        buf = g % 2
        pending = [functools.partial(scan_step, g, buf, i) for i in range(NB)]

        def between():
            if pending:
                pending.pop(0)()

        prep_block(g + 1, 1 - buf, between)
        while pending:
            pending.pop(0)()
        return carry

    last = P * n_blocks - 1
    lax.fori_loop(0, last, pipelined, 0)
    for i in range(NB):
        scan_step(last, last % 2, i)

    nw = nw_ref[...]

    E = GDN_EMIT_ROWS

    def emit(i, carry):
        rows = pl.ds(pl.multiple_of(i * E, E), E)
        out_rows = pl.ds(pl.multiple_of(i * E - skip_chunks * C, E), E)
        for hh in range(P):
            for vj in range(2):
                cols = slice((2 * hh + vj) * dv, (2 * hh + vj + 1) * dv)
                o = of_ref[hh, vj, rows, :] + ob_ref[hh, vj, rows, :]
                zf = z_ref[rows, cols].astype(F32)
                on = o * lax.rsqrt(jnp.mean(o * o, axis=-1, keepdims=True) + 1e-6) * nw * jax.nn.silu(zf)
                o_ref[out_rows, cols] = on.astype(o_ref.dtype)
        return carry

    lax.fori_loop(skip_chunks * C // E, n_chunks * C // E, emit, 0)


def _gdn_scan(qkv, proj, gates, norm_w, *, n_ctx, need_ctx, n_qk_heads, z_col_off):
    bsz, n, _ = qkv.shape
    C = GDN_CHUNK
    P = GDN_HEADS_PER_STEP
    HQ = n_qk_heads
    HV = gates.shape[-1] // 4
    assert HV == 2 * HQ and HQ % P == 0 and n % (GDN_PREP_CHUNKS * C) == 0 and n_ctx % C == 0
    assert n % GDN_EMIT_ROWS == 0 and n_ctx % GDN_EMIT_ROWS == 0 and (n // C) % 2 == 0
    nch = n // C
    skip = 0 if need_ctx else n_ctx // C
    n_out = n - skip * C
    g5 = gates.reshape(bsz, n, 4, HQ, 2)
    gcol = g5.transpose(0, 3, 1, 2, 4).reshape(bsz, HQ, n, 2 * GDN_INST)
    grow = gates.reshape(bsz, nch, C, 4, HQ, 2).transpose(0, 4, 1, 3, 5, 2).reshape(bsz, HQ, nch, 2 * GDN_INST, C)
    qk_w = P * GDN_DK
    v_w = P * 2 * GDN_DV
    assert z_col_off % v_w == 0
    k_blk, v_blk, z_blk = HQ * GDN_DK // qk_w, 2 * HQ * GDN_DK // v_w, z_col_off // v_w
    kern = functools.partial(_gdn_kernel, n_chunks=nch, ctx_chunks=n_ctx // C, skip_chunks=skip)
    return pl.pallas_call(
        kern,
        grid=(bsz, HQ // P),
        in_specs=[
            pl.BlockSpec((None, n, qk_w), lambda b, h: (b, 0, h)),
            pl.BlockSpec((None, n, qk_w), lambda b, h: (b, 0, k_blk + h)),
            pl.BlockSpec((None, n, v_w), lambda b, h: (b, 0, v_blk + h)),
            pl.BlockSpec((None, n, v_w), lambda b, h: (b, 0, z_blk + h)),
            pl.BlockSpec((None, P, n, 2 * GDN_INST), lambda b, h: (b, h, 0, 0)),
            pl.BlockSpec((None, P, nch, 2 * GDN_INST, C), lambda b, h: (b, h, 0, 0, 0)),
            pl.BlockSpec((1, GDN_DV), lambda b, h: (0, 0)),
        ],
        out_specs=pl.BlockSpec((None, n_out, v_w), lambda b, h: (b, 0, h)),
        out_shape=jax.ShapeDtypeStruct((bsz, n_out, HV * GDN_DV), BF16),
        scratch_shapes=[pltpu.VMEM((6 + len(GDN_INV_LEVELS), C, C), F32),
                        pltpu.VMEM((P, nch, GDN_DK, C), BF16),
                        pltpu.VMEM((2, GDN_INST, GDN_PREP_CHUNKS, GDN_DK + C, GDN_DV), BF16),
                        pltpu.VMEM((2, GDN_INST, GDN_PREP_CHUNKS, GDN_DK, GDN_DV), BF16),
                        pltpu.VMEM((2, GDN_INST, GDN_PREP_CHUNKS, C, GDN_DV), BF16),
                        pltpu.VMEM((P, 2, n, GDN_DV), F32), pltpu.VMEM((P, 2, n, GDN_DV), F32),
                        pltpu.VMEM((GDN_INST, GDN_DK, GDN_DV), F32)],
        compiler_params=_cparams("arbitrary", "arbitrary"),
        name="gdn_scan",
    )(qkv, qkv, qkv, proj, gcol, grow, norm_w.astype(F32).reshape(1, GDN_DV))


def kernel(x, c, ctx, c_ctx, ada_w, ada_b, ln_g, ln_b, ret_w_in, ret_decay, ret_w_out,
           gdn_w_in, gdn_conv, gdn_a_log, gdn_dt_bias, gdn_norm, gdn_w_out, ffn_w_in, ffn_w_out):
    bsz, n_lat, d = x.shape
    n_ctx = ctx.shape[1]
    depth = ada_w.shape[0]
    n_tot = n_ctx + n_lat
    alpha = (2 * depth) ** 0.25

    pad_rows = (-(bsz + 1)) % 16
    cc = jnp.concatenate([c, c_ctx[None], jnp.zeros((pad_rows, d), F32)], axis=0)
    mods = _ada_mod(cc, ada_w, ada_b)

    def layer_params(i):
        lat = mods[i, :bsz].reshape(bsz, 1, 6, d)
        cx = jnp.broadcast_to(mods[i, bsz].reshape(1, 1, 6, d), (bsz, 1, 6, d))
        return jnp.concatenate([cx, lat], axis=1)

    params = [layer_params(i) for i in range(depth)]
    h = (ctx, x)
    u = _modulate(h, params[0], shift=0, scale=1, n_ctx=n_ctx, n=n_tot)
    ctx_rows = n_ctx

    for i in range(depth):
        need_ctx = i < depth - 1
        j = i // 2
        u2 = u.reshape(bsz * n_tot, d)
        if i % 2 == 0:
            qkvg = _matmul(u2, ret_w_in, j, n_out=ret_w_in.shape[-1], tn=1024, out_dtype=BF16,
                           name="ret_in_proj").reshape(bsz, n_tot, -1)
            dk = qkvg.shape[-1] // (6 * RET_HEADS)
            cos, sin = _rope_tables(n_ctx, n_lat, dk)
            o = _retention(qkvg, ret_decay[j], cos, sin, n_ctx=n_ctx, need_ctx=need_ctx)
            w_out = ret_w_out
        else:
            qkv_w = gdn_conv.shape[-1]
            hv = gdn_a_log.shape[-1]
            main_w = qkv_w + hv * GDN_DV
            proj = _matmul(u2, gdn_w_in, j, n_out=main_w, tn=1024, out_dtype=BF16,
                           name="gdn_in_proj").reshape(bsz, n_tot, main_w)
            ba = _matmul(u2, gdn_w_in, j, n_out=4 * hv, col_off=main_w, tn=4 * hv, out_dtype=F32,
                         name="gdn_gate_proj").reshape(bsz, n_tot, 4 * hv)
            gates = _gdn_gates(ba, gdn_a_log[j], gdn_dt_bias[j])
            qkv = _gdn_conv(proj, gdn_conv[j], n_ctx=n_ctx, n_qk_heads=hv // 2)
            o = _gdn_scan(qkv, proj, gates, gdn_norm[j], n_ctx=n_ctx, need_ctx=need_ctx,
                          n_qk_heads=hv // 2, z_col_off=qkv_w)
            w_out = gdn_w_out

        n_rows = o.shape[1]
        out_ctx = n_ctx if need_ctx else 0
        y = _matmul(o.reshape(bsz * n_rows, -1), w_out, j, n_out=d, tn=512, out_dtype=F32,
                    name="mixer_out_proj").reshape(bsz, n_rows, d)
        h_off = ctx_rows - out_ctx
        if isinstance(h, tuple) and h_off:
            h, h_off = h[1], 0
        h, u = _resid_ln(h, y, params[i], ln_g[i, 0], ln_b[i, 0], params[i], alpha=alpha, gate=2,
                         shift=3, scale=4, n_ctx=out_ctx, h_row_offset=h_off)
        ctx_rows = out_ctx
        y = _ffn(u, ffn_w_in, ffn_w_out, i)
        if need_ctx:
            h, u = _resid_ln(h, y, params[i], ln_g[i, 1], ln_b[i, 1], params[i + 1], alpha=alpha, gate=5,
                             shift=0, scale=1, n_ctx=out_ctx)
        else:
            h = _resid_ln(h, y, params[i], ln_g[i, 1], ln_b[i, 1], params[i], alpha=alpha, gate=5,
                          shift=0, scale=1, n_ctx=out_ctx, with_u=False)
    return h
```

```python
import functools
import math

import jax
import jax.numpy as jnp
from jax import lax
from jax.experimental import pallas as pl
from jax.experimental.pallas import tpu as pltpu

F32 = jnp.float32
BF16 = jnp.bfloat16

GRID_W = 64
RET_HEADS = 8
RET_CHUNK = 256
RET_UNROLL = 3
ROPE_BASE = 10000.0
GDN_DK = 128
GDN_DV = 128
GDN_CONV = 5
GDN_CHUNK = 64
LN_EPS = 1e-5

VMEM_LIMIT_BYTES = 56 * 1024 * 1024
ROW_TILE = 256
MM_ROW_TILE = 1024
MM_ROW_TILE_CAP = 1536


def _row_tile(m):
    return max(t for t in range(ROW_TILE, MM_ROW_TILE_CAP + 1, ROW_TILE) if m % t == 0)


def _cparams(*sem):
    return pltpu.CompilerParams(dimension_semantics=sem, vmem_limit_bytes=VMEM_LIMIT_BYTES)


def _dot(a, b):
    return jnp.dot(a, b, preferred_element_type=F32)


def _dot_nt(a, b):
    return lax.dot_general(a, b, (((1,), (1,)), ((), ())), preferred_element_type=F32)


def _ada_kernel(c_ref, w_ref, b_ref, o_ref):
    a = jax.nn.silu(c_ref[...]).astype(BF16)
    o_ref[...] = _dot(a, w_ref[...].astype(BF16)) + b_ref[...]


def _ada_mod(cc, ada_w, ada_b, tn=1024):
    depth, d, n = ada_w.shape
    rows = cc.shape[0]
    return pl.pallas_call(
        _ada_kernel,
        grid=(depth, n // tn),
        in_specs=[
            pl.BlockSpec((rows, d), lambda l, j: (0, 0)),
            pl.BlockSpec((None, d, tn), lambda l, j: (l, 0, j)),
            pl.BlockSpec((None, 1, tn), lambda l, j: (l, 0, j)),
        ],
        out_specs=pl.BlockSpec((None, rows, tn), lambda l, j: (l, 0, j)),
        out_shape=jax.ShapeDtypeStruct((depth, rows, n), F32),
        compiler_params=_cparams("arbitrary", "arbitrary"),
        name="ada_mod",
    )(cc, ada_w, ada_b.reshape(depth, 1, n))


def _load_h(h_refs, ctx_tiles):
    if len(h_refs) == 1:
        return h_refs[0][...]
    return jnp.where(pl.program_id(1) < ctx_tiles, h_refs[0][...], h_refs[1][...])


def _h_operands(h, d, ctx_tiles, off=0):
    if not isinstance(h, tuple):
        return [h], [pl.BlockSpec((None, ROW_TILE, d), lambda b, r: (b, r + off, 0))]
    assert off == 0
    return list(h), [
        pl.BlockSpec((None, ROW_TILE, d), lambda b, r: (b, jnp.minimum(r, ctx_tiles - 1), 0)),
        pl.BlockSpec((None, ROW_TILE, d), lambda b, r: (b, jnp.maximum(r - ctx_tiles, 0), 0))]


def _mod_kernel(*refs, shift, scale, ctx_tiles):
    *h_refs, p_ref, u_ref = refs
    p = p_ref[...]
    h = _load_h(h_refs, ctx_tiles)
    u_ref[...] = (h * (1.0 + p[scale:scale + 1]) + p[shift:shift + 1]).astype(u_ref.dtype)


def _param_spec(d, seg_of_tile):
    return pl.BlockSpec((None, None, 6, d), lambda b, r: (b, seg_of_tile(r), 0, 0))


def _modulate(h, p, *, shift, scale, n_ctx, n):
    bsz, _, _, d = p.shape
    ctx_tiles = n_ctx // ROW_TILE
    seg = lambda r: jnp.where(r < ctx_tiles, 0, 1)
    h_args, h_specs = _h_operands(h, d, ctx_tiles)
    return pl.pallas_call(
        functools.partial(_mod_kernel, shift=shift, scale=scale, ctx_tiles=ctx_tiles),
        grid=(bsz, n // ROW_TILE),
        in_specs=h_specs + [_param_spec(d, seg)],
        out_specs=pl.BlockSpec((None, ROW_TILE, d), lambda b, r: (b, r, 0)),
        out_shape=jax.ShapeDtypeStruct((bsz, n, d), BF16),
        compiler_params=_cparams("arbitrary", "arbitrary"),
        name="modulate",
    )(*h_args, p)


def _resid_ln_kernel(*refs, alpha, gate, shift, scale, ctx_tiles, n_h):
    h_refs, (y_ref, p_ref, g_ref, b_ref, p2_ref, h_out, *u_out) = refs[:n_h], refs[n_h:]
    p = p_ref[...]
    x = alpha * _load_h(h_refs, ctx_tiles) + p[gate:gate + 1] * y_ref[...].astype(F32)
    mu = jnp.mean(x, axis=-1, keepdims=True)
    xc = x - mu
    var = jnp.mean(xc * xc, axis=-1, keepdims=True)
    hn = xc * lax.rsqrt(var + LN_EPS) * g_ref[...] + b_ref[...]
    h_out[...] = hn
    if u_out:
        p2 = p2_ref[...]
        u_out[0][...] = (hn * (1.0 + p2[scale:scale + 1]) + p2[shift:shift + 1]).astype(BF16)


def _resid_ln(h, y, p, ln_g, ln_b, p2, *, alpha, gate, shift, scale, n_ctx, h_row_offset=0, with_u=True):
    bsz, n, d = y.shape
    ctx_tiles = n_ctx // ROW_TILE
    seg = lambda r: jnp.where(r < ctx_tiles, 0, 1)
    h_args, h_specs = _h_operands(h, d, ctx_tiles, h_row_offset // ROW_TILE)
    row = pl.BlockSpec((None, ROW_TILE, d), lambda b, r: (b, r, 0))
    vec = pl.BlockSpec((1, d), lambda b, r: (0, 0))
    out_shape = [jax.ShapeDtypeStruct((bsz, n, d), F32)]
    out_specs = [row]
    if with_u:
        out_shape.append(jax.ShapeDtypeStruct((bsz, n, d), BF16))
        out_specs.append(row)
    outs = pl.pallas_call(
        functools.partial(_resid_ln_kernel, alpha=alpha, gate=gate, shift=shift, scale=scale,
                          ctx_tiles=ctx_tiles, n_h=len(h_args)),
        grid=(bsz, n // ROW_TILE),
        in_specs=h_specs + [row, _param_spec(d, seg), vec, vec, _param_spec(d, seg)],
        out_specs=out_specs,
        out_shape=out_shape,
        compiler_params=_cparams("arbitrary", "arbitrary"),
        name="resid_ln",
    )(*h_args, y, p, ln_g.reshape(1, d), ln_b.reshape(1, d), p2)
    return outs if with_u else outs[0]


def _mm_kernel(a_ref, w_ref, o_ref, wb_ref):
    @pl.when(pl.program_id(1) == 0)
    def _():
        wb_ref[...] = w_ref[...].astype(BF16)

    o_ref[...] = _dot(a_ref[...], wb_ref[...]).astype(o_ref.dtype)


def _matmul(a, w, layer, *, n_out, col_off=0, tn, tm=MM_ROW_TILE, out_dtype, name):
    m, k = a.shape
    assert m % tm == 0 and n_out % tn == 0 and col_off % tn == 0
    off = col_off // tn
    return pl.pallas_call(
        _mm_kernel,
        grid=(n_out // tn, m // tm),
        in_specs=[pl.BlockSpec((tm, k), lambda j, i: (i, 0)),
                  pl.BlockSpec((None, k, tn), lambda j, i: (layer, 0, j + off))],
        out_specs=pl.BlockSpec((tm, tn), lambda j, i: (i, j)),
        out_shape=jax.ShapeDtypeStruct((m, n_out), out_dtype),
        scratch_shapes=[pltpu.VMEM((k, tn), BF16)],
        compiler_params=_cparams("arbitrary", "arbitrary"),
        name=name,
    )(a, w)


def _swiglu_kernel(a_ref, wg_ref, wu_ref, o_ref, wgb_ref, wub_ref):
    @pl.when(pl.program_id(1) == 0)
    def _():
        wgb_ref[...] = wg_ref[...].astype(BF16)
        wub_ref[...] = wu_ref[...].astype(BF16)

    a = a_ref[...]
    gate = _dot(a, wgb_ref[...])
    up = _dot(a, wub_ref[...])
    o_ref[...] = (jax.nn.silu(gate) * up).astype(o_ref.dtype)


def _swiglu_in(a, w, layer, *, tn=512, tm=MM_ROW_TILE):
    m, k = a.shape
    f = w.shape[2] // 2
    assert m % tm == 0 and f % tn == 0
    nb = f // tn
    return pl.pallas_call(
        _swiglu_kernel,
        grid=(nb, m // tm),
        in_specs=[pl.BlockSpec((tm, k), lambda j, i: (i, 0)),
                  pl.BlockSpec((None, k, tn), lambda j, i: (layer, 0, j)),
                  pl.BlockSpec((None, k, tn), lambda j, i: (layer, 0, j + nb))],
        out_specs=pl.BlockSpec((tm, tn), lambda j, i: (i, j)),
        out_shape=jax.ShapeDtypeStruct((m, f), BF16),
        scratch_shapes=[pltpu.VMEM((k, tn), BF16), pltpu.VMEM((k, tn), BF16)],
        compiler_params=_cparams("arbitrary", "arbitrary"),
        name="ffn_in_swiglu",
    )(a, w, w)


def _ffn(u, w_in, w_out, layer):
    bsz, n, d = u.shape
    a = _swiglu_in(u.reshape(bsz * n, d), w_in, layer, tm=_row_tile(bsz * n))
    y = _matmul(a, w_out, layer, n_out=d, tn=512, tm=512, out_dtype=F32, name="ffn_out")
    return y.reshape(bsz, n, d)


def _ret_kernel(draw_ref, q_ref, k_ref, v_ref, g_ref, cos_ref, sin_ref, o_ref,
                qs_ref, ks_ref, oacc_ref, sf_ref, sb_ref, *, n_chunks, ctx_chunks, skip_chunks, dk):
    C = RET_CHUNK
    h = pl.program_id(1)
    half = dk // 2
    lg_f = -jnp.exp(jnp.full((1, 1), draw_ref[0, h], F32))
    lg_b = -jnp.exp(jnp.full((1, 1), draw_ref[1, h], F32))

    ri = lax.broadcasted_iota(jnp.int32, (C, C), 0)
    ci = lax.broadcasted_iota(jnp.int32, (C, C), 1)
    diff = (ri - ci).astype(F32)
    lower = diff >= 0
    dmat = jnp.where(lower, jnp.exp(lg_f * jnp.where(lower, diff, 0.0)),
                     jnp.exp(lg_b * jnp.where(lower, 0.0, -diff)))
    rc = lax.broadcasted_iota(jnp.int32, (C, 1), 0).astype(F32)
    qdec_f = jnp.exp(lg_f * (rc + 1.0))
    kdec_f = jnp.exp(lg_f * (C - 1.0 - rc))
    cdec_f = jnp.exp(lg_f * C)
    qdec_b = jnp.exp(lg_b * (C - rc))
    kdec_b = jnp.exp(lg_b * rc)
    cdec_b = jnp.exp(lg_b * C)

    sf_ref[...] = jnp.zeros_like(sf_ref)
    sb_ref[...] = jnp.zeros_like(sb_ref)
    qscale = dk ** -0.5

    def rope(x, cs, sn):
        x1, x2 = x[:, :half], x[:, half:]
        return jnp.concatenate([x1 * cs - x2 * sn, x1 * sn + x2 * cs], axis=-1)

    def fwd(c, carry):
        rows = pl.ds(pl.multiple_of(c * C, C), C)
        cs, sn = cos_ref[rows, :], sin_ref[rows, :]
        q = rope(q_ref[rows, :].astype(F32), cs, sn) * qscale
        k = rope(k_ref[rows, :].astype(F32), cs, sn)
        qb, kb = q.astype(BF16), k.astype(BF16)
        qs_ref[rows, :] = qb
        ks_ref[rows, :] = kb
        v = v_ref[rows, :]
        s = (_dot_nt(qb, kb) * dmat).astype(BF16)
        state = sf_ref[...]
        oacc_ref[rows, :] = _dot(s, v) + _dot(qb, state.astype(BF16)) * qdec_f
        kd_t = jnp.transpose(k * kdec_f).astype(BF16)
        sf_ref[...] = cdec_f * state + _dot(kd_t, v)
        return carry

    lax.fori_loop(0, n_chunks, fwd, 0, unroll=RET_UNROLL)

    def bwd(t, carry):
        c = jnp.where(t < ctx_chunks, ctx_chunks - 1 - t, n_chunks - 1 + ctx_chunks - t)
        rows = pl.ds(pl.multiple_of(c * C, C), C)
        qb, kb, v = qs_ref[rows, :], ks_ref[rows, :], v_ref[rows, :]
        state = sb_ref[...]
        o = oacc_ref[rows, :] + _dot(qb, state.astype(BF16)) * qdec_b
        kd_t = jnp.transpose(kb.astype(F32) * kdec_b).astype(BF16)
        sb_ref[...] = cdec_b * state + _dot(kd_t, v)

        def emit():
            mu = jnp.mean(o, axis=-1, keepdims=True)
            oc = o - mu
            var = jnp.mean(oc * oc, axis=-1, keepdims=True)
            on = oc * lax.rsqrt(var + LN_EPS)
            out_rows = pl.ds(pl.multiple_of((c - skip_chunks) * C, C), C)
            o_ref[out_rows, :] = (on * jax.nn.silu(g_ref[rows, :].astype(F32))).astype(o_ref.dtype)

        if skip_chunks:
            pl.when(c >= skip_chunks)(emit)
        else:
            emit()
        return carry

    lax.fori_loop(0, n_chunks, bwd, 0, unroll=RET_UNROLL)


def _retention(qkvg, decay_raw, cos, sin, *, n_ctx, need_ctx):
    bsz, n, width = qkvg.shape
    H = RET_HEADS
    dk = width // (6 * H)
    dv = 2 * dk
    C = RET_CHUNK
    assert n % C == 0 and n_ctx % C == 0
    skip = 0 if need_ctx else n_ctx // C
    n_out = n - skip * C
    kern = functools.partial(_ret_kernel, n_chunks=n // C, ctx_chunks=n_ctx // C, skip_chunks=skip, dk=dk)
    return pl.pallas_call(
        kern,
        grid=(bsz, H),
        in_specs=[
            pl.BlockSpec(memory_space=pltpu.SMEM),
            pl.BlockSpec((None, n, dk), lambda b, h: (b, 0, h)),
            pl.BlockSpec((None, n, dk), lambda b, h: (b, 0, H + h)),
            pl.BlockSpec((None, n, dv), lambda b, h: (b, 0, H + h)),
            pl.BlockSpec((None, n, dv), lambda b, h: (b, 0, 2 * H + h)),
            pl.BlockSpec((n, dk // 2), lambda b, h: (0, 0)),
            pl.BlockSpec((n, dk // 2), lambda b, h: (0, 0)),
        ],
        out_specs=pl.BlockSpec((None, n_out, dv), lambda b, h: (b, 0, h)),
        out_shape=jax.ShapeDtypeStruct((bsz, n_out, H * dv), BF16),
        scratch_shapes=[pltpu.VMEM((n, dk), BF16), pltpu.VMEM((n, dk), BF16), pltpu.VMEM((n, dv), F32),
                        pltpu.VMEM((dk, dv), F32), pltpu.VMEM((dk, dv), F32)],
        compiler_params=_cparams("arbitrary", "arbitrary"),
        name="retention",
    )(decay_raw.astype(F32), qkvg, qkvg, qkvg, qkvg, cos, sin)


def _rope_tables(n_ctx, n_lat, dim):
    rows = n_lat // GRID_W
    row = jnp.repeat(jnp.arange(rows, dtype=F32), GRID_W)
    col = jnp.tile(jnp.arange(GRID_W, dtype=F32), rows)
    n_freq = dim // 4
    inv_freq = ROPE_BASE ** (-jnp.arange(n_freq, dtype=F32) / n_freq)
    ang = jnp.concatenate([row[:, None] * inv_freq, col[:, None] * inv_freq], axis=-1)
    cos = jnp.concatenate([jnp.ones((n_ctx, dim // 2), F32), jnp.cos(ang)], axis=0)
    sin = jnp.concatenate([jnp.zeros((n_ctx, dim // 2), F32), jnp.sin(ang)], axis=0)
    return cos, sin


def _conv_kernel(x_ref, w_ref, o_ref, *, n_ctx, n_q_blocks, n_qk_blocks, qscale):
    j = pl.program_id(1)
    x = x_ref[...].astype(F32)
    n, tc = x.shape
    w = w_ref[...]
    pad = GDN_CONV // 2
    t = lax.broadcasted_iota(jnp.int32, (n, 1), 0)
    lo = jnp.where(t < n_ctx, 0, n_ctx)
    hi = jnp.where(t < n_ctx, n_ctx, n)
    acc = x * w[pad:pad + 1]
    for d in range(-pad, pad + 1):
        if d == 0:
            continue
        xs = pltpu.roll(x, (n - d) % n, 0)
        valid = (t + d >= lo) & (t + d < hi)
        acc = acc + jnp.where(valid, xs, 0.0) * w[d + pad:d + pad + 1]
    y = jax.nn.silu(acc)
    do_norm = j < n_qk_blocks
    scale = jnp.where(j < n_q_blocks, qscale, 1.0).astype(F32)
    for g0 in range(0, tc, GDN_DK):
        yg = y[:, g0:g0 + GDN_DK]
        ss = jnp.sum(yg * yg, axis=-1, keepdims=True)
        yn = yg * (lax.rsqrt(ss + 1e-6) * scale)
        o_ref[:, g0:g0 + GDN_DK] = jnp.where(do_norm, yn, yg).astype(o_ref.dtype)


def _gdn_conv(proj, conv_w, *, n_ctx, n_qk_heads, tc=512):
    bsz, n, _ = proj.shape
    width = conv_w.shape[1]
    qw = n_qk_heads * GDN_DK
    assert qw % tc == 0 and width % tc == 0
    kern = functools.partial(_conv_kernel, n_ctx=n_ctx, n_q_blocks=qw // tc, n_qk_blocks=2 * qw // tc,
                             qscale=GDN_DK ** -0.5)
    return pl.pallas_call(
        kern,
        grid=(bsz, width // tc),
        in_specs=[pl.BlockSpec((None, n, tc), lambda b, j: (b, 0, j)),
                  pl.BlockSpec((GDN_CONV, tc), lambda b, j: (0, j))],
        out_specs=pl.BlockSpec((None, n, tc), lambda b, j: (b, 0, j)),
        out_shape=jax.ShapeDtypeStruct((bsz, n, width), BF16),
        compiler_params=_cparams("arbitrary", "arbitrary"),
        name="gdn_conv",
    )(proj, conv_w)


def _gates_kernel(x_ref, alog_ref, dt_ref, col_ref, row_ref):
    C = GDN_CHUNK
    S = GDN_INST
    x = x_ref[...]
    n = x.shape[0]
    beta = jax.nn.sigmoid(x)
    z = x + dt_ref[...]
    softplus = jnp.maximum(z, 0.0) + jnp.log1p(jnp.exp(-jnp.abs(z)))
    la = -jnp.exp(alog_ref[...]) * softplus
    tm = lax.broadcasted_iota(jnp.int32, (n, 1), 0) % C
    pre = la
    suf = la
    s = 1
    while s < C:
        pre = pre + jnp.where(tm >= s, pltpu.roll(pre, s, 0), 0.0)
        suf = suf + jnp.where(tm < C - s, pltpu.roll(suf, n - s, 0), 0.0)
        s *= 2
    lane = lax.broadcasted_iota(jnp.int32, (1, x.shape[1]), 1) % (2 * S)
    is_fwd = (lane % S) < S // 2
    col_ref[...] = jnp.where(lane < S, beta, jnp.where(is_fwd, pre, suf))

    def xpose(p, carry):
        t = jnp.transpose(col_ref[pl.ds(pl.multiple_of(p * 2 * C, 2 * C), 2 * C), :])
        row_ref[2 * p] = t[:, :C]
        row_ref[2 * p + 1] = t[:, C:]
        return carry

    lax.fori_loop(0, n // (2 * C), xpose, 0)


def _gate_lane_order(n_qk_heads):
    idx = jnp.arange(8 * n_qk_heads).reshape(2, 2, n_qk_heads, 2)
    return idx.transpose(2, 0, 1, 3).reshape(-1)


def _gdn_gates(ba, a_log, dt_bias):
    bsz, n, w = ba.shape
    C = GDN_CHUNK
    hq = w // (2 * GDN_INST)

    def per_lane(p):
        p = p.astype(F32).reshape(2, hq, 2).transpose(1, 0, 2).reshape(hq, GDN_INST)
        return jnp.concatenate([jnp.zeros_like(p), p], axis=1).reshape(1, w)

    return pl.pallas_call(
        _gates_kernel,
        grid=(bsz,),
        in_specs=[pl.BlockSpec((None, n, w), lambda b: (b, 0, 0)),
                  pl.BlockSpec((1, w), lambda b: (0, 0)), pl.BlockSpec((1, w), lambda b: (0, 0))],
        out_specs=[pl.BlockSpec((None, n, w), lambda b: (b, 0, 0)),
                   pl.BlockSpec((None, n // C, w, C), lambda b: (b, 0, 0, 0))],
        out_shape=[jax.ShapeDtypeStruct((bsz, n, w), F32), jax.ShapeDtypeStruct((bsz, n // C, w, C), F32)],
        compiler_params=_cparams("arbitrary"),
        name="gdn_gates",
    )(ba, per_lane(a_log), per_lane(dt_bias))


GDN_INST = 4
GDN_STACK = GDN_INST * GDN_CHUNK
GDN_PREP_CHUNKS = 6
GDN_EMIT_ROWS = 256
GDN_HEADS_PER_STEP = 2


GDN_INV_BASE = 8
GDN_INV_LEVELS = (16, 32, 64)


def _merge_rows(g, lower):
    h = g // 2
    return [(r0 + h, r0 + g) if lower else (r0, r0 + h) for r0 in range(0, GDN_CHUNK, g)]


def _unit_tri_inverse(a_list, lower_list, same_base, off_masks, eye, between):
    def mm(x, y):
        return _dot(x.astype(BF16), y.astype(BF16))

    ad = [a * same_base for a in a_list]
    x = [eye - d for d in ad]
    p = [mm(d, d) for d in ad]
    between()
    xp = [mm(xi, pi) for xi, pi in zip(x, p)]
    p = [mm(pi, pi) for pi in p]
    between()
    x = [xi + xpi for xi, xpi in zip(x, xp)]
    x = [xi + mm(xi, pi) for xi, pi in zip(x, p)]
    between()
    for g, off in zip(GDN_INV_LEVELS, off_masks):
        h = g // 2
        sels = [_merge_rows(g, lower) for lower in lower_list]
        top = [jnp.concatenate([xi[a:b] for a, b in sel], axis=0) for xi, sel in zip(x, sels)]
        n = [mm(ti, a * off) for ti, a in zip(top, a_list)]
        between()
        upd = [ti - mm(ni, xi) for ti, ni, xi in zip(top, n, x)]
        between()
        merged = []
        for xi, ui, sel in zip(x, upd, sels):
            pieces = []
            for idx, (a, b) in enumerate(sel):
                new = ui[idx * h:(idx + 1) * h]
                r0 = (a // g) * g
                pieces += [xi[r0:r0 + h], new] if a != r0 else [new, xi[r0 + h:r0 + g]]
            merged.append(jnp.concatenate(pieces, axis=0))
        x = merged
    return x


def _gdn_kernel(q_ref, k_ref, v_ref, z_ref, gcol_ref, grow_ref, nw_ref, o_ref,
                mask_ref, kt_ref, gsel_ref, gq_ref, r_ref, o0_ref, of_ref, ob_ref, m_ref,
                *, n_chunks, ctx_chunks, skip_chunks):
    C = GDN_CHUNK
    S = GDN_INST
    dk = GDN_DK
    dv = GDN_DV
    n_lvl = len(GDN_INV_LEVELS)
    M_INCL, M_STRICT, M_BASE, M_OFF, M_EYE = 0, 2, 4, 5, 5 + n_lvl

    r = lax.broadcasted_iota(jnp.int32, (C, C), 0)
    c = lax.broadcasted_iota(jnp.int32, (C, C), 1)
    mask_ref[M_INCL] = (r >= c).astype(F32)
    mask_ref[M_INCL + 1] = (c >= r).astype(F32)
    mask_ref[M_STRICT] = (r > c).astype(F32)
    mask_ref[M_STRICT + 1] = (c > r).astype(F32)
    mask_ref[M_BASE] = ((r // GDN_INV_BASE) == (c // GDN_INV_BASE)).astype(F32)
    for lvl, g in enumerate(GDN_INV_LEVELS):
        h = g // 2
        mask_ref[M_OFF + lvl] = (((r // g) == (c // g)) & ((r // h) != (c // h))).astype(F32)
    mask_ref[M_EYE] = (r == c).astype(F32)

    P = GDN_HEADS_PER_STEP

    def lanes(start):
        return pl.ds(pl.multiple_of(start, dk), dk)

    gate_lanes = gcol_ref.shape[-1]

    def head_roll(hh):
        first = (pl.program_id(1) * P + hh) * 2 * S
        return (gate_lanes - first) % gate_lanes

    def head_rows(hh):
        return pl.ds(pl.multiple_of(hh * 2 * S, 2 * S), 2 * S)

    def xpose(p, carry):
        rows = pl.ds(pl.multiple_of(p * 2 * C, 2 * C), 2 * C)
        for hh in range(P):
            kt = jnp.transpose(k_ref[rows, hh * dk:(hh + 1) * dk].astype(F32)).astype(BF16)
            kt_ref[hh, 2 * p] = kt[:, :C]
            kt_ref[hh, 2 * p + 1] = kt[:, C:]
        return carry

    lax.fori_loop(0, n_chunks // 2, xpose, 0, unroll=3)

    def select_gates(i, carry):
        rows = pl.ds(pl.multiple_of(i * GDN_EMIT_ROWS, GDN_EMIT_ROWS), GDN_EMIT_ROWS)
        g = gcol_ref[rows, :]
        for hh in range(P):
            gsel_ref[hh, rows, :] = pltpu.roll(g, head_roll(hh), 1)
        return carry

    lax.fori_loop(0, n_chunks * C // GDN_EMIT_ROWS, select_gates, 0, unroll=3)

    NB = GDN_PREP_CHUNKS
    n_blocks = n_chunks // NB

    def scan_chunks(t):
        cb = jnp.where(t < ctx_chunks, ctx_chunks - 1 - t, n_chunks - 1 + ctx_chunks - t)
        return t, cb

    def prep_block(g, buf, between):
        hh, j = g // n_blocks, g % n_blocks
        items = []
        for i in range(NB):
            for d, ci in enumerate(scan_chunks(j * NB + i)):
                rows = pl.ds(pl.multiple_of(ci * C, C), C)
                qb, kb = q_ref[rows, lanes(hh * dk)], k_ref[rows, lanes(hh * dk)]
                pk = _dot_nt(jnp.concatenate([kb, qb], axis=0), kb)
                kk, qk = pk[:C], pk[C:]
                g8 = gsel_ref[hh, rows, :]
                grow8 = grow_ref[ci, head_rows(hh), :]
                qf, kf = qb.astype(F32), kb.astype(F32)
                kt = kt_ref[hh, ci].astype(F32)
                for vj in range(2):
                    s = 2 * d + vj
                    vf = v_ref[rows, lanes((2 * hh + vj) * dv)].astype(F32)
                    beta, gc = g8[:, s:s + 1], g8[:, S + s:S + s + 1]
                    g_row = grow8[S + s:S + s + 1, :]
                    e = jnp.exp(jnp.minimum(gc - g_row, 0.0))
                    a = beta * kk * (e * mask_ref[M_STRICT + d])
                    qkm = (qk * (e * mask_ref[M_INCL + d])).astype(BF16)
                    eg = jnp.exp(gc)
                    rhs = jnp.concatenate([vf * beta, kf * (beta * eg)], axis=-1)
                    g_last = g_row[:, 0:1] if d else g_row[:, C - 1:C]
                    kdt = (kt * jnp.exp(g_last - g_row)).astype(BF16)
                    items.append(dict(slot=i, s=s, a=a, qkm=qkm, rhs=rhs.astype(BF16), qd=qf * eg, kdt=kdt))
        between()

        xs = _unit_tri_inverse([it["a"] for it in items], [it["s"] < 2 for it in items], mask_ref[M_BASE],
                               [mask_ref[M_OFF + i] for i in range(n_lvl)], mask_ref[M_EYE], between)
        sols = [_dot(x.astype(BF16), it["rhs"]).astype(BF16) for x, it in zip(xs, items)]
        between()
        ows = [_dot(it["qkm"], sol) for it, sol in zip(items, sols)]
        rgs = [_dot(it["kdt"], sol) for it, sol in zip(items, sols)]
        between()
        for it, ow, rg in zip(items, ows, rgs):
            s, slot = it["s"], it["slot"]
            gq_ref[buf, s, slot, 0:dk, :] = rg[:, dv:].astype(BF16)
            gq_ref[buf, s, slot, dk:dk + C, :] = (it["qd"] - ow[:, dv:]).astype(BF16)
            r_ref[buf, s, slot] = rg[:, :dv].astype(BF16)
            o0_ref[buf, s, slot] = ow[:, :dv].astype(BF16)

    def scan_step(g, buf, i):
        hh, j = g // n_blocks, g % n_blocks
        cf, cb = scan_chunks(j * NB + i)
        cis = [cf, cf, cb, cb]
        ms = [m_ref[s] for s in range(S)]
        if i == 0:
            ms = [jnp.where(j == 0, 0.0, m) for m in ms]
        pq = [_dot(gq_ref[buf, s, i], ms[s].astype(BF16)) for s in range(S)]
        for s in range(S):
            ci = cis[s]
            g_s = grow_ref[ci, head_rows(hh), :][S + s:S + s + 1, :]
            g_last = g_s[:, C - 1:C] if s < 2 else g_s[:, 0:1]
            m_ref[s] = jnp.exp(g_last) * ms[s] - pq[s][:dk] + r_ref[buf, s, i].astype(F32)
            o = pq[s][dk:] + o0_ref[buf, s, i].astype(F32)
            rows = pl.ds(pl.multiple_of(ci * C, C), C)
            if s < 2:
                of_ref[hh, s, rows, :] = o
            else:
                ob_ref[hh, s - 2, rows, :] = o

    m_ref[...] = jnp.zeros_like(m_ref)
    prep_block(0, 0, lambda: None)

    def pipelined(g, carry):
        buf = g % 2
        pending = [functools.partial(scan_step, g, buf, i) for i in range(NB)]

        def between():
            if pending:
                pending.pop(0)()

        prep_block(g + 1, 1 - buf, between)
        while pending:
            pending.pop(0)()
        return carry

    last = P * n_blocks - 1
    lax.fori_loop(0, last, pipelined, 0)
    for i in range(NB):
        scan_step(last, last % 2, i)

    nw = nw_ref[...]

    E = GDN_EMIT_ROWS

    def emit(i, carry):
        rows = pl.ds(pl.multiple_of(i * E, E), E)
        out_rows = pl.ds(pl.multiple_of(i * E - skip_chunks * C, E), E)
        for hh in range(P):
            for vj in range(2):
                cols = slice((2 * hh + vj) * dv, (2 * hh + vj + 1) * dv)
                o = of_ref[hh, vj, rows, :] + ob_ref[hh, vj, rows, :]
                zf = z_ref[rows, cols].astype(F32)
                on = o * lax.rsqrt(jnp.mean(o * o, axis=-1, keepdims=True) + 1e-6) * nw * jax.nn.silu(zf)
                o_ref[out_rows, cols] = on.astype(o_ref.dtype)
        return carry

    lax.fori_loop(skip_chunks * C // E, n_chunks * C // E, emit, 0)


def _gdn_scan(qkv, proj, gcol, grow, norm_w, *, n_ctx, need_ctx, n_qk_heads, z_col_off):
    bsz, n, _ = qkv.shape
    C = GDN_CHUNK
    P = GDN_HEADS_PER_STEP
    HQ = n_qk_heads
    HV = 2 * HQ
    gw = gcol.shape[-1]
    assert gw == HQ * 2 * GDN_INST and HQ % P == 0 and n % (GDN_PREP_CHUNKS * C) == 0 and n_ctx % C == 0
    assert n % GDN_EMIT_ROWS == 0 and n_ctx % GDN_EMIT_ROWS == 0 and (n // C) % 2 == 0
    nch = n // C
    skip = 0 if need_ctx else n_ctx // C
    n_out = n - skip * C
    qk_w = P * GDN_DK
    v_w = P * 2 * GDN_DV
    assert z_col_off % v_w == 0
    k_blk, v_blk, z_blk = HQ * GDN_DK // qk_w, 2 * HQ * GDN_DK // v_w, z_col_off // v_w
    kern = functools.partial(_gdn_kernel, n_chunks=nch, ctx_chunks=n_ctx // C, skip_chunks=skip)
    return pl.pallas_call(
        kern,
        grid=(bsz, HQ // P),
        in_specs=[
            pl.BlockSpec((None, n, qk_w), lambda b, h: (b, 0, h)),
            pl.BlockSpec((None, n, qk_w), lambda b, h: (b, 0, k_blk + h)),
            pl.BlockSpec((None, n, v_w), lambda b, h: (b, 0, v_blk + h)),
            pl.BlockSpec((None, n, v_w), lambda b, h: (b, 0, z_blk + h)),
            pl.BlockSpec((None, n, gw), lambda b, h: (b, 0, 0)),
            pl.BlockSpec((None, nch, P * 2 * GDN_INST, C), lambda b, h: (b, 0, h, 0)),
            pl.BlockSpec((1, GDN_DV), lambda b, h: (0, 0)),
        ],
        out_specs=pl.BlockSpec((None, n_out, v_w), lambda b, h: (b, 0, h)),
        out_shape=jax.ShapeDtypeStruct((bsz, n_out, HV * GDN_DV), BF16),
        scratch_shapes=[pltpu.VMEM((6 + len(GDN_INV_LEVELS), C, C), F32),
                        pltpu.VMEM((P, nch, GDN_DK, C), BF16),
                        pltpu.VMEM((P, n, gw), F32),
                        pltpu.VMEM((2, GDN_INST, GDN_PREP_CHUNKS, GDN_DK + C, GDN_DV), BF16),
                        pltpu.VMEM((2, GDN_INST, GDN_PREP_CHUNKS, GDN_DK, GDN_DV), BF16),
                        pltpu.VMEM((2, GDN_INST, GDN_PREP_CHUNKS, C, GDN_DV), BF16),
                        pltpu.VMEM((P, 2, n, GDN_DV), F32), pltpu.VMEM((P, 2, n, GDN_DV), F32),
                        pltpu.VMEM((GDN_INST, GDN_DK, GDN_DV), F32)],
        compiler_params=_cparams("arbitrary", "arbitrary"),
        name="gdn_scan",
    )(qkv, qkv, qkv, proj, gcol, grow, norm_w.astype(F32).reshape(1, GDN_DV))


def kernel(x, c, ctx, c_ctx, ada_w, ada_b, ln_g, ln_b, ret_w_in, ret_decay, ret_w_out,
           gdn_w_in, gdn_conv, gdn_a_log, gdn_dt_bias, gdn_norm, gdn_w_out, ffn_w_in, ffn_w_out):
    bsz, n_lat, d = x.shape
    n_ctx = ctx.shape[1]
    depth = ada_w.shape[0]
    n_tot = n_ctx + n_lat
    alpha = (2 * depth) ** 0.25

    pad_rows = (-(bsz + 1)) % 16
    cc = jnp.concatenate([c, c_ctx[None], jnp.zeros((pad_rows, d), F32)], axis=0)
    mods = _ada_mod(cc, ada_w, ada_b)

    def layer_params(i):
        lat = mods[i, :bsz].reshape(bsz, 1, 6, d)
        cx = jnp.broadcast_to(mods[i, bsz].reshape(1, 1, 6, d), (bsz, 1, 6, d))
        return jnp.concatenate([cx, lat], axis=1)

    params = [layer_params(i) for i in range(depth)]
    h = (ctx, x)
    u = _modulate(h, params[0], shift=0, scale=1, n_ctx=n_ctx, n=n_tot)
    ctx_rows = n_ctx

    for i in range(depth):
        need_ctx = i < depth - 1
        j = i // 2
        u2 = u.reshape(bsz * n_tot, d)
        if i % 2 == 0:
            qkvg = _matmul(u2, ret_w_in, j, n_out=ret_w_in.shape[-1], tn=1024, tm=_row_tile(bsz * n_tot),
                           out_dtype=BF16, name="ret_in_proj").reshape(bsz, n_tot, -1)
            dk = qkvg.shape[-1] // (6 * RET_HEADS)
            cos, sin = _rope_tables(n_ctx, n_lat, dk)
            o = _retention(qkvg, ret_decay[j], cos, sin, n_ctx=n_ctx, need_ctx=need_ctx)
            w_out = ret_w_out
        else:
            qkv_w = gdn_conv.shape[-1]
            hv = gdn_a_log.shape[-1]
            main_w = qkv_w + hv * GDN_DV
            proj = _matmul(u2, gdn_w_in, j, n_out=main_w, tn=1024, tm=_row_tile(bsz * n_tot), out_dtype=BF16,
                           name="gdn_in_proj").reshape(bsz, n_tot, main_w)
            w_gate = jnp.take(gdn_w_in[j, :, main_w:], _gate_lane_order(hv // 2), axis=1)[None]
            ba = _matmul(u2, w_gate, 0, n_out=4 * hv, tn=4 * hv, out_dtype=F32,
                         name="gdn_gate_proj").reshape(bsz, n_tot, 4 * hv)
            gcol, grow = _gdn_gates(ba, gdn_a_log[j], gdn_dt_bias[j])
            qkv = _gdn_conv(proj, gdn_conv[j], n_ctx=n_ctx, n_qk_heads=hv // 2)
            o = _gdn_scan(qkv, proj, gcol, grow, gdn_norm[j], n_ctx=n_ctx, need_ctx=need_ctx,
                          n_qk_heads=hv // 2, z_col_off=qkv_w)
            w_out = gdn_w_out

        n_rows = o.shape[1]
        out_ctx = n_ctx if need_ctx else 0
        y = _matmul(o.reshape(bsz * n_rows, -1), w_out, j, n_out=d, tn=512, out_dtype=F32,
                    name="mixer_out_proj").reshape(bsz, n_rows, d)
        h_off = ctx_rows - out_ctx
        if isinstance(h, tuple) and h_off:
            h, h_off = h[1], 0
        h, u = _resid_ln(h, y, params[i], ln_g[i, 0], ln_b[i, 0], params[i], alpha=alpha, gate=2,
                         shift=3, scale=4, n_ctx=out_ctx, h_row_offset=h_off)
        ctx_rows = out_ctx
        y = _ffn(u, ffn_w_in, ffn_w_out, i)
        if need_ctx:
            h, u = _resid_ln(h, y, params[i], ln_g[i, 1], ln_b[i, 1], params[i + 1], alpha=alpha, gate=5,
                             shift=0, scale=1, n_ctx=out_ctx)
        else:
            h = _resid_ln(h, y, params[i], ln_g[i, 1], ln_b[i, 1], params[i], alpha=alpha, gate=5,
                          shift=0, scale=1, n_ctx=out_ctx, with_u=False)
    return h
```

```python
import functools
import math

import jax
import jax.numpy as jnp
from jax import lax
from jax.experimental import pallas as pl
from jax.experimental.pallas import tpu as pltpu

F32 = jnp.float32
BF16 = jnp.bfloat16

GRID_W = 64
RET_HEADS = 8
RET_CHUNK = 256
RET_UNROLL = 3
ROPE_BASE = 10000.0
GDN_DK = 128
GDN_DV = 128
GDN_CONV = 5
GDN_CHUNK = 64
LN_EPS = 1e-5

VMEM_LIMIT_BYTES = 56 * 1024 * 1024
ROW_TILE = 256
MM_ROW_TILE = 1024
MM_ROW_TILE_CAP = 1536


def _row_tile(m):
    return max(t for t in range(ROW_TILE, MM_ROW_TILE_CAP + 1, ROW_TILE) if m % t == 0)


def _cparams(*sem):
    return pltpu.CompilerParams(dimension_semantics=sem, vmem_limit_bytes=VMEM_LIMIT_BYTES)


def _dot(a, b):
    return jnp.dot(a, b, preferred_element_type=F32)


def _dot_nt(a, b):
    return lax.dot_general(a, b, (((1,), (1,)), ((), ())), preferred_element_type=F32)


def _ada_kernel(c_ref, w_ref, b_ref, o_ref):
    a = jax.nn.silu(c_ref[...]).astype(BF16)
    o_ref[...] = _dot(a, w_ref[...].astype(BF16)) + b_ref[...]


def _ada_mod(cc, ada_w, ada_b, tn=1024):
    depth, d, n = ada_w.shape
    rows = cc.shape[0]
    return pl.pallas_call(
        _ada_kernel,
        grid=(depth, n // tn),
        in_specs=[
            pl.BlockSpec((rows, d), lambda l, j: (0, 0)),
            pl.BlockSpec((None, d, tn), lambda l, j: (l, 0, j)),
            pl.BlockSpec((None, 1, tn), lambda l, j: (l, 0, j)),
        ],
        out_specs=pl.BlockSpec((None, rows, tn), lambda l, j: (l, 0, j)),
        out_shape=jax.ShapeDtypeStruct((depth, rows, n), F32),
        compiler_params=_cparams("arbitrary", "arbitrary"),
        name="ada_mod",
    )(cc, ada_w, ada_b.reshape(depth, 1, n))


def _load_h(h_refs, ctx_tiles):
    if len(h_refs) == 1:
        return h_refs[0][...]
    return jnp.where(pl.program_id(1) < ctx_tiles, h_refs[0][...], h_refs[1][...])


def _h_operands(h, d, ctx_tiles, off=0):
    if not isinstance(h, tuple):
        return [h], [pl.BlockSpec((None, ROW_TILE, d), lambda b, r: (b, r + off, 0))]
    assert off == 0
    return list(h), [
        pl.BlockSpec((None, ROW_TILE, d), lambda b, r: (b, jnp.minimum(r, ctx_tiles - 1), 0)),
        pl.BlockSpec((None, ROW_TILE, d), lambda b, r: (b, jnp.maximum(r - ctx_tiles, 0), 0))]


def _mod_kernel(*refs, shift, scale, ctx_tiles):
    *h_refs, p_ref, u_ref = refs
    p = p_ref[...]
    h = _load_h(h_refs, ctx_tiles)
    u_ref[...] = (h * (1.0 + p[scale:scale + 1]) + p[shift:shift + 1]).astype(u_ref.dtype)


def _param_spec(d, seg_of_tile):
    return pl.BlockSpec((None, None, 6, d), lambda b, r: (b, seg_of_tile(r), 0, 0))


def _modulate(h, p, *, shift, scale, n_ctx, n):
    bsz, _, _, d = p.shape
    ctx_tiles = n_ctx // ROW_TILE
    seg = lambda r: jnp.where(r < ctx_tiles, 0, 1)
    h_args, h_specs = _h_operands(h, d, ctx_tiles)
    return pl.pallas_call(
        functools.partial(_mod_kernel, shift=shift, scale=scale, ctx_tiles=ctx_tiles),
        grid=(bsz, n // ROW_TILE),
        in_specs=h_specs + [_param_spec(d, seg)],
        out_specs=pl.BlockSpec((None, ROW_TILE, d), lambda b, r: (b, r, 0)),
        out_shape=jax.ShapeDtypeStruct((bsz, n, d), BF16),
        compiler_params=_cparams("arbitrary", "arbitrary"),
        name="modulate",
    )(*h_args, p)


def _resid_ln_kernel(*refs, alpha, gate, shift, scale, ctx_tiles, n_h):
    h_refs, (y_ref, p_ref, g_ref, b_ref, p2_ref, h_out, *u_out) = refs[:n_h], refs[n_h:]
    p = p_ref[...]
    x = alpha * _load_h(h_refs, ctx_tiles) + p[gate:gate + 1] * y_ref[...].astype(F32)
    mu = jnp.mean(x, axis=-1, keepdims=True)
    xc = x - mu
    var = jnp.mean(xc * xc, axis=-1, keepdims=True)
    hn = xc * lax.rsqrt(var + LN_EPS) * g_ref[...] + b_ref[...]
    h_out[...] = hn
    if u_out:
        p2 = p2_ref[...]
        u_out[0][...] = (hn * (1.0 + p2[scale:scale + 1]) + p2[shift:shift + 1]).astype(BF16)


def _resid_ln(h, y, p, ln_g, ln_b, p2, *, alpha, gate, shift, scale, n_ctx, h_row_offset=0, with_u=True):
    bsz, n, d = y.shape
    ctx_tiles = n_ctx // ROW_TILE
    seg = lambda r: jnp.where(r < ctx_tiles, 0, 1)
    h_args, h_specs = _h_operands(h, d, ctx_tiles, h_row_offset // ROW_TILE)
    row = pl.BlockSpec((None, ROW_TILE, d), lambda b, r: (b, r, 0))
    vec = pl.BlockSpec((1, d), lambda b, r: (0, 0))
    out_shape = [jax.ShapeDtypeStruct((bsz, n, d), F32)]
    out_specs = [row]
    if with_u:
        out_shape.append(jax.ShapeDtypeStruct((bsz, n, d), BF16))
        out_specs.append(row)
    outs = pl.pallas_call(
        functools.partial(_resid_ln_kernel, alpha=alpha, gate=gate, shift=shift, scale=scale,
                          ctx_tiles=ctx_tiles, n_h=len(h_args)),
        grid=(bsz, n // ROW_TILE),
        in_specs=h_specs + [row, _param_spec(d, seg), vec, vec, _param_spec(d, seg)],
        out_specs=out_specs,
        out_shape=out_shape,
        compiler_params=_cparams("arbitrary", "arbitrary"),
        name="resid_ln",
    )(*h_args, y, p, ln_g.reshape(1, d), ln_b.reshape(1, d), p2)
    return outs if with_u else outs[0]


def _mm_kernel(a_ref, w_ref, o_ref, wb_ref):
    @pl.when(pl.program_id(1) == 0)
    def _():
        wb_ref[...] = w_ref[...].astype(BF16)

    o_ref[...] = _dot(a_ref[...], wb_ref[...]).astype(o_ref.dtype)


def _matmul(a, w, layer, *, n_out, col_off=0, tn, tm=MM_ROW_TILE, out_dtype, name):
    m, k = a.shape
    assert m % tm == 0 and n_out % tn == 0 and col_off % tn == 0
    off = col_off // tn
    return pl.pallas_call(
        _mm_kernel,
        grid=(n_out // tn, m // tm),
        in_specs=[pl.BlockSpec((tm, k), lambda j, i: (i, 0)),
                  pl.BlockSpec((None, k, tn), lambda j, i: (layer, 0, j + off))],
        out_specs=pl.BlockSpec((tm, tn), lambda j, i: (i, j)),
        out_shape=jax.ShapeDtypeStruct((m, n_out), out_dtype),
        scratch_shapes=[pltpu.VMEM((k, tn), BF16)],
        compiler_params=_cparams("arbitrary", "arbitrary"),
        name=name,
    )(a, w)


def _swiglu_kernel(a_ref, wg_ref, wu_ref, o_ref, wgb_ref, wub_ref):
    @pl.when(pl.program_id(1) == 0)
    def _():
        wgb_ref[...] = wg_ref[...].astype(BF16)
        wub_ref[...] = wu_ref[...].astype(BF16)

    a = a_ref[...]
    gate = _dot(a, wgb_ref[...])
    up = _dot(a, wub_ref[...])
    o_ref[...] = (jax.nn.silu(gate) * up).astype(o_ref.dtype)


def _swiglu_in(a, w, layer, *, tn=512, tm=MM_ROW_TILE):
    m, k = a.shape
    f = w.shape[2] // 2
    assert m % tm == 0 and f % tn == 0
    nb = f // tn
    return pl.pallas_call(
        _swiglu_kernel,
        grid=(nb, m // tm),
        in_specs=[pl.BlockSpec((tm, k), lambda j, i: (i, 0)),
                  pl.BlockSpec((None, k, tn), lambda j, i: (layer, 0, j)),
                  pl.BlockSpec((None, k, tn), lambda j, i: (layer, 0, j + nb))],
        out_specs=pl.BlockSpec((tm, tn), lambda j, i: (i, j)),
        out_shape=jax.ShapeDtypeStruct((m, f), BF16),
        scratch_shapes=[pltpu.VMEM((k, tn), BF16), pltpu.VMEM((k, tn), BF16)],
        compiler_params=_cparams("arbitrary", "arbitrary"),
        name="ffn_in_swiglu",
    )(a, w, w)


def _ffn(u, w_in, w_out, layer):
    bsz, n, d = u.shape
    a = _swiglu_in(u.reshape(bsz * n, d), w_in, layer)
    y = _matmul(a, w_out, layer, n_out=d, tn=512, tm=512, out_dtype=F32, name="ffn_out")
    return y.reshape(bsz, n, d)


def _ret_kernel(draw_ref, q_ref, k_ref, v_ref, g_ref, cos_ref, sin_ref, o_ref,
                qs_ref, ks_ref, oacc_ref, sf_ref, sb_ref, *, n_chunks, ctx_chunks, skip_chunks, dk):
    C = RET_CHUNK
    h = pl.program_id(1)
    half = dk // 2
    lg_f = -jnp.exp(jnp.full((1, 1), draw_ref[0, h], F32))
    lg_b = -jnp.exp(jnp.full((1, 1), draw_ref[1, h], F32))

    ri = lax.broadcasted_iota(jnp.int32, (C, C), 0)
    ci = lax.broadcasted_iota(jnp.int32, (C, C), 1)
    diff = (ri - ci).astype(F32)
    lower = diff >= 0
    dmat = jnp.where(lower, jnp.exp(lg_f * jnp.where(lower, diff, 0.0)),
                     jnp.exp(lg_b * jnp.where(lower, 0.0, -diff)))
    rc = lax.broadcasted_iota(jnp.int32, (C, 1), 0).astype(F32)
    qdec_f = jnp.exp(lg_f * (rc + 1.0))
    kdec_f = jnp.exp(lg_f * (C - 1.0 - rc))
    cdec_f = jnp.exp(lg_f * C)
    qdec_b = jnp.exp(lg_b * (C - rc))
    kdec_b = jnp.exp(lg_b * rc)
    cdec_b = jnp.exp(lg_b * C)

    sf_ref[...] = jnp.zeros_like(sf_ref)
    sb_ref[...] = jnp.zeros_like(sb_ref)
    qscale = dk ** -0.5

    def rope(x, cs, sn):
        x1, x2 = x[:, :half], x[:, half:]
        return jnp.concatenate([x1 * cs - x2 * sn, x1 * sn + x2 * cs], axis=-1)

    def fwd(c, carry):
        rows = pl.ds(pl.multiple_of(c * C, C), C)
        cs, sn = cos_ref[rows, :], sin_ref[rows, :]
        q = rope(q_ref[rows, :].astype(F32), cs, sn) * qscale
        k = rope(k_ref[rows, :].astype(F32), cs, sn)
        qb, kb = q.astype(BF16), k.astype(BF16)
        qs_ref[rows, :] = qb
        ks_ref[rows, :] = kb
        v = v_ref[rows, :]
        s = (_dot_nt(qb, kb) * dmat).astype(BF16)
        state = sf_ref[...]
        oacc_ref[rows, :] = _dot(s, v) + _dot(qb, state.astype(BF16)) * qdec_f
        kd_t = jnp.transpose(k * kdec_f).astype(BF16)
        sf_ref[...] = cdec_f * state + _dot(kd_t, v)
        return carry

    lax.fori_loop(0, n_chunks, fwd, 0, unroll=RET_UNROLL)

    def bwd(t, carry):
        c = jnp.where(t < ctx_chunks, ctx_chunks - 1 - t, n_chunks - 1 + ctx_chunks - t)
        rows = pl.ds(pl.multiple_of(c * C, C), C)
        qb, kb, v = qs_ref[rows, :], ks_ref[rows, :], v_ref[rows, :]
        state = sb_ref[...]
        o = oacc_ref[rows, :] + _dot(qb, state.astype(BF16)) * qdec_b
        kd_t = jnp.transpose(kb.astype(F32) * kdec_b).astype(BF16)
        sb_ref[...] = cdec_b * state + _dot(kd_t, v)

        def emit():
            mu = jnp.mean(o, axis=-1, keepdims=True)
            oc = o - mu
            var = jnp.mean(oc * oc, axis=-1, keepdims=True)
            on = oc * lax.rsqrt(var + LN_EPS)
            out_rows = pl.ds(pl.multiple_of((c - skip_chunks) * C, C), C)
            o_ref[out_rows, :] = (on * jax.nn.silu(g_ref[rows, :].astype(F32))).astype(o_ref.dtype)

        if skip_chunks:
            pl.when(c >= skip_chunks)(emit)
        else:
            emit()
        return carry

    lax.fori_loop(0, n_chunks, bwd, 0, unroll=RET_UNROLL)


def _retention(qkvg, decay_raw, cos, sin, *, n_ctx, need_ctx):
    bsz, n, width = qkvg.shape
    H = RET_HEADS
    dk = width // (6 * H)
    dv = 2 * dk
    C = RET_CHUNK
    assert n % C == 0 and n_ctx % C == 0
    skip = 0 if need_ctx else n_ctx // C
    n_out = n - skip * C
    kern = functools.partial(_ret_kernel, n_chunks=n // C, ctx_chunks=n_ctx // C, skip_chunks=skip, dk=dk)
    return pl.pallas_call(
        kern,
        grid=(bsz, H),
        in_specs=[
            pl.BlockSpec(memory_space=pltpu.SMEM),
            pl.BlockSpec((None, n, dk), lambda b, h: (b, 0, h)),
            pl.BlockSpec((None, n, dk), lambda b, h: (b, 0, H + h)),
            pl.BlockSpec((None, n, dv), lambda b, h: (b, 0, H + h)),
            pl.BlockSpec((None, n, dv), lambda b, h: (b, 0, 2 * H + h)),
            pl.BlockSpec((n, dk // 2), lambda b, h: (0, 0)),
            pl.BlockSpec((n, dk // 2), lambda b, h: (0, 0)),
        ],
        out_specs=pl.BlockSpec((None, n_out, dv), lambda b, h: (b, 0, h)),
        out_shape=jax.ShapeDtypeStruct((bsz, n_out, H * dv), BF16),
        scratch_shapes=[pltpu.VMEM((n, dk), BF16), pltpu.VMEM((n, dk), BF16), pltpu.VMEM((n, dv), F32),
                        pltpu.VMEM((dk, dv), F32), pltpu.VMEM((dk, dv), F32)],
        compiler_params=_cparams("arbitrary", "arbitrary"),
        name="retention",
    )(decay_raw.astype(F32), qkvg, qkvg, qkvg, qkvg, cos, sin)


def _rope_tables(n_ctx, n_lat, dim):
    rows = n_lat // GRID_W
    row = jnp.repeat(jnp.arange(rows, dtype=F32), GRID_W)
    col = jnp.tile(jnp.arange(GRID_W, dtype=F32), rows)
    n_freq = dim // 4
    inv_freq = ROPE_BASE ** (-jnp.arange(n_freq, dtype=F32) / n_freq)
    ang = jnp.concatenate([row[:, None] * inv_freq, col[:, None] * inv_freq], axis=-1)
    cos = jnp.concatenate([jnp.ones((n_ctx, dim // 2), F32), jnp.cos(ang)], axis=0)
    sin = jnp.concatenate([jnp.zeros((n_ctx, dim // 2), F32), jnp.sin(ang)], axis=0)
    return cos, sin


def _conv_kernel(x_ref, w_ref, o_ref, *, n_ctx, n_q_blocks, n_qk_blocks, qscale):
    j = pl.program_id(1)
    x = x_ref[...].astype(F32)
    n, tc = x.shape
    w = w_ref[...]
    pad = GDN_CONV // 2
    t = lax.broadcasted_iota(jnp.int32, (n, 1), 0)
    lo = jnp.where(t < n_ctx, 0, n_ctx)
    hi = jnp.where(t < n_ctx, n_ctx, n)
    acc = x * w[pad:pad + 1]
    for d in range(-pad, pad + 1):
        if d == 0:
            continue
        xs = pltpu.roll(x, (n - d) % n, 0)
        valid = (t + d >= lo) & (t + d < hi)
        acc = acc + jnp.where(valid, xs, 0.0) * w[d + pad:d + pad + 1]
    y = jax.nn.silu(acc)
    do_norm = j < n_qk_blocks
    scale = jnp.where(j < n_q_blocks, qscale, 1.0).astype(F32)
    for g0 in range(0, tc, GDN_DK):
        yg = y[:, g0:g0 + GDN_DK]
        ss = jnp.sum(yg * yg, axis=-1, keepdims=True)
        yn = yg * (lax.rsqrt(ss + 1e-6) * scale)
        o_ref[:, g0:g0 + GDN_DK] = jnp.where(do_norm, yn, yg).astype(o_ref.dtype)


def _gdn_conv(proj, conv_w, *, n_ctx, n_qk_heads, tc=512):
    bsz, n, _ = proj.shape
    width = conv_w.shape[1]
    qw = n_qk_heads * GDN_DK
    assert qw % tc == 0 and width % tc == 0
    kern = functools.partial(_conv_kernel, n_ctx=n_ctx, n_q_blocks=qw // tc, n_qk_blocks=2 * qw // tc,
                             qscale=GDN_DK ** -0.5)
    return pl.pallas_call(
        kern,
        grid=(bsz, width // tc),
        in_specs=[pl.BlockSpec((None, n, tc), lambda b, j: (b, 0, j)),
                  pl.BlockSpec((GDN_CONV, tc), lambda b, j: (0, j))],
        out_specs=pl.BlockSpec((None, n, tc), lambda b, j: (b, 0, j)),
        out_shape=jax.ShapeDtypeStruct((bsz, n, width), BF16),
        compiler_params=_cparams("arbitrary", "arbitrary"),
        name="gdn_conv",
    )(proj, conv_w)


def _gates_kernel(x_ref, alog_ref, dt_ref, col_ref, row_ref):
    C = GDN_CHUNK
    S = GDN_INST
    x = x_ref[...]
    n = x.shape[0]
    beta = jax.nn.sigmoid(x)
    z = x + dt_ref[...]
    softplus = jnp.maximum(z, 0.0) + jnp.log1p(jnp.exp(-jnp.abs(z)))
    la = -jnp.exp(alog_ref[...]) * softplus
    tm = lax.broadcasted_iota(jnp.int32, (n, 1), 0) % C
    pre = la
    suf = la
    s = 1
    while s < C:
        pre = pre + jnp.where(tm >= s, pltpu.roll(pre, s, 0), 0.0)
        suf = suf + jnp.where(tm < C - s, pltpu.roll(suf, n - s, 0), 0.0)
        s *= 2
    lane = lax.broadcasted_iota(jnp.int32, (1, x.shape[1]), 1) % (2 * S)
    is_fwd = (lane % S) < S // 2
    col_ref[...] = jnp.where(lane < S, beta, jnp.where(is_fwd, pre, suf))

    def xpose(p, carry):
        t = jnp.transpose(col_ref[pl.ds(pl.multiple_of(p * 2 * C, 2 * C), 2 * C), :])
        row_ref[2 * p] = t[:, :C]
        row_ref[2 * p + 1] = t[:, C:]
        return carry

    lax.fori_loop(0, n // (2 * C), xpose, 0)


def _gate_lane_order(n_qk_heads):
    idx = jnp.arange(8 * n_qk_heads).reshape(2, 2, n_qk_heads, 2)
    return idx.transpose(2, 0, 1, 3).reshape(-1)


def _gdn_gates(ba, a_log, dt_bias):
    bsz, n, w = ba.shape
    C = GDN_CHUNK
    hq = w // (2 * GDN_INST)

    def per_lane(p):
        p = p.astype(F32).reshape(2, hq, 2).transpose(1, 0, 2).reshape(hq, GDN_INST)
        return jnp.concatenate([jnp.zeros_like(p), p], axis=1).reshape(1, w)

    return pl.pallas_call(
        _gates_kernel,
        grid=(bsz,),
        in_specs=[pl.BlockSpec((None, n, w), lambda b: (b, 0, 0)),
                  pl.BlockSpec((1, w), lambda b: (0, 0)), pl.BlockSpec((1, w), lambda b: (0, 0))],
        out_specs=[pl.BlockSpec((None, n, w), lambda b: (b, 0, 0)),
                   pl.BlockSpec((None, n // C, w, C), lambda b: (b, 0, 0, 0))],
        out_shape=[jax.ShapeDtypeStruct((bsz, n, w), F32), jax.ShapeDtypeStruct((bsz, n // C, w, C), F32)],
        compiler_params=_cparams("arbitrary"),
        name="gdn_gates",
    )(ba, per_lane(a_log), per_lane(dt_bias))


GDN_INST = 4
GDN_STACK = GDN_INST * GDN_CHUNK
GDN_PREP_CHUNKS = 6
GDN_EMIT_ROWS = 256
GDN_HEADS_PER_STEP = 2


GDN_INV_BASE = 8
GDN_INV_LEVELS = (16, 32, 64)


def _merge_rows(g, lower):
    h = g // 2
    return [(r0 + h, r0 + g) if lower else (r0, r0 + h) for r0 in range(0, GDN_CHUNK, g)]


def _unit_tri_inverse(a_list, lower_list, same_base, off_masks, eye, between):
    def mm(x, y):
        return _dot(x.astype(BF16), y.astype(BF16))

    ad = [a * same_base for a in a_list]
    x = [eye - d for d in ad]
    p = [mm(d, d) for d in ad]
    between()
    xp = [mm(xi, pi) for xi, pi in zip(x, p)]
    p = [mm(pi, pi) for pi in p]
    between()
    x = [xi + xpi for xi, xpi in zip(x, xp)]
    x = [xi + mm(xi, pi) for xi, pi in zip(x, p)]
    between()
    for g, off in zip(GDN_INV_LEVELS, off_masks):
        h = g // 2
        sels = [_merge_rows(g, lower) for lower in lower_list]
        top = [jnp.concatenate([xi[a:b] for a, b in sel], axis=0) for xi, sel in zip(x, sels)]
        n = [mm(ti, a * off) for ti, a in zip(top, a_list)]
        between()
        upd = [ti - mm(ni, xi) for ti, ni, xi in zip(top, n, x)]
        between()
        merged = []
        for xi, ui, sel in zip(x, upd, sels):
            pieces = []
            for idx, (a, b) in enumerate(sel):
                new = ui[idx * h:(idx + 1) * h]
                r0 = (a // g) * g
                pieces += [xi[r0:r0 + h], new] if a != r0 else [new, xi[r0 + h:r0 + g]]
            merged.append(jnp.concatenate(pieces, axis=0))
        x = merged
    return x


def _gdn_kernel(q_ref, k_ref, v_ref, z_ref, gcol_ref, grow_ref, nw_ref, o_ref,
                mask_ref, gsel_ref, gq_ref, r_ref, o0_ref, of_ref, ob_ref, m_ref,
                *, n_chunks, ctx_chunks, skip_chunks):
    C = GDN_CHUNK
    S = GDN_INST
    dk = GDN_DK
    dv = GDN_DV
    n_lvl = len(GDN_INV_LEVELS)
    M_INCL, M_STRICT, M_BASE, M_OFF, M_EYE = 0, 2, 4, 5, 5 + n_lvl

    r = lax.broadcasted_iota(jnp.int32, (C, C), 0)
    c = lax.broadcasted_iota(jnp.int32, (C, C), 1)
    mask_ref[M_INCL] = (r >= c).astype(F32)
    mask_ref[M_INCL + 1] = (c >= r).astype(F32)
    mask_ref[M_STRICT] = (r > c).astype(F32)
    mask_ref[M_STRICT + 1] = (c > r).astype(F32)
    mask_ref[M_BASE] = ((r // GDN_INV_BASE) == (c // GDN_INV_BASE)).astype(F32)
    for lvl, g in enumerate(GDN_INV_LEVELS):
        h = g // 2
        mask_ref[M_OFF + lvl] = (((r // g) == (c // g)) & ((r // h) != (c // h))).astype(F32)
    mask_ref[M_EYE] = (r == c).astype(F32)

    P = GDN_HEADS_PER_STEP

    def lanes(start):
        return pl.ds(pl.multiple_of(start, dk), dk)

    gate_lanes = gcol_ref.shape[-1]

    def head_roll(hh):
        first = (pl.program_id(1) * P + hh) * 2 * S
        return (gate_lanes - first) % gate_lanes

    def head_rows(hh):
        return pl.ds(pl.multiple_of(hh * 2 * S, 2 * S), 2 * S)

    def select_gates(i, carry):
        rows = pl.ds(pl.multiple_of(i * GDN_EMIT_ROWS, GDN_EMIT_ROWS), GDN_EMIT_ROWS)
        g = gcol_ref[rows, :]
        for hh in range(P):
            gsel_ref[hh, rows, :] = pltpu.roll(g, head_roll(hh), 1)
        return carry

    lax.fori_loop(0, n_chunks * C // GDN_EMIT_ROWS, select_gates, 0, unroll=3)

    NB = GDN_PREP_CHUNKS
    n_blocks = n_chunks // NB

    def scan_chunks(t):
        cb = jnp.where(t < ctx_chunks, ctx_chunks - 1 - t, n_chunks - 1 + ctx_chunks - t)
        return t, cb

    def prep_block(g, buf, between):
        hh, j = g // n_blocks, g % n_blocks
        kts = {}
        for i in range(0, NB, 2):
            for d in range(2):
                lo = scan_chunks(j * NB + i + d)[d]
                slab = pl.ds(pl.multiple_of(lo * C, 2 * C), 2 * C)
                kt2 = jnp.transpose(k_ref[slab, lanes(hh * dk)].astype(F32))
                kts[(i + d, d)], kts[(i + 1 - d, d)] = kt2[:, :C], kt2[:, C:]
        items = []
        for i in range(NB):
            for d, ci in enumerate(scan_chunks(j * NB + i)):
                rows = pl.ds(pl.multiple_of(ci * C, C), C)
                qb, kb = q_ref[rows, lanes(hh * dk)], k_ref[rows, lanes(hh * dk)]
                pk = _dot_nt(jnp.concatenate([kb, qb], axis=0), kb)
                kk, qk = pk[:C], pk[C:]
                g8 = gsel_ref[hh, rows, :]
                grow8 = grow_ref[ci, head_rows(hh), :]
                qf, kf = qb.astype(F32), kb.astype(F32)
                kt = kts[(i, d)]
                for vj in range(2):
                    s = 2 * d + vj
                    vf = v_ref[rows, lanes((2 * hh + vj) * dv)].astype(F32)
                    beta, gc = g8[:, s:s + 1], g8[:, S + s:S + s + 1]
                    g_row = grow8[S + s:S + s + 1, :]
                    e = jnp.exp(jnp.minimum(gc - g_row, 0.0))
                    a = beta * kk * (e * mask_ref[M_STRICT + d])
                    qkm = (qk * (e * mask_ref[M_INCL + d])).astype(BF16)
                    eg = jnp.exp(gc)
                    rhs = jnp.concatenate([vf * beta, kf * (beta * eg)], axis=-1)
                    g_last = g_row[:, 0:1] if d else g_row[:, C - 1:C]
                    kdt = (kt * jnp.exp(g_last - g_row)).astype(BF16)
                    items.append(dict(slot=i, s=s, a=a, qkm=qkm, rhs=rhs.astype(BF16), qd=qf * eg, kdt=kdt))
        between()

        xs = _unit_tri_inverse([it["a"] for it in items], [it["s"] < 2 for it in items], mask_ref[M_BASE],
                               [mask_ref[M_OFF + i] for i in range(n_lvl)], mask_ref[M_EYE], between)
        sols = [_dot(x.astype(BF16), it["rhs"]).astype(BF16) for x, it in zip(xs, items)]
        between()
        ows = [_dot(it["qkm"], sol) for it, sol in zip(items, sols)]
        rgs = [_dot(it["kdt"], sol) for it, sol in zip(items, sols)]
        between()
        for it, ow, rg in zip(items, ows, rgs):
            s, slot = it["s"], it["slot"]
            gq_ref[buf, s, slot, 0:dk, :] = rg[:, dv:].astype(BF16)
            gq_ref[buf, s, slot, dk:dk + C, :] = (it["qd"] - ow[:, dv:]).astype(BF16)
            r_ref[buf, s, slot] = rg[:, :dv].astype(BF16)
            o0_ref[buf, s, slot] = ow[:, :dv].astype(BF16)

    def scan_step(g, buf, i):
        hh, j = g // n_blocks, g % n_blocks
        cf, cb = scan_chunks(j * NB + i)
        cis = [cf, cf, cb, cb]
        ms = [m_ref[s] for s in range(S)]
        if i == 0:
            ms = [jnp.where(j == 0, 0.0, m) for m in ms]
        pq = [_dot(gq_ref[buf, s, i], ms[s].astype(BF16)) for s in range(S)]
        for s in range(S):
            ci = cis[s]
            g_s = grow_ref[ci, head_rows(hh), :][S + s:S + s + 1, :]
            g_last = g_s[:, C - 1:C] if s < 2 else g_s[:, 0:1]
            m_ref[s] = jnp.exp(g_last) * ms[s] - pq[s][:dk] + r_ref[buf, s, i].astype(F32)
            o = pq[s][dk:] + o0_ref[buf, s, i].astype(F32)
            rows = pl.ds(pl.multiple_of(ci * C, C), C)
            if s < 2:
                of_ref[hh, s, rows, :] = o
            else:
                ob_ref[hh, s - 2, rows, :] = o

    m_ref[...] = jnp.zeros_like(m_ref)
    prep_block(0, 0, lambda: None)

    def pipelined(g, carry):
        buf = g % 2
        pending = [functools.partial(scan_step, g, buf, i) for i in range(NB)]

        def between():
            if pending:
                pending.pop(0)()

        prep_block(g + 1, 1 - buf, between)
        while pending:
            pending.pop(0)()
        return carry

    last = P * n_blocks - 1
    lax.fori_loop(0, last, pipelined, 0)
    for i in range(NB):
        scan_step(last, last % 2, i)

    nw = nw_ref[...]

    E = GDN_EMIT_ROWS

    def emit(i, carry):
        rows = pl.ds(pl.multiple_of(i * E, E), E)
        out_rows = pl.ds(pl.multiple_of(i * E - skip_chunks * C, E), E)
        for hh in range(P):
            for vj in range(2):
                cols = slice((2 * hh + vj) * dv, (2 * hh + vj + 1) * dv)
                o = of_ref[hh, vj, rows, :] + ob_ref[hh, vj, rows, :]
                zf = z_ref[rows, cols].astype(F32)
                on = o * lax.rsqrt(jnp.mean(o * o, axis=-1, keepdims=True) + 1e-6) * nw * jax.nn.silu(zf)
                o_ref[out_rows, cols] = on.astype(o_ref.dtype)
        return carry

    lax.fori_loop(skip_chunks * C // E, n_chunks * C // E, emit, 0)


def _gdn_scan(qkv, proj, gcol, grow, norm_w, *, n_ctx, need_ctx, n_qk_heads, z_col_off):
    bsz, n, _ = qkv.shape
    C = GDN_CHUNK
    P = GDN_HEADS_PER_STEP
    HQ = n_qk_heads
    HV = 2 * HQ
    gw = gcol.shape[-1]
    assert gw == HQ * 2 * GDN_INST and HQ % P == 0 and n % (GDN_PREP_CHUNKS * C) == 0 and n_ctx % C == 0
    assert n % GDN_EMIT_ROWS == 0 and n_ctx % GDN_EMIT_ROWS == 0
    assert (n // C) % 2 == 0 and (n_ctx // C) % 2 == 0 and GDN_PREP_CHUNKS % 2 == 0
    nch = n // C
    skip = 0 if need_ctx else n_ctx // C
    n_out = n - skip * C
    qk_w = P * GDN_DK
    v_w = P * 2 * GDN_DV
    assert z_col_off % v_w == 0
    k_blk, v_blk, z_blk = HQ * GDN_DK // qk_w, 2 * HQ * GDN_DK // v_w, z_col_off // v_w
    kern = functools.partial(_gdn_kernel, n_chunks=nch, ctx_chunks=n_ctx // C, skip_chunks=skip)
    return pl.pallas_call(
        kern,
        grid=(bsz, HQ // P),
        in_specs=[
            pl.BlockSpec((None, n, qk_w), lambda b, h: (b, 0, h)),
            pl.BlockSpec((None, n, qk_w), lambda b, h: (b, 0, k_blk + h)),
            pl.BlockSpec((None, n, v_w), lambda b, h: (b, 0, v_blk + h)),
            pl.BlockSpec((None, n, v_w), lambda b, h: (b, 0, z_blk + h)),
            pl.BlockSpec((None, n, gw), lambda b, h: (b, 0, 0)),
            pl.BlockSpec((None, nch, P * 2 * GDN_INST, C), lambda b, h: (b, 0, h, 0)),
            pl.BlockSpec((1, GDN_DV), lambda b, h: (0, 0)),
        ],
        out_specs=pl.BlockSpec((None, n_out, v_w), lambda b, h: (b, 0, h)),
        out_shape=jax.ShapeDtypeStruct((bsz, n_out, HV * GDN_DV), BF16),
        scratch_shapes=[pltpu.VMEM((6 + len(GDN_INV_LEVELS), C, C), F32),
                        pltpu.VMEM((P, n, gw), F32),
                        pltpu.VMEM((2, GDN_INST, GDN_PREP_CHUNKS, GDN_DK + C, GDN_DV), BF16),
                        pltpu.VMEM((2, GDN_INST, GDN_PREP_CHUNKS, GDN_DK, GDN_DV), BF16),
                        pltpu.VMEM((2, GDN_INST, GDN_PREP_CHUNKS, C, GDN_DV), BF16),
                        pltpu.VMEM((P, 2, n, GDN_DV), F32), pltpu.VMEM((P, 2, n, GDN_DV), F32),
                        pltpu.VMEM((GDN_INST, GDN_DK, GDN_DV), F32)],
        compiler_params=_cparams("arbitrary", "arbitrary"),
        name="gdn_scan",
    )(qkv, qkv, qkv, proj, gcol, grow, norm_w.astype(F32).reshape(1, GDN_DV))


def kernel(x, c, ctx, c_ctx, ada_w, ada_b, ln_g, ln_b, ret_w_in, ret_decay, ret_w_out,
           gdn_w_in, gdn_conv, gdn_a_log, gdn_dt_bias, gdn_norm, gdn_w_out, ffn_w_in, ffn_w_out):
    bsz, n_lat, d = x.shape
    n_ctx = ctx.shape[1]
    depth = ada_w.shape[0]
    n_tot = n_ctx + n_lat
    alpha = (2 * depth) ** 0.25

    pad_rows = (-(bsz + 1)) % 16
    cc = jnp.concatenate([c, c_ctx[None], jnp.zeros((pad_rows, d), F32)], axis=0)
    mods = _ada_mod(cc, ada_w, ada_b)

    def layer_params(i):
        lat = mods[i, :bsz].reshape(bsz, 1, 6, d)
        cx = jnp.broadcast_to(mods[i, bsz].reshape(1, 1, 6, d), (bsz, 1, 6, d))
        return jnp.concatenate([cx, lat], axis=1)

    params = [layer_params(i) for i in range(depth)]
    h = (ctx, x)
    u = _modulate(h, params[0], shift=0, scale=1, n_ctx=n_ctx, n=n_tot)
    ctx_rows = n_ctx

    for i in range(depth):
        need_ctx = i < depth - 1
        j = i // 2
        u2 = u.reshape(bsz * n_tot, d)
        if i % 2 == 0:
            qkvg = _matmul(u2, ret_w_in, j, n_out=ret_w_in.shape[-1], tn=1024, tm=_row_tile(bsz * n_tot),
                           out_dtype=BF16, name="ret_in_proj").reshape(bsz, n_tot, -1)
            dk = qkvg.shape[-1] // (6 * RET_HEADS)
            cos, sin = _rope_tables(n_ctx, n_lat, dk)
            o = _retention(qkvg, ret_decay[j], cos, sin, n_ctx=n_ctx, need_ctx=need_ctx)
            w_out = ret_w_out
        else:
            qkv_w = gdn_conv.shape[-1]
            hv = gdn_a_log.shape[-1]
            main_w = qkv_w + hv * GDN_DV
            proj = _matmul(u2, gdn_w_in, j, n_out=main_w, tn=1024, tm=_row_tile(bsz * n_tot), out_dtype=BF16,
                           name="gdn_in_proj").reshape(bsz, n_tot, main_w)
            w_gate = jnp.take(gdn_w_in[j, :, main_w:], _gate_lane_order(hv // 2), axis=1)[None]
            ba = _matmul(u2, w_gate, 0, n_out=4 * hv, tn=4 * hv, out_dtype=F32,
                         name="gdn_gate_proj").reshape(bsz, n_tot, 4 * hv)
            gcol, grow = _gdn_gates(ba, gdn_a_log[j], gdn_dt_bias[j])
            qkv = _gdn_conv(proj, gdn_conv[j], n_ctx=n_ctx, n_qk_heads=hv // 2)
            o = _gdn_scan(qkv, proj, gcol, grow, gdn_norm[j], n_ctx=n_ctx, need_ctx=need_ctx,
                          n_qk_heads=hv // 2, z_col_off=qkv_w)
            w_out = gdn_w_out

        n_rows = o.shape[1]
        out_ctx = n_ctx if need_ctx else 0
        y = _matmul(o.reshape(bsz * n_rows, -1), w_out, j, n_out=d, tn=512, out_dtype=F32,
                    name="mixer_out_proj").reshape(bsz, n_rows, d)
        h_off = ctx_rows - out_ctx
        if isinstance(h, tuple) and h_off:
            h, h_off = h[1], 0
        h, u = _resid_ln(h, y, params[i], ln_g[i, 0], ln_b[i, 0], params[i], alpha=alpha, gate=2,
                         shift=3, scale=4, n_ctx=out_ctx, h_row_offset=h_off)
        ctx_rows = out_ctx
        y = _ffn(u, ffn_w_in, ffn_w_out, i)
        if need_ctx:
            h, u = _resid_ln(h, y, params[i], ln_g[i, 1], ln_b[i, 1], params[i + 1], alpha=alpha, gate=5,
                             shift=0, scale=1, n_ctx=out_ctx)
        else:
            h = _resid_ln(h, y, params[i], ln_g[i, 1], ln_b[i, 1], params[i], alpha=alpha, gate=5,
                          shift=0, scale=1, n_ctx=out_ctx, with_u=False)
    return h
```

```python
import functools
import math

import jax
import jax.numpy as jnp
from jax import lax
from jax.experimental import pallas as pl
from jax.experimental.pallas import tpu as pltpu

F32 = jnp.float32
BF16 = jnp.bfloat16

GRID_W = 64
RET_HEADS = 8
RET_CHUNK = 256
RET_UNROLL = 3
ROPE_BASE = 10000.0
GDN_DK = 128
GDN_DV = 128
GDN_CONV = 5
GDN_CHUNK = 64
LN_EPS = 1e-5

VMEM_LIMIT_BYTES = 56 * 1024 * 1024
ROW_TILE = 256
MM_ROW_TILE = 1024
MM_ROW_TILE_CAP = 1536


def _row_tile(m):
    return max(t for t in range(ROW_TILE, MM_ROW_TILE_CAP + 1, ROW_TILE) if m % t == 0)


def _cparams(*sem):
    return pltpu.CompilerParams(dimension_semantics=sem, vmem_limit_bytes=VMEM_LIMIT_BYTES)


def _dot(a, b):
    return jnp.dot(a, b, preferred_element_type=F32)


def _dot_nt(a, b):
    return lax.dot_general(a, b, (((1,), (1,)), ((), ())), preferred_element_type=F32)


def _ada_kernel(c_ref, w_ref, b_ref, o_ref):
    a = jax.nn.silu(c_ref[...]).astype(BF16)
    o_ref[...] = _dot(a, w_ref[...].astype(BF16)) + b_ref[...]


def _ada_mod(cc, ada_w, ada_b, tn=1024):
    depth, d, n = ada_w.shape
    rows = cc.shape[0]
    return pl.pallas_call(
        _ada_kernel,
        grid=(depth, n // tn),
        in_specs=[
            pl.BlockSpec((rows, d), lambda l, j: (0, 0)),
            pl.BlockSpec((None, d, tn), lambda l, j: (l, 0, j)),
            pl.BlockSpec((None, 1, tn), lambda l, j: (l, 0, j)),
        ],
        out_specs=pl.BlockSpec((None, rows, tn), lambda l, j: (l, 0, j)),
        out_shape=jax.ShapeDtypeStruct((depth, rows, n), F32),
        compiler_params=_cparams("arbitrary", "arbitrary"),
        name="ada_mod",
    )(cc, ada_w, ada_b.reshape(depth, 1, n))


def _load_h(h_refs, ctx_tiles):
    if len(h_refs) == 1:
        return h_refs[0][...]
    return jnp.where(pl.program_id(1) < ctx_tiles, h_refs[0][...], h_refs[1][...])


def _h_operands(h, d, ctx_tiles, off=0):
    if not isinstance(h, tuple):
        return [h], [pl.BlockSpec((None, ROW_TILE, d), lambda b, r: (b, r + off, 0))]
    assert off == 0
    return list(h), [
        pl.BlockSpec((None, ROW_TILE, d), lambda b, r: (b, jnp.minimum(r, ctx_tiles - 1), 0)),
        pl.BlockSpec((None, ROW_TILE, d), lambda b, r: (b, jnp.maximum(r - ctx_tiles, 0), 0))]


def _mod_kernel(*refs, shift, scale, ctx_tiles):
    *h_refs, p_ref, u_ref = refs
    p = p_ref[...]
    h = _load_h(h_refs, ctx_tiles)
    u_ref[...] = (h * (1.0 + p[scale:scale + 1]) + p[shift:shift + 1]).astype(u_ref.dtype)


def _param_spec(d, seg_of_tile):
    return pl.BlockSpec((None, None, 6, d), lambda b, r: (b, seg_of_tile(r), 0, 0))


def _modulate(h, p, *, shift, scale, n_ctx, n):
    bsz, _, _, d = p.shape
    ctx_tiles = n_ctx // ROW_TILE
    seg = lambda r: jnp.where(r < ctx_tiles, 0, 1)
    h_args, h_specs = _h_operands(h, d, ctx_tiles)
    return pl.pallas_call(
        functools.partial(_mod_kernel, shift=shift, scale=scale, ctx_tiles=ctx_tiles),
        grid=(bsz, n // ROW_TILE),
        in_specs=h_specs + [_param_spec(d, seg)],
        out_specs=pl.BlockSpec((None, ROW_TILE, d), lambda b, r: (b, r, 0)),
        out_shape=jax.ShapeDtypeStruct((bsz, n, d), BF16),
        compiler_params=_cparams("arbitrary", "arbitrary"),
        name="modulate",
    )(*h_args, p)


def _resid_ln_kernel(*refs, alpha, gate, shift, scale, ctx_tiles, n_h):
    h_refs, (y_ref, p_ref, g_ref, b_ref, p2_ref, h_out, *u_out) = refs[:n_h], refs[n_h:]
    p = p_ref[...]
    x = alpha * _load_h(h_refs, ctx_tiles) + p[gate:gate + 1] * y_ref[...].astype(F32)
    mu = jnp.mean(x, axis=-1, keepdims=True)
    xc = x - mu
    var = jnp.mean(xc * xc, axis=-1, keepdims=True)
    hn = xc * lax.rsqrt(var + LN_EPS) * g_ref[...] + b_ref[...]
    h_out[...] = hn
    if u_out:
        p2 = p2_ref[...]
        u_out[0][...] = (hn * (1.0 + p2[scale:scale + 1]) + p2[shift:shift + 1]).astype(BF16)


def _resid_ln(h, y, p, ln_g, ln_b, p2, *, alpha, gate, shift, scale, n_ctx, h_row_offset=0, with_u=True):
    bsz, n, d = y.shape
    ctx_tiles = n_ctx // ROW_TILE
    seg = lambda r: jnp.where(r < ctx_tiles, 0, 1)
    h_args, h_specs = _h_operands(h, d, ctx_tiles, h_row_offset // ROW_TILE)
    row = pl.BlockSpec((None, ROW_TILE, d), lambda b, r: (b, r, 0))
    vec = pl.BlockSpec((1, d), lambda b, r: (0, 0))
    out_shape = [jax.ShapeDtypeStruct((bsz, n, d), F32)]
    out_specs = [row]
    if with_u:
        out_shape.append(jax.ShapeDtypeStruct((bsz, n, d), BF16))
        out_specs.append(row)
    outs = pl.pallas_call(
        functools.partial(_resid_ln_kernel, alpha=alpha, gate=gate, shift=shift, scale=scale,
                          ctx_tiles=ctx_tiles, n_h=len(h_args)),
        grid=(bsz, n // ROW_TILE),
        in_specs=h_specs + [row, _param_spec(d, seg), vec, vec, _param_spec(d, seg)],
        out_specs=out_specs,
        out_shape=out_shape,
        compiler_params=_cparams("arbitrary", "arbitrary"),
        name="resid_ln",
    )(*h_args, y, p, ln_g.reshape(1, d), ln_b.reshape(1, d), p2)
    return outs if with_u else outs[0]


def _mm_kernel(a_ref, w_ref, o_ref, wb_ref):
    @pl.when(pl.program_id(1) == 0)
    def _():
        wb_ref[...] = w_ref[...].astype(BF16)

    o_ref[...] = _dot(a_ref[...], wb_ref[...]).astype(o_ref.dtype)


def _matmul(a, w, layer, *, n_out, col_off=0, tn, tm=MM_ROW_TILE, out_dtype, name):
    m, k = a.shape
    assert m % tm == 0 and n_out % tn == 0 and col_off % tn == 0
    off = col_off // tn
    return pl.pallas_call(
        _mm_kernel,
        grid=(n_out // tn, m // tm),
        in_specs=[pl.BlockSpec((tm, k), lambda j, i: (i, 0)),
                  pl.BlockSpec((None, k, tn), lambda j, i: (layer, 0, j + off))],
        out_specs=pl.BlockSpec((tm, tn), lambda j, i: (i, j)),
        out_shape=jax.ShapeDtypeStruct((m, n_out), out_dtype),
        scratch_shapes=[pltpu.VMEM((k, tn), BF16)],
        compiler_params=_cparams("arbitrary", "arbitrary"),
        name=name,
    )(a, w)


def _swiglu_kernel(a_ref, wg_ref, wu_ref, o_ref, wgb_ref, wub_ref):
    @pl.when(pl.program_id(1) == 0)
    def _():
        wgb_ref[...] = wg_ref[...].astype(BF16)
        wub_ref[...] = wu_ref[...].astype(BF16)

    a = a_ref[...]
    gate = _dot(a, wgb_ref[...])
    up = _dot(a, wub_ref[...])
    o_ref[...] = (jax.nn.silu(gate) * up).astype(o_ref.dtype)


def _swiglu_in(a, w, layer, *, tn=512, tm=MM_ROW_TILE):
    m, k = a.shape
    f = w.shape[2] // 2
    assert m % tm == 0 and f % tn == 0
    nb = f // tn
    return pl.pallas_call(
        _swiglu_kernel,
        grid=(nb, m // tm),
        in_specs=[pl.BlockSpec((tm, k), lambda j, i: (i, 0)),
                  pl.BlockSpec((None, k, tn), lambda j, i: (layer, 0, j)),
                  pl.BlockSpec((None, k, tn), lambda j, i: (layer, 0, j + nb))],
        out_specs=pl.BlockSpec((tm, tn), lambda j, i: (i, j)),
        out_shape=jax.ShapeDtypeStruct((m, f), BF16),
        scratch_shapes=[pltpu.VMEM((k, tn), BF16), pltpu.VMEM((k, tn), BF16)],
        compiler_params=_cparams("arbitrary", "arbitrary"),
        name="ffn_in_swiglu",
    )(a, w, w)


def _ffn(u, w_in, w_out, layer):
    bsz, n, d = u.shape
    a = _swiglu_in(u.reshape(bsz * n, d), w_in, layer)
    y = _matmul(a, w_out, layer, n_out=d, tn=512, tm=512, out_dtype=F32, name="ffn_out")
    return y.reshape(bsz, n, d)


def _ret_kernel(draw_ref, q_ref, k_ref, v_ref, g_ref, cos_ref, sin_ref, o_ref,
                qs_ref, ks_ref, oacc_ref, sf_ref, sb_ref, *, n_chunks, ctx_chunks, skip_chunks, dk):
    C = RET_CHUNK
    h = pl.program_id(1)
    half = dk // 2
    lg_f = -jnp.exp(jnp.full((1, 1), draw_ref[0, h], F32))
    lg_b = -jnp.exp(jnp.full((1, 1), draw_ref[1, h], F32))

    ri = lax.broadcasted_iota(jnp.int32, (C, C), 0)
    ci = lax.broadcasted_iota(jnp.int32, (C, C), 1)
    diff = (ri - ci).astype(F32)
    lower = diff >= 0
    dmat = jnp.where(lower, jnp.exp(lg_f * jnp.where(lower, diff, 0.0)),
                     jnp.exp(lg_b * jnp.where(lower, 0.0, -diff)))
    rc = lax.broadcasted_iota(jnp.int32, (C, 1), 0).astype(F32)
    qdec_f = jnp.exp(lg_f * (rc + 1.0))
    kdec_f = jnp.exp(lg_f * (C - 1.0 - rc))
    cdec_f = jnp.exp(lg_f * C)
    qdec_b = jnp.exp(lg_b * (C - rc))
    kdec_b = jnp.exp(lg_b * rc)
    cdec_b = jnp.exp(lg_b * C)

    sf_ref[...] = jnp.zeros_like(sf_ref)
    sb_ref[...] = jnp.zeros_like(sb_ref)
    qscale = dk ** -0.5

    def rope(x, cs, sn):
        x1, x2 = x[:, :half], x[:, half:]
        return jnp.concatenate([x1 * cs - x2 * sn, x1 * sn + x2 * cs], axis=-1)

    def fwd(c, carry):
        rows = pl.ds(pl.multiple_of(c * C, C), C)
        cs, sn = cos_ref[rows, :], sin_ref[rows, :]
        q = rope(q_ref[rows, :].astype(F32), cs, sn) * qscale
        k = rope(k_ref[rows, :].astype(F32), cs, sn)
        qb, kb = q.astype(BF16), k.astype(BF16)
        qs_ref[rows, :] = qb
        ks_ref[rows, :] = kb
        v = v_ref[rows, :]
        s = (_dot_nt(qb, kb) * dmat).astype(BF16)
        state = sf_ref[...]
        oacc_ref[rows, :] = _dot(s, v) + _dot(qb, state.astype(BF16)) * qdec_f
        kd_t = jnp.transpose(k * kdec_f).astype(BF16)
        sf_ref[...] = cdec_f * state + _dot(kd_t, v)
        return carry

    lax.fori_loop(0, n_chunks, fwd, 0, unroll=RET_UNROLL)

    def bwd(t, carry):
        c = jnp.where(t < ctx_chunks, ctx_chunks - 1 - t, n_chunks - 1 + ctx_chunks - t)
        rows = pl.ds(pl.multiple_of(c * C, C), C)
        qb, kb, v = qs_ref[rows, :], ks_ref[rows, :], v_ref[rows, :]
        state = sb_ref[...]
        o = oacc_ref[rows, :] + _dot(qb, state.astype(BF16)) * qdec_b
        kd_t = jnp.transpose(kb.astype(F32) * kdec_b).astype(BF16)
        sb_ref[...] = cdec_b * state + _dot(kd_t, v)

        def emit():
            mu = jnp.mean(o, axis=-1, keepdims=True)
            oc = o - mu
            var = jnp.mean(oc * oc, axis=-1, keepdims=True)
            on = oc * lax.rsqrt(var + LN_EPS)
            out_rows = pl.ds(pl.multiple_of((c - skip_chunks) * C, C), C)
            o_ref[out_rows, :] = (on * jax.nn.silu(g_ref[rows, :].astype(F32))).astype(o_ref.dtype)

        if skip_chunks:
            pl.when(c >= skip_chunks)(emit)
        else:
            emit()
        return carry

    lax.fori_loop(0, n_chunks, bwd, 0, unroll=RET_UNROLL)


def _retention(qkvg, decay_raw, cos, sin, *, n_ctx, need_ctx):
    bsz, n, width = qkvg.shape
    H = RET_HEADS
    dk = width // (6 * H)
    dv = 2 * dk
    C = RET_CHUNK
    assert n % C == 0 and n_ctx % C == 0
    skip = 0 if need_ctx else n_ctx // C
    n_out = n - skip * C
    kern = functools.partial(_ret_kernel, n_chunks=n // C, ctx_chunks=n_ctx // C, skip_chunks=skip, dk=dk)
    return pl.pallas_call(
        kern,
        grid=(bsz, H),
        in_specs=[
            pl.BlockSpec(memory_space=pltpu.SMEM),
            pl.BlockSpec((None, n, dk), lambda b, h: (b, 0, h)),
            pl.BlockSpec((None, n, dk), lambda b, h: (b, 0, H + h)),
            pl.BlockSpec((None, n, dv), lambda b, h: (b, 0, H + h)),
            pl.BlockSpec((None, n, dv), lambda b, h: (b, 0, 2 * H + h)),
            pl.BlockSpec((n, dk // 2), lambda b, h: (0, 0)),
            pl.BlockSpec((n, dk // 2), lambda b, h: (0, 0)),
        ],
        out_specs=pl.BlockSpec((None, n_out, dv), lambda b, h: (b, 0, h)),
        out_shape=jax.ShapeDtypeStruct((bsz, n_out, H * dv), BF16),
        scratch_shapes=[pltpu.VMEM((n, dk), BF16), pltpu.VMEM((n, dk), BF16), pltpu.VMEM((n, dv), F32),
                        pltpu.VMEM((dk, dv), F32), pltpu.VMEM((dk, dv), F32)],
        compiler_params=_cparams("arbitrary", "arbitrary"),
        name="retention",
    )(decay_raw.astype(F32), qkvg, qkvg, qkvg, qkvg, cos, sin)


def _rope_tables(n_ctx, n_lat, dim):
    rows = n_lat // GRID_W
    row = jnp.repeat(jnp.arange(rows, dtype=F32), GRID_W)
    col = jnp.tile(jnp.arange(GRID_W, dtype=F32), rows)
    n_freq = dim // 4
    inv_freq = ROPE_BASE ** (-jnp.arange(n_freq, dtype=F32) / n_freq)
    ang = jnp.concatenate([row[:, None] * inv_freq, col[:, None] * inv_freq], axis=-1)
    cos = jnp.concatenate([jnp.ones((n_ctx, dim // 2), F32), jnp.cos(ang)], axis=0)
    sin = jnp.concatenate([jnp.zeros((n_ctx, dim // 2), F32), jnp.sin(ang)], axis=0)
    return cos, sin


def _conv_kernel(x_ref, w_ref, o_ref, *, n_ctx, n_q_blocks, n_qk_blocks, qscale):
    j = pl.program_id(1)
    x = x_ref[...].astype(F32)
    n, tc = x.shape
    w = w_ref[...]
    pad = GDN_CONV // 2
    t = lax.broadcasted_iota(jnp.int32, (n, 1), 0)
    lo = jnp.where(t < n_ctx, 0, n_ctx)
    hi = jnp.where(t < n_ctx, n_ctx, n)
    acc = x * w[pad:pad + 1]
    for d in range(-pad, pad + 1):
        if d == 0:
            continue
        xs = pltpu.roll(x, (n - d) % n, 0)
        valid = (t + d >= lo) & (t + d < hi)
        acc = acc + jnp.where(valid, xs, 0.0) * w[d + pad:d + pad + 1]
    y = jax.nn.silu(acc)
    do_norm = j < n_qk_blocks
    scale = jnp.where(j < n_q_blocks, qscale, 1.0).astype(F32)
    for g0 in range(0, tc, GDN_DK):
        yg = y[:, g0:g0 + GDN_DK]
        ss = jnp.sum(yg * yg, axis=-1, keepdims=True)
        yn = yg * (lax.rsqrt(ss + 1e-6) * scale)
        o_ref[:, g0:g0 + GDN_DK] = jnp.where(do_norm, yn, yg).astype(o_ref.dtype)


def _gdn_conv(proj, conv_w, *, n_ctx, n_qk_heads, tc=512):
    bsz, n, _ = proj.shape
    width = conv_w.shape[1]
    qw = n_qk_heads * GDN_DK
    assert qw % tc == 0 and width % tc == 0
    kern = functools.partial(_conv_kernel, n_ctx=n_ctx, n_q_blocks=qw // tc, n_qk_blocks=2 * qw // tc,
                             qscale=GDN_DK ** -0.5)
    return pl.pallas_call(
        kern,
        grid=(bsz, width // tc),
        in_specs=[pl.BlockSpec((None, n, tc), lambda b, j: (b, 0, j)),
                  pl.BlockSpec((GDN_CONV, tc), lambda b, j: (0, j))],
        out_specs=pl.BlockSpec((None, n, tc), lambda b, j: (b, 0, j)),
        out_shape=jax.ShapeDtypeStruct((bsz, n, width), BF16),
        compiler_params=_cparams("arbitrary", "arbitrary"),
        name="gdn_conv",
    )(proj, conv_w)


def _gates_kernel(x_ref, alog_ref, dt_ref, col_ref, row_ref):
    C = GDN_CHUNK
    S = GDN_INST
    x = x_ref[...]
    n = x.shape[0]
    beta = jax.nn.sigmoid(x)
    z = x + dt_ref[...]
    softplus = jnp.maximum(z, 0.0) + jnp.log1p(jnp.exp(-jnp.abs(z)))
    la = -jnp.exp(alog_ref[...]) * softplus
    tm = lax.broadcasted_iota(jnp.int32, (n, 1), 0) % C
    pre = la
    suf = la
    s = 1
    while s < C:
        pre = pre + jnp.where(tm >= s, pltpu.roll(pre, s, 0), 0.0)
        suf = suf + jnp.where(tm < C - s, pltpu.roll(suf, n - s, 0), 0.0)
        s *= 2
    lane = lax.broadcasted_iota(jnp.int32, (1, x.shape[1]), 1) % (2 * S)
    is_fwd = (lane % S) < S // 2
    col_ref[...] = jnp.where(lane < S, beta, jnp.where(is_fwd, pre, suf))

    def xpose(p, carry):
        t = jnp.transpose(col_ref[pl.ds(pl.multiple_of(p * 2 * C, 2 * C), 2 * C), :])
        row_ref[2 * p] = t[:, :C]
        row_ref[2 * p + 1] = t[:, C:]
        return carry

    lax.fori_loop(0, n // (2 * C), xpose, 0)


def _gate_lane_order(n_qk_heads):
    idx = jnp.arange(8 * n_qk_heads).reshape(2, 2, n_qk_heads, 2)
    return idx.transpose(2, 0, 1, 3).reshape(-1)


def _gdn_gates(ba, a_log, dt_bias):
    bsz, n, w = ba.shape
    C = GDN_CHUNK
    hq = w // (2 * GDN_INST)

    def per_lane(p):
        p = p.astype(F32).reshape(2, hq, 2).transpose(1, 0, 2).reshape(hq, GDN_INST)
        return jnp.concatenate([jnp.zeros_like(p), p], axis=1).reshape(1, w)

    return pl.pallas_call(
        _gates_kernel,
        grid=(bsz,),
        in_specs=[pl.BlockSpec((None, n, w), lambda b: (b, 0, 0)),
                  pl.BlockSpec((1, w), lambda b: (0, 0)), pl.BlockSpec((1, w), lambda b: (0, 0))],
        out_specs=[pl.BlockSpec((None, n, w), lambda b: (b, 0, 0)),
                   pl.BlockSpec((None, n // C, w, C), lambda b: (b, 0, 0, 0))],
        out_shape=[jax.ShapeDtypeStruct((bsz, n, w), F32), jax.ShapeDtypeStruct((bsz, n // C, w, C), F32)],
        compiler_params=_cparams("arbitrary"),
        name="gdn_gates",
    )(ba, per_lane(a_log), per_lane(dt_bias))


GDN_INST = 4
GDN_STACK = GDN_INST * GDN_CHUNK
GDN_PREP_CHUNKS = 6
GDN_EMIT_ROWS = 256
GDN_HEADS_PER_STEP = 2


GDN_INV_BASE = 8
GDN_INV_LEVELS = (16, 32, 64)


def _merge_rows(g, lower):
    h = g // 2
    return [(r0 + h, r0 + g) if lower else (r0, r0 + h) for r0 in range(0, GDN_CHUNK, g)]


def _unit_tri_inverse(a_list, lower_list, same_base, off_masks, eye, between):
    def mm(x, y):
        return _dot(x.astype(BF16), y.astype(BF16))

    ad = [a * same_base for a in a_list]
    x = [eye - d for d in ad]
    p = [mm(d, d) for d in ad]
    between()
    xp = [mm(xi, pi) for xi, pi in zip(x, p)]
    p = [mm(pi, pi) for pi in p]
    between()
    x = [xi + xpi for xi, xpi in zip(x, xp)]
    x = [xi + mm(xi, pi) for xi, pi in zip(x, p)]
    between()
    for g, off in zip(GDN_INV_LEVELS, off_masks):
        h = g // 2
        sels = [_merge_rows(g, lower) for lower in lower_list]
        top = [jnp.concatenate([xi[a:b] for a, b in sel], axis=0) for xi, sel in zip(x, sels)]
        n = [mm(ti, a * off) for ti, a in zip(top, a_list)]
        between()
        upd = [ti - mm(ni, xi) for ti, ni, xi in zip(top, n, x)]
        between()
        merged = []
        for xi, ui, sel in zip(x, upd, sels):
            pieces = []
            for idx, (a, b) in enumerate(sel):
                new = ui[idx * h:(idx + 1) * h]
                r0 = (a // g) * g
                pieces += [xi[r0:r0 + h], new] if a != r0 else [new, xi[r0 + h:r0 + g]]
            merged.append(jnp.concatenate(pieces, axis=0))
        x = merged
    return x


def _gdn_kernel(q_ref, k_ref, v_ref, z_ref, gcol_ref, grow_ref, nw_ref, o_ref,
                mask_ref, gsel_ref, u0_ref, wq_ref, qk_ref, of_ref, ob_ref, m_ref,
                *, n_chunks, ctx_chunks, skip_chunks):
    C = GDN_CHUNK
    S = GDN_INST
    dk = GDN_DK
    dv = GDN_DV
    n_lvl = len(GDN_INV_LEVELS)
    M_INCL, M_STRICT, M_BASE, M_OFF, M_EYE = 0, 2, 4, 5, 5 + n_lvl

    r = lax.broadcasted_iota(jnp.int32, (C, C), 0)
    c = lax.broadcasted_iota(jnp.int32, (C, C), 1)
    mask_ref[M_INCL] = (r >= c).astype(F32)
    mask_ref[M_INCL + 1] = (c >= r).astype(F32)
    mask_ref[M_STRICT] = (r > c).astype(F32)
    mask_ref[M_STRICT + 1] = (c > r).astype(F32)
    mask_ref[M_BASE] = ((r // GDN_INV_BASE) == (c // GDN_INV_BASE)).astype(F32)
    for lvl, g in enumerate(GDN_INV_LEVELS):
        h = g // 2
        mask_ref[M_OFF + lvl] = (((r // g) == (c // g)) & ((r // h) != (c // h))).astype(F32)
    mask_ref[M_EYE] = (r == c).astype(F32)

    P = GDN_HEADS_PER_STEP

    def lanes(start):
        return pl.ds(pl.multiple_of(start, dk), dk)

    gate_lanes = gcol_ref.shape[-1]

    def head_roll(hh):
        first = (pl.program_id(1) * P + hh) * 2 * S
        return (gate_lanes - first) % gate_lanes

    def head_rows(hh):
        return pl.ds(pl.multiple_of(hh * 2 * S, 2 * S), 2 * S)

    def select_gates(i, carry):
        rows = pl.ds(pl.multiple_of(i * GDN_EMIT_ROWS, GDN_EMIT_ROWS), GDN_EMIT_ROWS)
        g = gcol_ref[rows, :]
        for hh in range(P):
            gsel_ref[hh, rows, :] = pltpu.roll(g, head_roll(hh), 1)
        return carry

    lax.fori_loop(0, n_chunks * C // GDN_EMIT_ROWS, select_gates, 0, unroll=3)

    NB = GDN_PREP_CHUNKS
    n_blocks = n_chunks // NB

    def scan_chunks(t):
        cb = jnp.where(t < ctx_chunks, ctx_chunks - 1 - t, n_chunks - 1 + ctx_chunks - t)
        return t, cb

    def prep_block(g, buf, between):
        hh, j = g // n_blocks, g % n_blocks
        between()
        kts = {}
        for i in range(0, NB, 2):
            for d in range(2):
                lo = scan_chunks(j * NB + i + d)[d]
                slab = pl.ds(pl.multiple_of(lo * C, 2 * C), 2 * C)
                kt2 = jnp.transpose(k_ref[slab, lanes(hh * dk)].astype(F32))
                kts[(i + d, d)], kts[(i + 1 - d, d)] = kt2[:, :C], kt2[:, C:]
        items = []
        for i in range(NB):
            for d, ci in enumerate(scan_chunks(j * NB + i)):
                rows = pl.ds(pl.multiple_of(ci * C, C), C)
                qb, kb = q_ref[rows, lanes(hh * dk)], k_ref[rows, lanes(hh * dk)]
                pk = _dot_nt(jnp.concatenate([kb, qb], axis=0), kb)
                kk, qk = pk[:C], pk[C:]
                g8 = gsel_ref[hh, rows, :]
                grow8 = grow_ref[ci, head_rows(hh), :]
                qf, kf = qb.astype(F32), kb.astype(F32)
                kt = kts[(i, d)]
                for vj in range(2):
                    s = 2 * d + vj
                    vf = v_ref[rows, lanes((2 * hh + vj) * dv)].astype(F32)
                    beta, gc = g8[:, s:s + 1], g8[:, S + s:S + s + 1]
                    g_row = grow8[S + s:S + s + 1, :]
                    e = jnp.exp(jnp.minimum(gc - g_row, 0.0))
                    a = beta * kk * (e * mask_ref[M_STRICT + d])
                    qkm = (qk * (e * mask_ref[M_INCL + d])).astype(BF16)
                    eg = jnp.exp(gc)
                    rhs = jnp.concatenate([vf * beta, kf * (beta * eg)], axis=-1)
                    g_last = g_row[:, 0:1] if d else g_row[:, C - 1:C]
                    kdt = (kt * jnp.exp(g_last - g_row)).astype(BF16)
                    items.append(dict(slot=i, s=s, a=a, qkm=qkm, rhs=rhs.astype(BF16), qd=qf * eg, kdt=kdt))
        between()

        xs = _unit_tri_inverse([it["a"] for it in items], [it["s"] < 2 for it in items], mask_ref[M_BASE],
                               [mask_ref[M_OFF + i] for i in range(n_lvl)], mask_ref[M_EYE], between)
        sols = [_dot(x.astype(BF16), it["rhs"]) for x, it in zip(xs, items)]
        between()
        for it, sol in zip(items, sols):
            s, slot = it["s"], it["slot"]
            u0_ref[buf, s, slot] = sol[:, :dv]
            wq_ref[buf, s, slot, 0:C, :] = sol[:, dv:].astype(BF16)
            wq_ref[buf, s, slot, C:2 * C, :] = it["qd"].astype(BF16)
            qk_ref[buf, s, slot, 0:C, :] = it["qkm"]
            qk_ref[buf, s, slot, C:C + dk, :] = it["kdt"]

    def scan_first(g, buf, i, st):
        j = g % n_blocks
        ms = [m_ref[s] for s in range(S)]
        if i == 0:
            ms = [jnp.where(j == 0, 0.0, m) for m in ms]
        st["ms"] = ms
        st["wm"] = [_dot(wq_ref[buf, s, i], ms[s].astype(BF16)) for s in range(S)]

    def scan_second(g, buf, i, st):
        hh, j = g // n_blocks, g % n_blocks
        cf, cb = scan_chunks(j * NB + i)
        cis = [cf, cf, cb, cb]
        ms, wm = st["ms"], st["wm"]
        ub = [(u0_ref[buf, s, i] - wm[s][:C]).astype(BF16) for s in range(S)]
        om = [_dot(qk_ref[buf, s, i], ub[s]) for s in range(S)]
        for s in range(S):
            ci = cis[s]
            g_s = grow_ref[ci, head_rows(hh), :][S + s:S + s + 1, :]
            g_last = g_s[:, C - 1:C] if s < 2 else g_s[:, 0:1]
            m_ref[s] = jnp.exp(g_last) * ms[s] + om[s][C:]
            o = wm[s][C:] + om[s][:C]
            rows = pl.ds(pl.multiple_of(ci * C, C), C)
            if s < 2:
                of_ref[hh, s, rows, :] = o
            else:
                ob_ref[hh, s - 2, rows, :] = o

    def scan_thunks(g, buf):
        thunks = []
        for i in range(NB):
            st = {}
            thunks += [functools.partial(scan_first, g, buf, i, st), functools.partial(scan_second, g, buf, i, st)]
        return thunks

    m_ref[...] = jnp.zeros_like(m_ref)
    prep_block(0, 0, lambda: None)

    def pipelined(g, carry):
        buf = g % 2
        pending = scan_thunks(g, buf)

        def between():
            if pending:
                pending.pop(0)()

        prep_block(g + 1, 1 - buf, between)
        while pending:
            pending.pop(0)()
        return carry

    nw = nw_ref[...]
    E = GDN_EMIT_ROWS
    emit_blocks = range(skip_chunks * C // E, n_chunks * C // E)

    def emit(hh, i):
        aligned = (lambda s: s) if isinstance(i, int) else (lambda s: pl.multiple_of(s, E))
        rows = pl.ds(aligned(i * E), E)
        out_rows = pl.ds(aligned(i * E - skip_chunks * C), E)
        for vj in range(2):
            cols = slice((2 * hh + vj) * dv, (2 * hh + vj + 1) * dv)
            o = of_ref[hh, vj, rows, :] + ob_ref[hh, vj, rows, :]
            zf = z_ref[rows, cols].astype(F32)
            on = o * lax.rsqrt(jnp.mean(o * o, axis=-1, keepdims=True) + 1e-6) * nw * jax.nn.silu(zf)
            o_ref[out_rows, cols] = on.astype(o_ref.dtype)

    last = P * n_blocks - 1
    lax.fori_loop(0, last, pipelined, 0)

    done = [functools.partial(emit, hh, i) for hh in range(P - 1) for i in emit_blocks]
    tail = scan_thunks(last, last % 2)
    for k, thunk in enumerate(tail):
        thunk()
        for _ in range(-(-len(done) // (len(tail) - k))):
            done.pop(0)()

    def emit_last(i, carry):
        emit(P - 1, i)
        return carry

    lax.fori_loop(emit_blocks.start, emit_blocks.stop, emit_last, 0)


def _gdn_scan(qkv, proj, gcol, grow, norm_w, *, n_ctx, need_ctx, n_qk_heads, z_col_off):
    bsz, n, _ = qkv.shape
    C = GDN_CHUNK
    P = GDN_HEADS_PER_STEP
    HQ = n_qk_heads
    HV = 2 * HQ
    gw = gcol.shape[-1]
    assert gw == HQ * 2 * GDN_INST and HQ % P == 0 and n % (GDN_PREP_CHUNKS * C) == 0 and n_ctx % C == 0
    assert n % GDN_EMIT_ROWS == 0 and n_ctx % GDN_EMIT_ROWS == 0
    assert (n // C) % 2 == 0 and (n_ctx // C) % 2 == 0 and GDN_PREP_CHUNKS % 2 == 0
    nch = n // C
    skip = 0 if need_ctx else n_ctx // C
    n_out = n - skip * C
    qk_w = P * GDN_DK
    v_w = P * 2 * GDN_DV
    assert z_col_off % v_w == 0
    k_blk, v_blk, z_blk = HQ * GDN_DK // qk_w, 2 * HQ * GDN_DK // v_w, z_col_off // v_w
    kern = functools.partial(_gdn_kernel, n_chunks=nch, ctx_chunks=n_ctx // C, skip_chunks=skip)
    return pl.pallas_call(
        kern,
        grid=(bsz, HQ // P),
        in_specs=[
            pl.BlockSpec((None, n, qk_w), lambda b, h: (b, 0, h)),
            pl.BlockSpec((None, n, qk_w), lambda b, h: (b, 0, k_blk + h)),
            pl.BlockSpec((None, n, v_w), lambda b, h: (b, 0, v_blk + h)),
            pl.BlockSpec((None, n, v_w), lambda b, h: (b, 0, z_blk + h)),
            pl.BlockSpec((None, n, gw), lambda b, h: (b, 0, 0)),
            pl.BlockSpec((None, nch, P * 2 * GDN_INST, C), lambda b, h: (b, 0, h, 0)),
            pl.BlockSpec((1, GDN_DV), lambda b, h: (0, 0)),
        ],
        out_specs=pl.BlockSpec((None, n_out, v_w), lambda b, h: (b, 0, h)),
        out_shape=jax.ShapeDtypeStruct((bsz, n_out, HV * GDN_DV), BF16),
        scratch_shapes=[pltpu.VMEM((6 + len(GDN_INV_LEVELS), C, C), F32),
                        pltpu.VMEM((P, n, gw), F32),
                        pltpu.VMEM((2, GDN_INST, GDN_PREP_CHUNKS, C, GDN_DV), F32),
                        pltpu.VMEM((2, GDN_INST, GDN_PREP_CHUNKS, 2 * C, GDN_DK), BF16),
                        pltpu.VMEM((2, GDN_INST, GDN_PREP_CHUNKS, C + GDN_DK, C), BF16),
                        pltpu.VMEM((P, 2, n, GDN_DV), F32), pltpu.VMEM((P, 2, n, GDN_DV), F32),
                        pltpu.VMEM((GDN_INST, GDN_DK, GDN_DV), F32)],
        compiler_params=_cparams("arbitrary", "arbitrary"),
        name="gdn_scan",
    )(qkv, qkv, qkv, proj, gcol, grow, norm_w.astype(F32).reshape(1, GDN_DV))


def kernel(x, c, ctx, c_ctx, ada_w, ada_b, ln_g, ln_b, ret_w_in, ret_decay, ret_w_out,
           gdn_w_in, gdn_conv, gdn_a_log, gdn_dt_bias, gdn_norm, gdn_w_out, ffn_w_in, ffn_w_out):
    bsz, n_lat, d = x.shape
    n_ctx = ctx.shape[1]
    depth = ada_w.shape[0]
    n_tot = n_ctx + n_lat
    alpha = (2 * depth) ** 0.25

    pad_rows = (-(bsz + 1)) % 16
    cc = jnp.concatenate([c, c_ctx[None], jnp.zeros((pad_rows, d), F32)], axis=0)
    mods = _ada_mod(cc, ada_w, ada_b)

    def layer_params(i):
        lat = mods[i, :bsz].reshape(bsz, 1, 6, d)
        cx = jnp.broadcast_to(mods[i, bsz].reshape(1, 1, 6, d), (bsz, 1, 6, d))
        return jnp.concatenate([cx, lat], axis=1)

    params = [layer_params(i) for i in range(depth)]
    h = (ctx, x)
    u = _modulate(h, params[0], shift=0, scale=1, n_ctx=n_ctx, n=n_tot)
    ctx_rows = n_ctx

    for i in range(depth):
        need_ctx = i < depth - 1
        j = i // 2
        u2 = u.reshape(bsz * n_tot, d)
        if i % 2 == 0:
            qkvg = _matmul(u2, ret_w_in, j, n_out=ret_w_in.shape[-1], tn=1024, tm=_row_tile(bsz * n_tot),
                           out_dtype=BF16, name="ret_in_proj").reshape(bsz, n_tot, -1)
            dk = qkvg.shape[-1] // (6 * RET_HEADS)
            cos, sin = _rope_tables(n_ctx, n_lat, dk)
            o = _retention(qkvg, ret_decay[j], cos, sin, n_ctx=n_ctx, need_ctx=need_ctx)
            w_out = ret_w_out
        else:
            qkv_w = gdn_conv.shape[-1]
            hv = gdn_a_log.shape[-1]
            main_w = qkv_w + hv * GDN_DV
            proj = _matmul(u2, gdn_w_in, j, n_out=main_w, tn=1024, tm=_row_tile(bsz * n_tot), out_dtype=BF16,
                           name="gdn_in_proj").reshape(bsz, n_tot, main_w)
            w_gate = jnp.take(gdn_w_in[j, :, main_w:], _gate_lane_order(hv // 2), axis=1)[None]
            ba = _matmul(u2, w_gate, 0, n_out=4 * hv, tn=4 * hv, out_dtype=F32,
                         name="gdn_gate_proj").reshape(bsz, n_tot, 4 * hv)
            gcol, grow = _gdn_gates(ba, gdn_a_log[j], gdn_dt_bias[j])
            qkv = _gdn_conv(proj, gdn_conv[j], n_ctx=n_ctx, n_qk_heads=hv // 2)
            o = _gdn_scan(qkv, proj, gcol, grow, gdn_norm[j], n_ctx=n_ctx, need_ctx=need_ctx,
                          n_qk_heads=hv // 2, z_col_off=qkv_w)
            w_out = gdn_w_out

        n_rows = o.shape[1]
        out_ctx = n_ctx if need_ctx else 0
        y = _matmul(o.reshape(bsz * n_rows, -1), w_out, j, n_out=d, tn=512, out_dtype=F32,
                    name="mixer_out_proj").reshape(bsz, n_rows, d)
        h_off = ctx_rows - out_ctx
        if isinstance(h, tuple) and h_off:
            h, h_off = h[1], 0
        h, u = _resid_ln(h, y, params[i], ln_g[i, 0], ln_b[i, 0], params[i], alpha=alpha, gate=2,
                         shift=3, scale=4, n_ctx=out_ctx, h_row_offset=h_off)
        ctx_rows = out_ctx
        y = _ffn(u, ffn_w_in, ffn_w_out, i)
        if need_ctx:
            h, u = _resid_ln(h, y, params[i], ln_g[i, 1], ln_b[i, 1], params[i + 1], alpha=alpha, gate=5,
                             shift=0, scale=1, n_ctx=out_ctx)
        else:
            h = _resid_ln(h, y, params[i], ln_g[i, 1], ln_b[i, 1], params[i], alpha=alpha, gate=5,
                          shift=0, scale=1, n_ctx=out_ctx, with_u=False)
    return h
```

```python
import functools
import math

import jax
import jax.numpy as jnp
from jax import lax
from jax.experimental import pallas as pl
from jax.experimental.pallas import tpu as pltpu

F32 = jnp.float32
BF16 = jnp.bfloat16

GRID_W = 64
RET_HEADS = 8
RET_CHUNK = 256
RET_UNROLL = 9
ROPE_BASE = 10000.0
GDN_DK = 128
GDN_DV = 128
GDN_CONV = 5
GDN_CHUNK = 64
LN_EPS = 1e-5

VMEM_LIMIT_BYTES = 56 * 1024 * 1024
ROW_TILE = 256
MM_ROW_TILE = 1024
MM_ROW_TILE_CAP = 1536


def _row_tile(m):
    return max(t for t in range(ROW_TILE, MM_ROW_TILE_CAP + 1, ROW_TILE) if m % t == 0)


def _cparams(*sem):
    return pltpu.CompilerParams(dimension_semantics=sem, vmem_limit_bytes=VMEM_LIMIT_BYTES)


def _dot(a, b):
    return jnp.dot(a, b, preferred_element_type=F32)


def _dot_nt(a, b):
    return lax.dot_general(a, b, (((1,), (1,)), ((), ())), preferred_element_type=F32)


def _ada_kernel(c_ref, w_ref, b_ref, o_ref):
    a = jax.nn.silu(c_ref[...]).astype(BF16)
    o_ref[...] = _dot(a, w_ref[...].astype(BF16)) + b_ref[...]


def _ada_mod(cc, ada_w, ada_b, tn=1024):
    depth, d, n = ada_w.shape
    rows = cc.shape[0]
    return pl.pallas_call(
        _ada_kernel,
        grid=(depth, n // tn),
        in_specs=[
            pl.BlockSpec((rows, d), lambda l, j: (0, 0)),
            pl.BlockSpec((None, d, tn), lambda l, j: (l, 0, j)),
            pl.BlockSpec((None, 1, tn), lambda l, j: (l, 0, j)),
        ],
        out_specs=pl.BlockSpec((None, rows, tn), lambda l, j: (l, 0, j)),
        out_shape=jax.ShapeDtypeStruct((depth, rows, n), F32),
        compiler_params=_cparams("arbitrary", "arbitrary"),
        name="ada_mod",
    )(cc, ada_w, ada_b.reshape(depth, 1, n))


def _load_h(h_refs, ctx_tiles):
    if len(h_refs) == 1:
        return h_refs[0][...]
    return jnp.where(pl.program_id(1) < ctx_tiles, h_refs[0][...], h_refs[1][...])


def _h_operands(h, d, ctx_tiles, off=0):
    if not isinstance(h, tuple):
        return [h], [pl.BlockSpec((None, ROW_TILE, d), lambda b, r: (b, r + off, 0))]
    assert off == 0
    return list(h), [
        pl.BlockSpec((None, ROW_TILE, d), lambda b, r: (b, jnp.minimum(r, ctx_tiles - 1), 0)),
        pl.BlockSpec((None, ROW_TILE, d), lambda b, r: (b, jnp.maximum(r - ctx_tiles, 0), 0))]


def _mod_kernel(*refs, shift, scale, ctx_tiles):
    *h_refs, p_ref, u_ref = refs
    p = p_ref[...]
    h = _load_h(h_refs, ctx_tiles)
    u_ref[...] = (h * (1.0 + p[scale:scale + 1]) + p[shift:shift + 1]).astype(u_ref.dtype)


def _param_spec(d, seg_of_tile):
    return pl.BlockSpec((None, None, 6, d), lambda b, r: (b, seg_of_tile(r), 0, 0))


def _modulate(h, p, *, shift, scale, n_ctx, n):
    bsz, _, _, d = p.shape
    ctx_tiles = n_ctx // ROW_TILE
    seg = lambda r: jnp.where(r < ctx_tiles, 0, 1)
    h_args, h_specs = _h_operands(h, d, ctx_tiles)
    return pl.pallas_call(
        functools.partial(_mod_kernel, shift=shift, scale=scale, ctx_tiles=ctx_tiles),
        grid=(bsz, n // ROW_TILE),
        in_specs=h_specs + [_param_spec(d, seg)],
        out_specs=pl.BlockSpec((None, ROW_TILE, d), lambda b, r: (b, r, 0)),
        out_shape=jax.ShapeDtypeStruct((bsz, n, d), BF16),
        compiler_params=_cparams("arbitrary", "arbitrary"),
        name="modulate",
    )(*h_args, p)


def _resid_ln_kernel(*refs, alpha, gate, shift, scale, ctx_tiles, n_h):
    h_refs, (y_ref, p_ref, g_ref, b_ref, p2_ref, h_out, *u_out) = refs[:n_h], refs[n_h:]
    p = p_ref[...]
    x = alpha * _load_h(h_refs, ctx_tiles) + p[gate:gate + 1] * y_ref[...].astype(F32)
    mu = jnp.mean(x, axis=-1, keepdims=True)
    xc = x - mu
    var = jnp.mean(xc * xc, axis=-1, keepdims=True)
    hn = xc * lax.rsqrt(var + LN_EPS) * g_ref[...] + b_ref[...]
    h_out[...] = hn
    if u_out:
        p2 = p2_ref[...]
        u_out[0][...] = (hn * (1.0 + p2[scale:scale + 1]) + p2[shift:shift + 1]).astype(BF16)


def _resid_ln(h, y, p, ln_g, ln_b, p2, *, alpha, gate, shift, scale, n_ctx, h_row_offset=0, with_u=True):
    bsz, n, d = y.shape
    ctx_tiles = n_ctx // ROW_TILE
    seg = lambda r: jnp.where(r < ctx_tiles, 0, 1)
    h_args, h_specs = _h_operands(h, d, ctx_tiles, h_row_offset // ROW_TILE)
    row = pl.BlockSpec((None, ROW_TILE, d), lambda b, r: (b, r, 0))
    vec = pl.BlockSpec((1, d), lambda b, r: (0, 0))
    out_shape = [jax.ShapeDtypeStruct((bsz, n, d), F32)]
    out_specs = [row]
    if with_u:
        out_shape.append(jax.ShapeDtypeStruct((bsz, n, d), BF16))
        out_specs.append(row)
    outs = pl.pallas_call(
        functools.partial(_resid_ln_kernel, alpha=alpha, gate=gate, shift=shift, scale=scale,
                          ctx_tiles=ctx_tiles, n_h=len(h_args)),
        grid=(bsz, n // ROW_TILE),
        in_specs=h_specs + [row, _param_spec(d, seg), vec, vec, _param_spec(d, seg)],
        out_specs=out_specs,
        out_shape=out_shape,
        compiler_params=_cparams("arbitrary", "arbitrary"),
        name="resid_ln",
    )(*h_args, y, p, ln_g.reshape(1, d), ln_b.reshape(1, d), p2)
    return outs if with_u else outs[0]


def _mm_kernel(a_ref, w_ref, o_ref, wb_ref):
    @pl.when(pl.program_id(1) == 0)
    def _():
        wb_ref[...] = w_ref[...].astype(BF16)

    o_ref[...] = _dot(a_ref[...], wb_ref[...]).astype(o_ref.dtype)


def _matmul(a, w, layer, *, n_out, col_off=0, tn, tm=MM_ROW_TILE, out_dtype, name):
    m, k = a.shape
    assert m % tm == 0 and n_out % tn == 0 and col_off % tn == 0
    off = col_off // tn
    return pl.pallas_call(
        _mm_kernel,
        grid=(n_out // tn, m // tm),
        in_specs=[pl.BlockSpec((tm, k), lambda j, i: (i, 0)),
                  pl.BlockSpec((None, k, tn), lambda j, i: (layer, 0, j + off))],
        out_specs=pl.BlockSpec((tm, tn), lambda j, i: (i, j)),
        out_shape=jax.ShapeDtypeStruct((m, n_out), out_dtype),
        scratch_shapes=[pltpu.VMEM((k, tn), BF16)],
        compiler_params=_cparams("arbitrary", "arbitrary"),
        name=name,
    )(a, w)


def _swiglu_kernel(a_ref, wg_ref, wu_ref, o_ref, wgb_ref, wub_ref):
    @pl.when(pl.program_id(1) == 0)
    def _():
        wgb_ref[...] = wg_ref[...].astype(BF16)
        wub_ref[...] = wu_ref[...].astype(BF16)

    a = a_ref[...]
    gate = _dot(a, wgb_ref[...])
    up = _dot(a, wub_ref[...])
    o_ref[...] = (jax.nn.silu(gate) * up).astype(o_ref.dtype)


def _swiglu_in(a, w, layer, *, tn=512, tm=MM_ROW_TILE):
    m, k = a.shape
    f = w.shape[2] // 2
    assert m % tm == 0 and f % tn == 0
    nb = f // tn
    return pl.pallas_call(
        _swiglu_kernel,
        grid=(nb, m // tm),
        in_specs=[pl.BlockSpec((tm, k), lambda j, i: (i, 0)),
                  pl.BlockSpec((None, k, tn), lambda j, i: (layer, 0, j)),
                  pl.BlockSpec((None, k, tn), lambda j, i: (layer, 0, j + nb))],
        out_specs=pl.BlockSpec((tm, tn), lambda j, i: (i, j)),
        out_shape=jax.ShapeDtypeStruct((m, f), BF16),
        scratch_shapes=[pltpu.VMEM((k, tn), BF16), pltpu.VMEM((k, tn), BF16)],
        compiler_params=_cparams("arbitrary", "arbitrary"),
        name="ffn_in_swiglu",
    )(a, w, w)


def _ffn(u, w_in, w_out, layer):
    bsz, n, d = u.shape
    a = _swiglu_in(u.reshape(bsz * n, d), w_in, layer)
    y = _matmul(a, w_out, layer, n_out=d, tn=512, tm=512, out_dtype=F32, name="ffn_out")
    return y.reshape(bsz, n, d)


def _ret_kernel(draw_ref, q_ref, k_ref, v_ref, g_ref, cos_ref, sin_ref, o_ref,
                qs_ref, ks_ref, oacc_ref, sf_ref, sb_ref, *, n_chunks, ctx_chunks, skip_chunks, dk):
    C = RET_CHUNK
    h = pl.program_id(1)
    half = dk // 2
    lg_f = -jnp.exp(jnp.full((1, 1), draw_ref[0, h], F32))
    lg_b = -jnp.exp(jnp.full((1, 1), draw_ref[1, h], F32))

    ri = lax.broadcasted_iota(jnp.int32, (C, C), 0)
    ci = lax.broadcasted_iota(jnp.int32, (C, C), 1)
    diff = (ri - ci).astype(F32)
    lower = diff >= 0
    dmat = jnp.where(lower, jnp.exp(lg_f * jnp.where(lower, diff, 0.0)),
                     jnp.exp(lg_b * jnp.where(lower, 0.0, -diff)))
    rc = lax.broadcasted_iota(jnp.int32, (C, 1), 0).astype(F32)
    qdec_f = jnp.exp(lg_f * (rc + 1.0))
    kdec_f = jnp.exp(lg_f * (C - 1.0 - rc))
    cdec_f = jnp.exp(lg_f * C)
    qdec_b = jnp.exp(lg_b * (C - rc))
    kdec_b = jnp.exp(lg_b * rc)
    cdec_b = jnp.exp(lg_b * C)

    sf_ref[...] = jnp.zeros_like(sf_ref)
    sb_ref[...] = jnp.zeros_like(sb_ref)
    qscale = dk ** -0.5

    def rope(x, cs, sn):
        x1, x2 = x[:, :half], x[:, half:]
        return jnp.concatenate([x1 * cs - x2 * sn, x1 * sn + x2 * cs], axis=-1)

    def fwd(c, carry):
        rows = pl.ds(pl.multiple_of(c * C, C), C)
        cs, sn = cos_ref[rows, :], sin_ref[rows, :]
        q = rope(q_ref[rows, :].astype(F32), cs, sn) * qscale
        k = rope(k_ref[rows, :].astype(F32), cs, sn)
        qb, kb = q.astype(BF16), k.astype(BF16)
        qs_ref[rows, :] = qb
        ks_ref[rows, :] = kb
        v = v_ref[rows, :]
        s = (_dot_nt(qb, kb) * dmat).astype(BF16)
        state = sf_ref[...]
        oacc_ref[rows, :] = _dot(s, v) + _dot(qb, state.astype(BF16)) * qdec_f
        kd_t = jnp.transpose(k * kdec_f).astype(BF16)
        sf_ref[...] = cdec_f * state + _dot(kd_t, v)
        return carry

    lax.fori_loop(0, n_chunks, fwd, 0, unroll=RET_UNROLL)

    def bwd(t, carry):
        c = jnp.where(t < ctx_chunks, ctx_chunks - 1 - t, n_chunks - 1 + ctx_chunks - t)
        rows = pl.ds(pl.multiple_of(c * C, C), C)
        qb, kb, v = qs_ref[rows, :], ks_ref[rows, :], v_ref[rows, :]
        state = sb_ref[...]
        o = oacc_ref[rows, :] + _dot(qb, state.astype(BF16)) * qdec_b
        kd_t = jnp.transpose(kb.astype(F32) * kdec_b).astype(BF16)
        sb_ref[...] = cdec_b * state + _dot(kd_t, v)

        def emit():
            mu = jnp.mean(o, axis=-1, keepdims=True)
            oc = o - mu
            var = jnp.mean(oc * oc, axis=-1, keepdims=True)
            on = oc * lax.rsqrt(var + LN_EPS)
            out_rows = pl.ds(pl.multiple_of((c - skip_chunks) * C, C), C)
            o_ref[out_rows, :] = (on * jax.nn.silu(g_ref[rows, :].astype(F32))).astype(o_ref.dtype)

        if skip_chunks:
            pl.when(c >= skip_chunks)(emit)
        else:
            emit()
        return carry

    lax.fori_loop(0, n_chunks, bwd, 0, unroll=RET_UNROLL)


def _retention(qkvg, decay_raw, cos, sin, *, n_ctx, need_ctx):
    bsz, n, width = qkvg.shape
    H = RET_HEADS
    dk = width // (6 * H)
    dv = 2 * dk
    C = RET_CHUNK
    assert n % C == 0 and n_ctx % C == 0
    skip = 0 if need_ctx else n_ctx // C
    n_out = n - skip * C
    kern = functools.partial(_ret_kernel, n_chunks=n // C, ctx_chunks=n_ctx // C, skip_chunks=skip, dk=dk)
    return pl.pallas_call(
        kern,
        grid=(bsz, H),
        in_specs=[
            pl.BlockSpec(memory_space=pltpu.SMEM),
            pl.BlockSpec((None, n, dk), lambda b, h: (b, 0, h)),
            pl.BlockSpec((None, n, dk), lambda b, h: (b, 0, H + h)),
            pl.BlockSpec((None, n, dv), lambda b, h: (b, 0, H + h)),
            pl.BlockSpec((None, n, dv), lambda b, h: (b, 0, 2 * H + h)),
            pl.BlockSpec((n, dk // 2), lambda b, h: (0, 0)),
            pl.BlockSpec((n, dk // 2), lambda b, h: (0, 0)),
        ],
        out_specs=pl.BlockSpec((None, n_out, dv), lambda b, h: (b, 0, h)),
        out_shape=jax.ShapeDtypeStruct((bsz, n_out, H * dv), BF16),
        scratch_shapes=[pltpu.VMEM((n, dk), BF16), pltpu.VMEM((n, dk), BF16), pltpu.VMEM((n, dv), F32),
                        pltpu.VMEM((dk, dv), F32), pltpu.VMEM((dk, dv), F32)],
        compiler_params=_cparams("arbitrary", "arbitrary"),
        name="retention",
    )(decay_raw.astype(F32), qkvg, qkvg, qkvg, qkvg, cos, sin)


def _rope_tables(n_ctx, n_lat, dim):
    rows = n_lat // GRID_W
    row = jnp.repeat(jnp.arange(rows, dtype=F32), GRID_W)
    col = jnp.tile(jnp.arange(GRID_W, dtype=F32), rows)
    n_freq = dim // 4
    inv_freq = ROPE_BASE ** (-jnp.arange(n_freq, dtype=F32) / n_freq)
    ang = jnp.concatenate([row[:, None] * inv_freq, col[:, None] * inv_freq], axis=-1)
    cos = jnp.concatenate([jnp.ones((n_ctx, dim // 2), F32), jnp.cos(ang)], axis=0)
    sin = jnp.concatenate([jnp.zeros((n_ctx, dim // 2), F32), jnp.sin(ang)], axis=0)
    return cos, sin


def _conv_kernel(x_ref, w_ref, o_ref, *, n_ctx, n_q_blocks, n_qk_blocks, qscale):
    j = pl.program_id(1)
    x = x_ref[...].astype(F32)
    n, tc = x.shape
    w = w_ref[...]
    pad = GDN_CONV // 2
    t = lax.broadcasted_iota(jnp.int32, (n, 1), 0)
    lo = jnp.where(t < n_ctx, 0, n_ctx)
    hi = jnp.where(t < n_ctx, n_ctx, n)
    acc = x * w[pad:pad + 1]
    for d in range(-pad, pad + 1):
        if d == 0:
            continue
        xs = pltpu.roll(x, (n - d) % n, 0)
        valid = (t + d >= lo) & (t + d < hi)
        acc = acc + jnp.where(valid, xs, 0.0) * w[d + pad:d + pad + 1]
    y = jax.nn.silu(acc)
    do_norm = j < n_qk_blocks
    scale = jnp.where(j < n_q_blocks, qscale, 1.0).astype(F32)
    for g0 in range(0, tc, GDN_DK):
        yg = y[:, g0:g0 + GDN_DK]
        ss = jnp.sum(yg * yg, axis=-1, keepdims=True)
        yn = yg * (lax.rsqrt(ss + 1e-6) * scale)
        o_ref[:, g0:g0 + GDN_DK] = jnp.where(do_norm, yn, yg).astype(o_ref.dtype)


def _gdn_conv(proj, conv_w, *, n_ctx, n_qk_heads, tc=512):
    bsz, n, _ = proj.shape
    width = conv_w.shape[1]
    qw = n_qk_heads * GDN_DK
    assert qw % tc == 0 and width % tc == 0
    kern = functools.partial(_conv_kernel, n_ctx=n_ctx, n_q_blocks=qw // tc, n_qk_blocks=2 * qw // tc,
                             qscale=GDN_DK ** -0.5)
    return pl.pallas_call(
        kern,
        grid=(bsz, width // tc),
        in_specs=[pl.BlockSpec((None, n, tc), lambda b, j: (b, 0, j)),
                  pl.BlockSpec((GDN_CONV, tc), lambda b, j: (0, j))],
        out_specs=pl.BlockSpec((None, n, tc), lambda b, j: (b, 0, j)),
        out_shape=jax.ShapeDtypeStruct((bsz, n, width), BF16),
        compiler_params=_cparams("arbitrary", "arbitrary"),
        name="gdn_conv",
    )(proj, conv_w)


def _gates_kernel(x_ref, alog_ref, dt_ref, col_ref, row_ref):
    C = GDN_CHUNK
    S = GDN_INST
    x = x_ref[...]
    n = x.shape[0]
    beta = jax.nn.sigmoid(x)
    z = x + dt_ref[...]
    softplus = jnp.maximum(z, 0.0) + jnp.log1p(jnp.exp(-jnp.abs(z)))
    la = -jnp.exp(alog_ref[...]) * softplus
    tm = lax.broadcasted_iota(jnp.int32, (n, 1), 0) % C
    pre = la
    suf = la
    s = 1
    while s < C:
        pre = pre + jnp.where(tm >= s, pltpu.roll(pre, s, 0), 0.0)
        suf = suf + jnp.where(tm < C - s, pltpu.roll(suf, n - s, 0), 0.0)
        s *= 2
    lane = lax.broadcasted_iota(jnp.int32, (1, x.shape[1]), 1) % (2 * S)
    is_fwd = (lane % S) < S // 2
    col_ref[...] = jnp.where(lane < S, beta, jnp.where(is_fwd, pre, suf))

    def xpose(p, carry):
        t = jnp.transpose(col_ref[pl.ds(pl.multiple_of(p * 2 * C, 2 * C), 2 * C), :])
        row_ref[2 * p] = t[:, :C]
        row_ref[2 * p + 1] = t[:, C:]
        return carry

    lax.fori_loop(0, n // (2 * C), xpose, 0)


def _gate_lane_order(n_qk_heads):
    idx = jnp.arange(8 * n_qk_heads).reshape(2, 2, n_qk_heads, 2)
    return idx.transpose(2, 0, 1, 3).reshape(-1)


def _gdn_gates(ba, a_log, dt_bias):
    bsz, n, w = ba.shape
    C = GDN_CHUNK
    hq = w // (2 * GDN_INST)

    def per_lane(p):
        p = p.astype(F32).reshape(2, hq, 2).transpose(1, 0, 2).reshape(hq, GDN_INST)
        return jnp.concatenate([jnp.zeros_like(p), p], axis=1).reshape(1, w)

    return pl.pallas_call(
        _gates_kernel,
        grid=(bsz,),
        in_specs=[pl.BlockSpec((None, n, w), lambda b: (b, 0, 0)),
                  pl.BlockSpec((1, w), lambda b: (0, 0)), pl.BlockSpec((1, w), lambda b: (0, 0))],
        out_specs=[pl.BlockSpec((None, n, w), lambda b: (b, 0, 0)),
                   pl.BlockSpec((None, n // C, w, C), lambda b: (b, 0, 0, 0))],
        out_shape=[jax.ShapeDtypeStruct((bsz, n, w), F32), jax.ShapeDtypeStruct((bsz, n // C, w, C), F32)],
        compiler_params=_cparams("arbitrary"),
        name="gdn_gates",
    )(ba, per_lane(a_log), per_lane(dt_bias))


GDN_INST = 4
GDN_STACK = GDN_INST * GDN_CHUNK
GDN_PREP_CHUNKS = 6
GDN_EMIT_ROWS = 256
GDN_HEADS_PER_STEP = 2


GDN_INV_BASE = 8
GDN_INV_LEVELS = (16, 32, 64)


def _merge_rows(g, lower):
    h = g // 2
    return [(r0 + h, r0 + g) if lower else (r0, r0 + h) for r0 in range(0, GDN_CHUNK, g)]


def _unit_tri_inverse(a_list, lower_list, same_base, off_masks, eye, between):
    def mm(x, y):
        return _dot(x.astype(BF16), y.astype(BF16))

    ad = [a * same_base for a in a_list]
    x = [eye - d for d in ad]
    p = [mm(d, d) for d in ad]
    between()
    xp = [mm(xi, pi) for xi, pi in zip(x, p)]
    p = [mm(pi, pi) for pi in p]
    between()
    x = [xi + xpi for xi, xpi in zip(x, xp)]
    x = [xi + mm(xi, pi) for xi, pi in zip(x, p)]
    between()
    for g, off in zip(GDN_INV_LEVELS, off_masks):
        h = g // 2
        sels = [_merge_rows(g, lower) for lower in lower_list]
        top = [jnp.concatenate([xi[a:b] for a, b in sel], axis=0) for xi, sel in zip(x, sels)]
        n = [mm(ti, a * off) for ti, a in zip(top, a_list)]
        between()
        upd = [ti - mm(ni, xi) for ti, ni, xi in zip(top, n, x)]
        between()
        merged = []
        for xi, ui, sel in zip(x, upd, sels):
            pieces = []
            for idx, (a, b) in enumerate(sel):
                new = ui[idx * h:(idx + 1) * h]
                r0 = (a // g) * g
                pieces += [xi[r0:r0 + h], new] if a != r0 else [new, xi[r0 + h:r0 + g]]
            merged.append(jnp.concatenate(pieces, axis=0))
        x = merged
    return x


def _gdn_kernel(q_ref, k_ref, v_ref, z_ref, gcol_ref, grow_ref, nw_ref, o_ref,
                mask_ref, gsel_ref, u0_ref, wq_ref, qk_ref, of_ref, ob_ref, m_ref,
                *, n_chunks, ctx_chunks, skip_chunks):
    C = GDN_CHUNK
    S = GDN_INST
    dk = GDN_DK
    dv = GDN_DV
    n_lvl = len(GDN_INV_LEVELS)
    M_INCL, M_STRICT, M_BASE, M_OFF, M_EYE = 0, 2, 4, 5, 5 + n_lvl

    r = lax.broadcasted_iota(jnp.int32, (C, C), 0)
    c = lax.broadcasted_iota(jnp.int32, (C, C), 1)
    mask_ref[M_INCL] = (r >= c).astype(F32)
    mask_ref[M_INCL + 1] = (c >= r).astype(F32)
    mask_ref[M_STRICT] = (r > c).astype(F32)
    mask_ref[M_STRICT + 1] = (c > r).astype(F32)
    mask_ref[M_BASE] = ((r // GDN_INV_BASE) == (c // GDN_INV_BASE)).astype(F32)
    for lvl, g in enumerate(GDN_INV_LEVELS):
        h = g // 2
        mask_ref[M_OFF + lvl] = (((r // g) == (c // g)) & ((r // h) != (c // h))).astype(F32)
    mask_ref[M_EYE] = (r == c).astype(F32)

    P = GDN_HEADS_PER_STEP

    def lanes(start):
        return pl.ds(pl.multiple_of(start, dk), dk)

    gate_lanes = gcol_ref.shape[-1]

    def head_roll(hh):
        first = (pl.program_id(1) * P + hh) * 2 * S
        return (gate_lanes - first) % gate_lanes

    def head_rows(hh):
        return pl.ds(pl.multiple_of(hh * 2 * S, 2 * S), 2 * S)

    def select_gates(i, carry):
        rows = pl.ds(pl.multiple_of(i * GDN_EMIT_ROWS, GDN_EMIT_ROWS), GDN_EMIT_ROWS)
        g = gcol_ref[rows, :]
        for hh in range(P):
            gsel_ref[hh, rows, :] = pltpu.roll(g, head_roll(hh), 1)
        return carry

    lax.fori_loop(0, n_chunks * C // GDN_EMIT_ROWS, select_gates, 0, unroll=3)

    NB = GDN_PREP_CHUNKS
    n_blocks = n_chunks // NB

    def scan_chunks(t):
        cb = jnp.where(t < ctx_chunks, ctx_chunks - 1 - t, n_chunks - 1 + ctx_chunks - t)
        return t, cb

    def prep_block(g, buf, between):
        hh, j = g // n_blocks, g % n_blocks
        between()
        kts = {}
        for i in range(0, NB, 2):
            for d in range(2):
                lo = scan_chunks(j * NB + i + d)[d]
                slab = pl.ds(pl.multiple_of(lo * C, 2 * C), 2 * C)
                kt2 = jnp.transpose(k_ref[slab, lanes(hh * dk)].astype(F32))
                kts[(i + d, d)], kts[(i + 1 - d, d)] = kt2[:, :C], kt2[:, C:]
        items = []
        for i in range(NB):
            for d, ci in enumerate(scan_chunks(j * NB + i)):
                rows = pl.ds(pl.multiple_of(ci * C, C), C)
                qb, kb = q_ref[rows, lanes(hh * dk)], k_ref[rows, lanes(hh * dk)]
                pk = _dot_nt(jnp.concatenate([kb, qb], axis=0), kb)
                kk, qk = pk[:C], pk[C:]
                g8 = gsel_ref[hh, rows, :]
                grow8 = grow_ref[ci, head_rows(hh), :]
                qf, kf = qb.astype(F32), kb.astype(F32)
                kt = kts[(i, d)]
                for vj in range(2):
                    s = 2 * d + vj
                    vf = v_ref[rows, lanes((2 * hh + vj) * dv)].astype(F32)
                    beta, gc = g8[:, s:s + 1], g8[:, S + s:S + s + 1]
                    g_row = grow8[S + s:S + s + 1, :]
                    e = jnp.exp(jnp.minimum(gc - g_row, 0.0))
                    a = beta * kk * (e * mask_ref[M_STRICT + d])
                    qkm = (qk * (e * mask_ref[M_INCL + d])).astype(BF16)
                    eg = jnp.exp(gc)
                    rhs = jnp.concatenate([vf * beta, kf * (beta * eg)], axis=-1)
                    g_last = g_row[:, 0:1] if d else g_row[:, C - 1:C]
                    kdt = (kt * jnp.exp(g_last - g_row)).astype(BF16)
                    items.append(dict(slot=i, s=s, a=a, qkm=qkm, rhs=rhs.astype(BF16), qd=qf * eg, kdt=kdt))
        between()

        xs = _unit_tri_inverse([it["a"] for it in items], [it["s"] < 2 for it in items], mask_ref[M_BASE],
                               [mask_ref[M_OFF + i] for i in range(n_lvl)], mask_ref[M_EYE], between)
        sols = [_dot(x.astype(BF16), it["rhs"]) for x, it in zip(xs, items)]
        between()
        for it, sol in zip(items, sols):
            s, slot = it["s"], it["slot"]
            u0_ref[buf, s, slot] = sol[:, :dv]
            wq_ref[buf, s, slot, 0:C, :] = sol[:, dv:].astype(BF16)
            wq_ref[buf, s, slot, C:2 * C, :] = it["qd"].astype(BF16)
            qk_ref[buf, s, slot, 0:C, :] = it["qkm"]
            qk_ref[buf, s, slot, C:C + dk, :] = it["kdt"]

    def scan_first(g, buf, i, st):
        j = g % n_blocks
        ms = [m_ref[s] for s in range(S)]
        if i == 0:
            ms = [jnp.where(j == 0, 0.0, m) for m in ms]
        st["ms"] = ms
        st["wm"] = [_dot(wq_ref[buf, s, i], ms[s].astype(BF16)) for s in range(S)]

    def scan_second(g, buf, i, st):
        hh, j = g // n_blocks, g % n_blocks
        cf, cb = scan_chunks(j * NB + i)
        cis = [cf, cf, cb, cb]
        ms, wm = st["ms"], st["wm"]
        ub = [(u0_ref[buf, s, i] - wm[s][:C]).astype(BF16) for s in range(S)]
        om = [_dot(qk_ref[buf, s, i], ub[s]) for s in range(S)]
        for s in range(S):
            ci = cis[s]
            g_s = grow_ref[ci, head_rows(hh), :][S + s:S + s + 1, :]
            g_last = g_s[:, C - 1:C] if s < 2 else g_s[:, 0:1]
            m_ref[s] = jnp.exp(g_last) * ms[s] + om[s][C:]
            o = wm[s][C:] + om[s][:C]
            rows = pl.ds(pl.multiple_of(ci * C, C), C)
            if s < 2:
                of_ref[hh, s, rows, :] = o
            else:
                ob_ref[hh, s - 2, rows, :] = o

    def scan_thunks(g, buf):
        thunks = []
        for i in range(NB):
            st = {}
            thunks += [functools.partial(scan_first, g, buf, i, st), functools.partial(scan_second, g, buf, i, st)]
        return thunks

    m_ref[...] = jnp.zeros_like(m_ref)
    prep_block(0, 0, lambda: None)

    def pipelined(g, carry):
        buf = g % 2
        pending = scan_thunks(g, buf)

        def between():
            if pending:
                pending.pop(0)()

        prep_block(g + 1, 1 - buf, between)
        while pending:
            pending.pop(0)()
        return carry

    nw = nw_ref[...]
    E = GDN_EMIT_ROWS
    emit_blocks = range(skip_chunks * C // E, n_chunks * C // E)

    def emit(hh, i):
        aligned = (lambda s: s) if isinstance(i, int) else (lambda s: pl.multiple_of(s, E))
        rows = pl.ds(aligned(i * E), E)
        out_rows = pl.ds(aligned(i * E - skip_chunks * C), E)
        for vj in range(2):
            cols = slice((2 * hh + vj) * dv, (2 * hh + vj + 1) * dv)
            o = of_ref[hh, vj, rows, :] + ob_ref[hh, vj, rows, :]
            zf = z_ref[rows, cols].astype(F32)
            on = o * lax.rsqrt(jnp.mean(o * o, axis=-1, keepdims=True) + 1e-6) * nw * jax.nn.silu(zf)
            o_ref[out_rows, cols] = on.astype(o_ref.dtype)

    last = P * n_blocks - 1
    lax.fori_loop(0, last, pipelined, 0)

    done = [functools.partial(emit, hh, i) for hh in range(P - 1) for i in emit_blocks]
    tail = scan_thunks(last, last % 2)
    for k, thunk in enumerate(tail):
        thunk()
        for _ in range(-(-len(done) // (len(tail) - k))):
            done.pop(0)()

    def emit_last(i, carry):
        emit(P - 1, i)
        return carry

    lax.fori_loop(emit_blocks.start, emit_blocks.stop, emit_last, 0)


def _gdn_scan(qkv, proj, gcol, grow, norm_w, *, n_ctx, need_ctx, n_qk_heads, z_col_off):
    bsz, n, _ = qkv.shape
    C = GDN_CHUNK
    P = GDN_HEADS_PER_STEP
    HQ = n_qk_heads
    HV = 2 * HQ
    gw = gcol.shape[-1]
    assert gw == HQ * 2 * GDN_INST and HQ % P == 0 and n % (GDN_PREP_CHUNKS * C) == 0 and n_ctx % C == 0
    assert n % GDN_EMIT_ROWS == 0 and n_ctx % GDN_EMIT_ROWS == 0
    assert (n // C) % 2 == 0 and (n_ctx // C) % 2 == 0 and GDN_PREP_CHUNKS % 2 == 0
    nch = n // C
    skip = 0 if need_ctx else n_ctx // C
    n_out = n - skip * C
    qk_w = P * GDN_DK
    v_w = P * 2 * GDN_DV
    assert z_col_off % v_w == 0
    k_blk, v_blk, z_blk = HQ * GDN_DK // qk_w, 2 * HQ * GDN_DK // v_w, z_col_off // v_w
    kern = functools.partial(_gdn_kernel, n_chunks=nch, ctx_chunks=n_ctx // C, skip_chunks=skip)
    return pl.pallas_call(
        kern,
        grid=(bsz, HQ // P),
        in_specs=[
            pl.BlockSpec((None, n, qk_w), lambda b, h: (b, 0, h)),
            pl.BlockSpec((None, n, qk_w), lambda b, h: (b, 0, k_blk + h)),
            pl.BlockSpec((None, n, v_w), lambda b, h: (b, 0, v_blk + h)),
            pl.BlockSpec((None, n, v_w), lambda b, h: (b, 0, z_blk + h)),
            pl.BlockSpec((None, n, gw), lambda b, h: (b, 0, 0)),
            pl.BlockSpec((None, nch, P * 2 * GDN_INST, C), lambda b, h: (b, 0, h, 0)),
            pl.BlockSpec((1, GDN_DV), lambda b, h: (0, 0)),
        ],
        out_specs=pl.BlockSpec((None, n_out, v_w), lambda b, h: (b, 0, h)),
        out_shape=jax.ShapeDtypeStruct((bsz, n_out, HV * GDN_DV), BF16),
        scratch_shapes=[pltpu.VMEM((6 + len(GDN_INV_LEVELS), C, C), F32),
                        pltpu.VMEM((P, n, gw), F32),
                        pltpu.VMEM((2, GDN_INST, GDN_PREP_CHUNKS, C, GDN_DV), F32),
                        pltpu.VMEM((2, GDN_INST, GDN_PREP_CHUNKS, 2 * C, GDN_DK), BF16),
                        pltpu.VMEM((2, GDN_INST, GDN_PREP_CHUNKS, C + GDN_DK, C), BF16),
                        pltpu.VMEM((P, 2, n, GDN_DV), F32), pltpu.VMEM((P, 2, n, GDN_DV), F32),
                        pltpu.VMEM((GDN_INST, GDN_DK, GDN_DV), F32)],
        compiler_params=_cparams("arbitrary", "arbitrary"),
        name="gdn_scan",
    )(qkv, qkv, qkv, proj, gcol, grow, norm_w.astype(F32).reshape(1, GDN_DV))


def kernel(x, c, ctx, c_ctx, ada_w, ada_b, ln_g, ln_b, ret_w_in, ret_decay, ret_w_out,
           gdn_w_in, gdn_conv, gdn_a_log, gdn_dt_bias, gdn_norm, gdn_w_out, ffn_w_in, ffn_w_out):
    bsz, n_lat, d = x.shape
    n_ctx = ctx.shape[1]
    depth = ada_w.shape[0]
    n_tot = n_ctx + n_lat
    alpha = (2 * depth) ** 0.25

    pad_rows = (-(bsz + 1)) % 16
    cc = jnp.concatenate([c, c_ctx[None], jnp.zeros((pad_rows, d), F32)], axis=0)
    mods = _ada_mod(cc, ada_w, ada_b)

    def layer_params(i):
        lat = mods[i, :bsz].reshape(bsz, 1, 6, d)
        cx = jnp.broadcast_to(mods[i, bsz].reshape(1, 1, 6, d), (bsz, 1, 6, d))
        return jnp.concatenate([cx, lat], axis=1)

    params = [layer_params(i) for i in range(depth)]
    h = (ctx, x)
    u = _modulate(h, params[0], shift=0, scale=1, n_ctx=n_ctx, n=n_tot)
    ctx_rows = n_ctx

    for i in range(depth):
        need_ctx = i < depth - 1
        j = i // 2
        u2 = u.reshape(bsz * n_tot, d)
        if i % 2 == 0:
            qkvg = _matmul(u2, ret_w_in, j, n_out=ret_w_in.shape[-1], tn=1024, tm=_row_tile(bsz * n_tot),
                           out_dtype=BF16, name="ret_in_proj").reshape(bsz, n_tot, -1)
            dk = qkvg.shape[-1] // (6 * RET_HEADS)
            cos, sin = _rope_tables(n_ctx, n_lat, dk)
            o = _retention(qkvg, ret_decay[j], cos, sin, n_ctx=n_ctx, need_ctx=need_ctx)
            w_out = ret_w_out
        else:
            qkv_w = gdn_conv.shape[-1]
            hv = gdn_a_log.shape[-1]
            main_w = qkv_w + hv * GDN_DV
            proj = _matmul(u2, gdn_w_in, j, n_out=main_w, tn=1024, tm=_row_tile(bsz * n_tot), out_dtype=BF16,
                           name="gdn_in_proj").reshape(bsz, n_tot, main_w)
            w_gate = jnp.take(gdn_w_in[j, :, main_w:], _gate_lane_order(hv // 2), axis=1)[None]
            ba = _matmul(u2, w_gate, 0, n_out=4 * hv, tn=4 * hv, out_dtype=F32,
                         name="gdn_gate_proj").reshape(bsz, n_tot, 4 * hv)
            gcol, grow = _gdn_gates(ba, gdn_a_log[j], gdn_dt_bias[j])
            qkv = _gdn_conv(proj, gdn_conv[j], n_ctx=n_ctx, n_qk_heads=hv // 2)
            o = _gdn_scan(qkv, proj, gcol, grow, gdn_norm[j], n_ctx=n_ctx, need_ctx=need_ctx,
                          n_qk_heads=hv // 2, z_col_off=qkv_w)
            w_out = gdn_w_out

        n_rows = o.shape[1]
        out_ctx = n_ctx if need_ctx else 0
        y = _matmul(o.reshape(bsz * n_rows, -1), w_out, j, n_out=d, tn=512, out_dtype=F32,
                    name="mixer_out_proj").reshape(bsz, n_rows, d)
        h_off = ctx_rows - out_ctx
        if isinstance(h, tuple) and h_off:
            h, h_off = h[1], 0
        h, u = _resid_ln(h, y, params[i], ln_g[i, 0], ln_b[i, 0], params[i], alpha=alpha, gate=2,
                         shift=3, scale=4, n_ctx=out_ctx, h_row_offset=h_off)
        ctx_rows = out_ctx
        y = _ffn(u, ffn_w_in, ffn_w_out, i)
        if need_ctx:
            h, u = _resid_ln(h, y, params[i], ln_g[i, 1], ln_b[i, 1], params[i + 1], alpha=alpha, gate=5,
                             shift=0, scale=1, n_ctx=out_ctx)
        else:
            h = _resid_ln(h, y, params[i], ln_g[i, 1], ln_b[i, 1], params[i], alpha=alpha, gate=5,
                          shift=0, scale=1, n_ctx=out_ctx, with_u=False)
    return h
```

```python
import functools

import jax
import jax.numpy as jnp
from jax import lax
from jax.experimental import pallas as pl
from jax.experimental.pallas import tpu as pltpu

F32 = jnp.float32
BF16 = jnp.bfloat16

GRID_W = 64
RET_HEADS = 8
RET_CHUNK = 256
RET_UNROLL = 9
ROPE_BASE = 10000.0
GDN_DK = 128
GDN_DV = 128
GDN_CONV = 5
GDN_CHUNK = 64
LN_EPS = 1e-5

VMEM_LIMIT_BYTES = 56 * 1024 * 1024
ROW_TILE = 256
MM_ROW_TILE = 1024
MM_ROW_TILE_CAP = 1536


def _row_tile(m):
    return max(t for t in range(ROW_TILE, MM_ROW_TILE_CAP + 1, ROW_TILE) if m % t == 0)


def _cparams(*sem):
    return pltpu.CompilerParams(dimension_semantics=sem, vmem_limit_bytes=VMEM_LIMIT_BYTES)


def _dot(a, b):
    return jnp.dot(a, b, preferred_element_type=F32)


def _dot_nt(a, b):
    return lax.dot_general(a, b, (((1,), (1,)), ((), ())), preferred_element_type=F32)


def _ada_kernel(c_ref, w_ref, b_ref, o_ref):
    a = jax.nn.silu(c_ref[...]).astype(BF16)
    o_ref[...] = _dot(a, w_ref[...].astype(BF16)) + b_ref[...]


def _ada_mod(cc, ada_w, ada_b, tn=1024):
    depth, d, n = ada_w.shape
    rows = cc.shape[0]
    return pl.pallas_call(
        _ada_kernel,
        grid=(depth, n // tn),
        in_specs=[
            pl.BlockSpec((rows, d), lambda l, j: (0, 0)),
            pl.BlockSpec((None, d, tn), lambda l, j: (l, 0, j)),
            pl.BlockSpec((None, 1, tn), lambda l, j: (l, 0, j)),
        ],
        out_specs=pl.BlockSpec((None, rows, tn), lambda l, j: (l, 0, j)),
        out_shape=jax.ShapeDtypeStruct((depth, rows, n), F32),
        compiler_params=_cparams("arbitrary", "arbitrary"),
        name="ada_mod",
    )(cc, ada_w, ada_b.reshape(depth, 1, n))


def _load_h(h_refs, ctx_tiles):
    if len(h_refs) == 1:
        return h_refs[0][...]
    return jnp.where(pl.program_id(1) < ctx_tiles, h_refs[0][...], h_refs[1][...])


def _h_operands(h, d, ctx_tiles, off=0):
    if not isinstance(h, tuple):
        return [h], [pl.BlockSpec((None, ROW_TILE, d), lambda b, r: (b, r + off, 0))]
    assert off == 0
    return list(h), [
        pl.BlockSpec((None, ROW_TILE, d), lambda b, r: (b, jnp.minimum(r, ctx_tiles - 1), 0)),
        pl.BlockSpec((None, ROW_TILE, d), lambda b, r: (b, jnp.maximum(r - ctx_tiles, 0), 0))]


def _mod_kernel(*refs, shift, scale, ctx_tiles):
    *h_refs, p_ref, u_ref = refs
    p = p_ref[...]
    h = _load_h(h_refs, ctx_tiles)
    u_ref[...] = (h * (1.0 + p[scale:scale + 1]) + p[shift:shift + 1]).astype(u_ref.dtype)


def _param_spec(d, seg_of_tile):
    return pl.BlockSpec((None, None, 6, d), lambda b, r: (b, seg_of_tile(r), 0, 0))


def _modulate(h, p, *, shift, scale, n_ctx, n):
    bsz, _, _, d = p.shape
    ctx_tiles = n_ctx // ROW_TILE
    seg = lambda r: jnp.where(r < ctx_tiles, 0, 1)
    h_args, h_specs = _h_operands(h, d, ctx_tiles)
    return pl.pallas_call(
        functools.partial(_mod_kernel, shift=shift, scale=scale, ctx_tiles=ctx_tiles),
        grid=(bsz, n // ROW_TILE),
        in_specs=h_specs + [_param_spec(d, seg)],
        out_specs=pl.BlockSpec((None, ROW_TILE, d), lambda b, r: (b, r, 0)),
        out_shape=jax.ShapeDtypeStruct((bsz, n, d), BF16),
        compiler_params=_cparams("arbitrary", "arbitrary"),
        name="modulate",
    )(*h_args, p)


def _resid_ln_kernel(*refs, alpha, gate, shift, scale, ctx_tiles, n_h):
    h_refs, (y_ref, p_ref, g_ref, b_ref, p2_ref, h_out, *u_out) = refs[:n_h], refs[n_h:]
    p = p_ref[...]
    x = alpha * _load_h(h_refs, ctx_tiles) + p[gate:gate + 1] * y_ref[...].astype(F32)
    mu = jnp.mean(x, axis=-1, keepdims=True)
    xc = x - mu
    var = jnp.mean(xc * xc, axis=-1, keepdims=True)
    hn = xc * lax.rsqrt(var + LN_EPS) * g_ref[...] + b_ref[...]
    h_out[...] = hn
    if u_out:
        p2 = p2_ref[...]
        u_out[0][...] = (hn * (1.0 + p2[scale:scale + 1]) + p2[shift:shift + 1]).astype(BF16)


def _resid_ln(h, y, p, ln_g, ln_b, p2, *, alpha, gate, shift, scale, n_ctx, h_row_offset=0, with_u=True):
    bsz, n, d = y.shape
    ctx_tiles = n_ctx // ROW_TILE
    seg = lambda r: jnp.where(r < ctx_tiles, 0, 1)
    h_args, h_specs = _h_operands(h, d, ctx_tiles, h_row_offset // ROW_TILE)
    row = pl.BlockSpec((None, ROW_TILE, d), lambda b, r: (b, r, 0))
    vec = pl.BlockSpec((1, d), lambda b, r: (0, 0))
    out_shape = [jax.ShapeDtypeStruct((bsz, n, d), F32)]
    out_specs = [row]
    if with_u:
        out_shape.append(jax.ShapeDtypeStruct((bsz, n, d), BF16))
        out_specs.append(row)
    outs = pl.pallas_call(
        functools.partial(_resid_ln_kernel, alpha=alpha, gate=gate, shift=shift, scale=scale,
                          ctx_tiles=ctx_tiles, n_h=len(h_args)),
        grid=(bsz, n // ROW_TILE),
        in_specs=h_specs + [row, _param_spec(d, seg), vec, vec, _param_spec(d, seg)],
        out_specs=out_specs,
        out_shape=out_shape,
        compiler_params=_cparams("arbitrary", "arbitrary"),
        name="resid_ln",
    )(*h_args, y, p, ln_g.reshape(1, d), ln_b.reshape(1, d), p2)
    return outs if with_u else outs[0]


def _mm_kernel(a_ref, w_ref, o_ref, wb_ref):
    @pl.when(pl.program_id(1) == 0)
    def _():
        wb_ref[...] = w_ref[...].astype(BF16)

    o_ref[...] = _dot(a_ref[...], wb_ref[...]).astype(o_ref.dtype)


def _matmul(a, w, layer, *, n_out, col_off=0, tn, tm=MM_ROW_TILE, out_dtype, name):
    m, k = a.shape
    assert m % tm == 0 and n_out % tn == 0 and col_off % tn == 0
    off = col_off // tn
    return pl.pallas_call(
        _mm_kernel,
        grid=(n_out // tn, m // tm),
        in_specs=[pl.BlockSpec((tm, k), lambda j, i: (i, 0)),
                  pl.BlockSpec((None, k, tn), lambda j, i: (layer, 0, j + off))],
        out_specs=pl.BlockSpec((tm, tn), lambda j, i: (i, j)),
        out_shape=jax.ShapeDtypeStruct((m, n_out), out_dtype),
        scratch_shapes=[pltpu.VMEM((k, tn), BF16)],
        compiler_params=_cparams("arbitrary", "arbitrary"),
        name=name,
    )(a, w)


def _swiglu_kernel(a_ref, wg_ref, wu_ref, o_ref, wgb_ref, wub_ref):
    @pl.when(pl.program_id(1) == 0)
    def _():
        wgb_ref[...] = wg_ref[...].astype(BF16)
        wub_ref[...] = wu_ref[...].astype(BF16)

    a = a_ref[...]
    gate = _dot(a, wgb_ref[...])
    up = _dot(a, wub_ref[...])
    o_ref[...] = (jax.nn.silu(gate) * up).astype(o_ref.dtype)


def _swiglu_in(a, w, layer, *, tn=512, tm=MM_ROW_TILE):
    m, k = a.shape
    f = w.shape[2] // 2
    assert m % tm == 0 and f % tn == 0
    nb = f // tn
    return pl.pallas_call(
        _swiglu_kernel,
        grid=(nb, m // tm),
        in_specs=[pl.BlockSpec((tm, k), lambda j, i: (i, 0)),
                  pl.BlockSpec((None, k, tn), lambda j, i: (layer, 0, j)),
                  pl.BlockSpec((None, k, tn), lambda j, i: (layer, 0, j + nb))],
        out_specs=pl.BlockSpec((tm, tn), lambda j, i: (i, j)),
        out_shape=jax.ShapeDtypeStruct((m, f), BF16),
        scratch_shapes=[pltpu.VMEM((k, tn), BF16), pltpu.VMEM((k, tn), BF16)],
        compiler_params=_cparams("arbitrary", "arbitrary"),
        name="ffn_in_swiglu",
    )(a, w, w)


def _ffn(u, w_in, w_out, layer):
    bsz, n, d = u.shape
    a = _swiglu_in(u.reshape(bsz * n, d), w_in, layer)
    y = _matmul(a, w_out, layer, n_out=d, tn=512, tm=512, out_dtype=F32, name="ffn_out")
    return y.reshape(bsz, n, d)


def _ret_kernel(draw_ref, q_ref, k_ref, v_ref, g_ref, cos_ref, sin_ref, o_ref,
                qs_ref, ks_ref, oacc_ref, sf_ref, sb_ref, *, n_chunks, ctx_chunks, skip_chunks, dk):
    C = RET_CHUNK
    h = pl.program_id(1)
    half = dk // 2
    lg_f = -jnp.exp(jnp.full((1, 1), draw_ref[0, h], F32))
    lg_b = -jnp.exp(jnp.full((1, 1), draw_ref[1, h], F32))

    ri = lax.broadcasted_iota(jnp.int32, (C, C), 0)
    ci = lax.broadcasted_iota(jnp.int32, (C, C), 1)
    diff = (ri - ci).astype(F32)
    lower = diff >= 0
    dmat = jnp.where(lower, jnp.exp(lg_f * jnp.where(lower, diff, 0.0)),
                     jnp.exp(lg_b * jnp.where(lower, 0.0, -diff)))
    rc = lax.broadcasted_iota(jnp.int32, (C, 1), 0).astype(F32)
    qdec_f = jnp.exp(lg_f * (rc + 1.0))
    kdec_f = jnp.exp(lg_f * (C - 1.0 - rc))
    cdec_f = jnp.exp(lg_f * C)
    qdec_b = jnp.exp(lg_b * (C - rc))
    kdec_b = jnp.exp(lg_b * rc)
    cdec_b = jnp.exp(lg_b * C)

    sf_ref[...] = jnp.zeros_like(sf_ref)
    sb_ref[...] = jnp.zeros_like(sb_ref)
    qscale = dk ** -0.5

    def rope(x, cs, sn):
        x1, x2 = x[:, :half], x[:, half:]
        return jnp.concatenate([x1 * cs - x2 * sn, x1 * sn + x2 * cs], axis=-1)

    def fwd(c, carry):
        rows = pl.ds(pl.multiple_of(c * C, C), C)
        cs, sn = cos_ref[rows, :], sin_ref[rows, :]
        q = rope(q_ref[rows, :].astype(F32), cs, sn) * qscale
        k = rope(k_ref[rows, :].astype(F32), cs, sn)
        qb, kb = q.astype(BF16), k.astype(BF16)
        qs_ref[rows, :] = qb
        ks_ref[rows, :] = kb
        v = v_ref[rows, :]
        s = (_dot_nt(qb, kb) * dmat).astype(BF16)
        state = sf_ref[...]
        oacc_ref[rows, :] = _dot(s, v) + _dot(qb, state.astype(BF16)) * qdec_f
        kd_t = jnp.transpose(k * kdec_f).astype(BF16)
        sf_ref[...] = cdec_f * state + _dot(kd_t, v)
        return carry

    lax.fori_loop(0, n_chunks, fwd, 0, unroll=RET_UNROLL)

    def bwd(t, carry):
        c = jnp.where(t < ctx_chunks, ctx_chunks - 1 - t, n_chunks - 1 + ctx_chunks - t)
        rows = pl.ds(pl.multiple_of(c * C, C), C)
        qb, kb, v = qs_ref[rows, :], ks_ref[rows, :], v_ref[rows, :]
        state = sb_ref[...]
        o = oacc_ref[rows, :] + _dot(qb, state.astype(BF16)) * qdec_b
        kd_t = jnp.transpose(kb.astype(F32) * kdec_b).astype(BF16)
        sb_ref[...] = cdec_b * state + _dot(kd_t, v)

        def emit():
            mu = jnp.mean(o, axis=-1, keepdims=True)
            oc = o - mu
            var = jnp.mean(oc * oc, axis=-1, keepdims=True)
            on = oc * lax.rsqrt(var + LN_EPS)
            out_rows = pl.ds(pl.multiple_of((c - skip_chunks) * C, C), C)
            o_ref[out_rows, :] = (on * jax.nn.silu(g_ref[rows, :].astype(F32))).astype(o_ref.dtype)

        if skip_chunks:
            pl.when(c >= skip_chunks)(emit)
        else:
            emit()
        return carry

    lax.fori_loop(0, n_chunks, bwd, 0, unroll=RET_UNROLL)


def _retention(qkvg, decay_raw, cos, sin, *, n_ctx, need_ctx):
    bsz, n, width = qkvg.shape
    H = RET_HEADS
    dk = width // (6 * H)
    dv = 2 * dk
    C = RET_CHUNK
    assert n % C == 0 and n_ctx % C == 0
    skip = 0 if need_ctx else n_ctx // C
    n_out = n - skip * C
    kern = functools.partial(_ret_kernel, n_chunks=n // C, ctx_chunks=n_ctx // C, skip_chunks=skip, dk=dk)
    return pl.pallas_call(
        kern,
        grid=(bsz, H),
        in_specs=[
            pl.BlockSpec(memory_space=pltpu.SMEM),
            pl.BlockSpec((None, n, dk), lambda b, h: (b, 0, h)),
            pl.BlockSpec((None, n, dk), lambda b, h: (b, 0, H + h)),
            pl.BlockSpec((None, n, dv), lambda b, h: (b, 0, H + h)),
            pl.BlockSpec((None, n, dv), lambda b, h: (b, 0, 2 * H + h)),
            pl.BlockSpec((n, dk // 2), lambda b, h: (0, 0)),
            pl.BlockSpec((n, dk // 2), lambda b, h: (0, 0)),
        ],
        out_specs=pl.BlockSpec((None, n_out, dv), lambda b, h: (b, 0, h)),
        out_shape=jax.ShapeDtypeStruct((bsz, n_out, H * dv), BF16),
        scratch_shapes=[pltpu.VMEM((n, dk), BF16), pltpu.VMEM((n, dk), BF16), pltpu.VMEM((n, dv), F32),
                        pltpu.VMEM((dk, dv), F32), pltpu.VMEM((dk, dv), F32)],
        compiler_params=_cparams("arbitrary", "arbitrary"),
        name="retention",
    )(decay_raw.astype(F32), qkvg, qkvg, qkvg, qkvg, cos, sin)


def _rope_tables(n_ctx, n_lat, dim):
    rows = n_lat // GRID_W
    row = jnp.repeat(jnp.arange(rows, dtype=F32), GRID_W)
    col = jnp.tile(jnp.arange(GRID_W, dtype=F32), rows)
    n_freq = dim // 4
    inv_freq = ROPE_BASE ** (-jnp.arange(n_freq, dtype=F32) / n_freq)
    ang = jnp.concatenate([row[:, None] * inv_freq, col[:, None] * inv_freq], axis=-1)
    cos = jnp.concatenate([jnp.ones((n_ctx, dim // 2), F32), jnp.cos(ang)], axis=0)
    sin = jnp.concatenate([jnp.zeros((n_ctx, dim // 2), F32), jnp.sin(ang)], axis=0)
    return cos, sin


def _conv_kernel(x_ref, w_ref, o_ref, *, n_ctx, n_q_blocks, n_qk_blocks, qscale):
    j = pl.program_id(1)
    x = x_ref[...].astype(F32)
    n, tc = x.shape
    w = w_ref[...]
    pad = GDN_CONV // 2
    t = lax.broadcasted_iota(jnp.int32, (n, 1), 0)
    lo = jnp.where(t < n_ctx, 0, n_ctx)
    hi = jnp.where(t < n_ctx, n_ctx, n)
    acc = x * w[pad:pad + 1]
    for d in range(-pad, pad + 1):
        if d == 0:
            continue
        xs = pltpu.roll(x, (n - d) % n, 0)
        valid = (t + d >= lo) & (t + d < hi)
        acc = acc + jnp.where(valid, xs, 0.0) * w[d + pad:d + pad + 1]
    y = jax.nn.silu(acc)
    do_norm = j < n_qk_blocks
    scale = jnp.where(j < n_q_blocks, qscale, 1.0).astype(F32)
    for g0 in range(0, tc, GDN_DK):
        yg = y[:, g0:g0 + GDN_DK]
        ss = jnp.sum(yg * yg, axis=-1, keepdims=True)
        yn = yg * (lax.rsqrt(ss + 1e-6) * scale)
        o_ref[:, g0:g0 + GDN_DK] = jnp.where(do_norm, yn, yg).astype(o_ref.dtype)


def _gdn_conv(proj, conv_w, *, n_ctx, n_qk_heads, tc=512):
    bsz, n, _ = proj.shape
    width = conv_w.shape[1]
    qw = n_qk_heads * GDN_DK
    assert qw % tc == 0 and width % tc == 0
    kern = functools.partial(_conv_kernel, n_ctx=n_ctx, n_q_blocks=qw // tc, n_qk_blocks=2 * qw // tc,
                             qscale=GDN_DK ** -0.5)
    return pl.pallas_call(
        kern,
        grid=(bsz, width // tc),
        in_specs=[pl.BlockSpec((None, n, tc), lambda b, j: (b, 0, j)),
                  pl.BlockSpec((GDN_CONV, tc), lambda b, j: (0, j))],
        out_specs=pl.BlockSpec((None, n, tc), lambda b, j: (b, 0, j)),
        out_shape=jax.ShapeDtypeStruct((bsz, n, width), BF16),
        compiler_params=_cparams("arbitrary", "arbitrary"),
        name="gdn_conv",
    )(proj, conv_w)


def _gates_kernel(x_ref, alog_ref, dt_ref, col_ref, row_ref):
    C = GDN_CHUNK
    S = GDN_INST
    x = x_ref[...]
    n = x.shape[0]
    beta = jax.nn.sigmoid(x)
    z = x + dt_ref[...]
    softplus = jnp.maximum(z, 0.0) + jnp.log1p(jnp.exp(-jnp.abs(z)))
    la = -jnp.exp(alog_ref[...]) * softplus
    tm = lax.broadcasted_iota(jnp.int32, (n, 1), 0) % C
    pre = la
    suf = la
    s = 1
    while s < C:
        pre = pre + jnp.where(tm >= s, pltpu.roll(pre, s, 0), 0.0)
        suf = suf + jnp.where(tm < C - s, pltpu.roll(suf, n - s, 0), 0.0)
        s *= 2
    lane = lax.broadcasted_iota(jnp.int32, (1, x.shape[1]), 1) % (2 * S)
    is_fwd = (lane % S) < S // 2
    col_ref[...] = jnp.where(lane < S, beta, jnp.where(is_fwd, pre, suf))

    def xpose(p, carry):
        t = jnp.transpose(col_ref[pl.ds(pl.multiple_of(p * 2 * C, 2 * C), 2 * C), :])
        row_ref[2 * p] = t[:, :C]
        row_ref[2 * p + 1] = t[:, C:]
        return carry

    lax.fori_loop(0, n // (2 * C), xpose, 0)


def _gate_lane_order(n_qk_heads):
    idx = jnp.arange(8 * n_qk_heads).reshape(2, 2, n_qk_heads, 2)
    return idx.transpose(2, 0, 1, 3).reshape(-1)


def _gdn_gates(ba, a_log, dt_bias):
    bsz, n, w = ba.shape
    C = GDN_CHUNK
    hq = w // (2 * GDN_INST)

    def per_lane(p):
        p = p.astype(F32).reshape(2, hq, 2).transpose(1, 0, 2).reshape(hq, GDN_INST)
        return jnp.concatenate([jnp.zeros_like(p), p], axis=1).reshape(1, w)

    return pl.pallas_call(
        _gates_kernel,
        grid=(bsz,),
        in_specs=[pl.BlockSpec((None, n, w), lambda b: (b, 0, 0)),
                  pl.BlockSpec((1, w), lambda b: (0, 0)), pl.BlockSpec((1, w), lambda b: (0, 0))],
        out_specs=[pl.BlockSpec((None, n, w), lambda b: (b, 0, 0)),
                   pl.BlockSpec((None, n // C, w, C), lambda b: (b, 0, 0, 0))],
        out_shape=[jax.ShapeDtypeStruct((bsz, n, w), F32), jax.ShapeDtypeStruct((bsz, n // C, w, C), F32)],
        compiler_params=_cparams("arbitrary"),
        name="gdn_gates",
    )(ba, per_lane(a_log), per_lane(dt_bias))


GDN_INST = 4
GDN_PREP_CHUNKS = 6
GDN_EMIT_ROWS = 256
GDN_HEADS_PER_STEP = 2


GDN_INV_BASE = 8
GDN_INV_LEVELS = (16, 32, 64)


def _merge_rows(g, lower):
    h = g // 2
    return [(r0 + h, r0 + g) if lower else (r0, r0 + h) for r0 in range(0, GDN_CHUNK, g)]


def _unit_tri_inverse(a_list, lower_list, same_base, off_masks, eye, between):
    def mm(x, y):
        return _dot(x.astype(BF16), y.astype(BF16))

    ad = [a * same_base for a in a_list]
    x = [eye - d for d in ad]
    p = [mm(d, d) for d in ad]
    between()
    xp = [mm(xi, pi) for xi, pi in zip(x, p)]
    p = [mm(pi, pi) for pi in p]
    between()
    x = [xi + xpi for xi, xpi in zip(x, xp)]
    x = [xi + mm(xi, pi) for xi, pi in zip(x, p)]
    between()
    for g, off in zip(GDN_INV_LEVELS, off_masks):
        h = g // 2
        sels = [_merge_rows(g, lower) for lower in lower_list]
        top = [jnp.concatenate([xi[a:b] for a, b in sel], axis=0) for xi, sel in zip(x, sels)]
        n = [mm(ti, a * off) for ti, a in zip(top, a_list)]
        between()
        upd = [ti - mm(ni, xi) for ti, ni, xi in zip(top, n, x)]
        between()
        merged = []
        for xi, ui, sel in zip(x, upd, sels):
            pieces = []
            for idx, (a, b) in enumerate(sel):
                new = ui[idx * h:(idx + 1) * h]
                r0 = (a // g) * g
                pieces += [xi[r0:r0 + h], new] if a != r0 else [new, xi[r0 + h:r0 + g]]
            merged.append(jnp.concatenate(pieces, axis=0))
        x = merged
    return x


def _gdn_kernel(q_ref, k_ref, v_ref, z_ref, gcol_ref, grow_ref, nw_ref, o_ref,
                mask_ref, gsel_ref, u0_ref, wq_ref, qk_ref, of_ref, ob_ref, m_ref,
                *, n_chunks, ctx_chunks, skip_chunks):
    C = GDN_CHUNK
    S = GDN_INST
    dk = GDN_DK
    dv = GDN_DV
    n_lvl = len(GDN_INV_LEVELS)
    M_INCL, M_STRICT, M_BASE, M_OFF, M_EYE = 0, 2, 4, 5, 5 + n_lvl

    r = lax.broadcasted_iota(jnp.int32, (C, C), 0)
    c = lax.broadcasted_iota(jnp.int32, (C, C), 1)
    mask_ref[M_INCL] = (r >= c).astype(F32)
    mask_ref[M_INCL + 1] = (c >= r).astype(F32)
    mask_ref[M_STRICT] = (r > c).astype(F32)
    mask_ref[M_STRICT + 1] = (c > r).astype(F32)
    mask_ref[M_BASE] = ((r // GDN_INV_BASE) == (c // GDN_INV_BASE)).astype(F32)
    for lvl, g in enumerate(GDN_INV_LEVELS):
        h = g // 2
        mask_ref[M_OFF + lvl] = (((r // g) == (c // g)) & ((r // h) != (c // h))).astype(F32)
    mask_ref[M_EYE] = (r == c).astype(F32)

    P = GDN_HEADS_PER_STEP

    def lanes(start):
        return pl.ds(pl.multiple_of(start, dk), dk)

    gate_lanes = gcol_ref.shape[-1]

    def head_roll(hh):
        first = (pl.program_id(1) * P + hh) * 2 * S
        return (gate_lanes - first) % gate_lanes

    def head_rows(hh):
        return pl.ds(pl.multiple_of(hh * 2 * S, 2 * S), 2 * S)

    def select_gates(i, carry):
        rows = pl.ds(pl.multiple_of(i * GDN_EMIT_ROWS, GDN_EMIT_ROWS), GDN_EMIT_ROWS)
        g = gcol_ref[rows, :]
        for hh in range(P):
            gsel_ref[hh, rows, :] = pltpu.roll(g, head_roll(hh), 1)
        return carry

    lax.fori_loop(0, n_chunks * C // GDN_EMIT_ROWS, select_gates, 0, unroll=3)

    NB = GDN_PREP_CHUNKS
    n_blocks = n_chunks // NB

    def scan_chunks(t):
        cb = jnp.where(t < ctx_chunks, ctx_chunks - 1 - t, n_chunks - 1 + ctx_chunks - t)
        return t, cb

    def prep_block(g, buf, between):
        hh, j = g // n_blocks, g % n_blocks
        between()
        kts = {}
        for i in range(0, NB, 2):
            for d in range(2):
                lo = scan_chunks(j * NB + i + d)[d]
                slab = pl.ds(pl.multiple_of(lo * C, 2 * C), 2 * C)
                kt2 = jnp.transpose(k_ref[slab, lanes(hh * dk)].astype(F32))
                kts[(i + d, d)], kts[(i + 1 - d, d)] = kt2[:, :C], kt2[:, C:]
        items = []
        for i in range(NB):
            for d, ci in enumerate(scan_chunks(j * NB + i)):
                rows = pl.ds(pl.multiple_of(ci * C, C), C)
                qb, kb = q_ref[rows, lanes(hh * dk)], k_ref[rows, lanes(hh * dk)]
                pk = _dot_nt(jnp.concatenate([kb, qb], axis=0), kb)
                kk, qk = pk[:C], pk[C:]
                g8 = gsel_ref[hh, rows, :]
                grow8 = grow_ref[ci, head_rows(hh), :]
                qf, kf = qb.astype(F32), kb.astype(F32)
                kt = kts[(i, d)]
                for vj in range(2):
                    s = 2 * d + vj
                    vf = v_ref[rows, lanes((2 * hh + vj) * dv)].astype(F32)
                    beta, gc = g8[:, s:s + 1], g8[:, S + s:S + s + 1]
                    g_row = grow8[S + s:S + s + 1, :]
                    e = jnp.exp(jnp.minimum(gc - g_row, 0.0))
                    a = beta * kk * (e * mask_ref[M_STRICT + d])
                    qkm = (qk * (e * mask_ref[M_INCL + d])).astype(BF16)
                    eg = jnp.exp(gc)
                    rhs = jnp.concatenate([vf * beta, kf * (beta * eg)], axis=-1)
                    g_last = g_row[:, 0:1] if d else g_row[:, C - 1:C]
                    kdt = (kt * jnp.exp(g_last - g_row)).astype(BF16)
                    items.append(dict(slot=i, s=s, a=a, qkm=qkm, rhs=rhs.astype(BF16), qd=qf * eg, kdt=kdt))
        between()

        xs = _unit_tri_inverse([it["a"] for it in items], [it["s"] < 2 for it in items], mask_ref[M_BASE],
                               [mask_ref[M_OFF + i] for i in range(n_lvl)], mask_ref[M_EYE], between)
        sols = [_dot(x.astype(BF16), it["rhs"]) for x, it in zip(xs, items)]
        between()
        for it, sol in zip(items, sols):
            s, slot = it["s"], it["slot"]
            u0_ref[buf, s, slot] = sol[:, :dv]
            wq_ref[buf, s, slot, 0:C, :] = sol[:, dv:].astype(BF16)
            wq_ref[buf, s, slot, C:2 * C, :] = it["qd"].astype(BF16)
            qk_ref[buf, s, slot, 0:C, :] = it["qkm"]
            qk_ref[buf, s, slot, C:C + dk, :] = it["kdt"]

    def scan_first(g, buf, i, st):
        j = g % n_blocks
        ms = [m_ref[s] for s in range(S)]
        if i == 0:
            ms = [jnp.where(j == 0, 0.0, m) for m in ms]
        st["ms"] = ms
        st["wm"] = [_dot(wq_ref[buf, s, i], ms[s].astype(BF16)) for s in range(S)]

    def scan_second(g, buf, i, st):
        hh, j = g // n_blocks, g % n_blocks
        cf, cb = scan_chunks(j * NB + i)
        cis = [cf, cf, cb, cb]
        ms, wm = st["ms"], st["wm"]
        ub = [(u0_ref[buf, s, i] - wm[s][:C]).astype(BF16) for s in range(S)]
        om = [_dot(qk_ref[buf, s, i], ub[s]) for s in range(S)]
        for s in range(S):
            ci = cis[s]
            g_s = grow_ref[ci, head_rows(hh), :][S + s:S + s + 1, :]
            g_last = g_s[:, C - 1:C] if s < 2 else g_s[:, 0:1]
            m_ref[s] = jnp.exp(g_last) * ms[s] + om[s][C:]
            o = wm[s][C:] + om[s][:C]
            rows = pl.ds(pl.multiple_of(ci * C, C), C)
            if s < 2:
                of_ref[hh, s, rows, :] = o
            else:
                ob_ref[hh, s - 2, rows, :] = o

    def scan_thunks(g, buf):
        thunks = []
        for i in range(NB):
            st = {}
            thunks += [functools.partial(scan_first, g, buf, i, st), functools.partial(scan_second, g, buf, i, st)]
        return thunks

    m_ref[...] = jnp.zeros_like(m_ref)
    prep_block(0, 0, lambda: None)

    def pipelined(g, carry):
        buf = g % 2
        pending = scan_thunks(g, buf)

        def between():
            if pending:
                pending.pop(0)()

        prep_block(g + 1, 1 - buf, between)
        while pending:
            pending.pop(0)()
        return carry

    nw = nw_ref[...]
    E = GDN_EMIT_ROWS
    emit_blocks = range(skip_chunks * C // E, n_chunks * C // E)

    def emit(hh, i):
        aligned = (lambda s: s) if isinstance(i, int) else (lambda s: pl.multiple_of(s, E))
        rows = pl.ds(aligned(i * E), E)
        out_rows = pl.ds(aligned(i * E - skip_chunks * C), E)
        for vj in range(2):
            cols = slice((2 * hh + vj) * dv, (2 * hh + vj + 1) * dv)
            o = of_ref[hh, vj, rows, :] + ob_ref[hh, vj, rows, :]
            zf = z_ref[rows, cols].astype(F32)
            on = o * lax.rsqrt(jnp.mean(o * o, axis=-1, keepdims=True) + 1e-6) * nw * jax.nn.silu(zf)
            o_ref[out_rows, cols] = on.astype(o_ref.dtype)

    last = P * n_blocks - 1
    lax.fori_loop(0, last, pipelined, 0)

    done = [functools.partial(emit, hh, i) for hh in range(P - 1) for i in emit_blocks]
    tail = scan_thunks(last, last % 2)
    for k, thunk in enumerate(tail):
        thunk()
        for _ in range(-(-len(done) // (len(tail) - k))):
            done.pop(0)()

    def emit_last(i, carry):
        emit(P - 1, i)
        return carry

    lax.fori_loop(emit_blocks.start, emit_blocks.stop, emit_last, 0)


def _gdn_scan(qkv, proj, gcol, grow, norm_w, *, n_ctx, need_ctx, n_qk_heads, z_col_off):
    bsz, n, _ = qkv.shape
    C = GDN_CHUNK
    P = GDN_HEADS_PER_STEP
    HQ = n_qk_heads
    HV = 2 * HQ
    gw = gcol.shape[-1]
    assert gw == HQ * 2 * GDN_INST and HQ % P == 0 and n % (GDN_PREP_CHUNKS * C) == 0 and n_ctx % C == 0
    assert n % GDN_EMIT_ROWS == 0 and n_ctx % GDN_EMIT_ROWS == 0
    assert (n // C) % 2 == 0 and (n_ctx // C) % 2 == 0 and GDN_PREP_CHUNKS % 2 == 0
    nch = n // C
    skip = 0 if need_ctx else n_ctx // C
    n_out = n - skip * C
    qk_w = P * GDN_DK
    v_w = P * 2 * GDN_DV
    assert z_col_off % v_w == 0
    k_blk, v_blk, z_blk = HQ * GDN_DK // qk_w, 2 * HQ * GDN_DK // v_w, z_col_off // v_w
    kern = functools.partial(_gdn_kernel, n_chunks=nch, ctx_chunks=n_ctx // C, skip_chunks=skip)
    return pl.pallas_call(
        kern,
        grid=(bsz, HQ // P),
        in_specs=[
            pl.BlockSpec((None, n, qk_w), lambda b, h: (b, 0, h)),
            pl.BlockSpec((None, n, qk_w), lambda b, h: (b, 0, k_blk + h)),
            pl.BlockSpec((None, n, v_w), lambda b, h: (b, 0, v_blk + h)),
            pl.BlockSpec((None, n, v_w), lambda b, h: (b, 0, z_blk + h)),
            pl.BlockSpec((None, n, gw), lambda b, h: (b, 0, 0)),
            pl.BlockSpec((None, nch, P * 2 * GDN_INST, C), lambda b, h: (b, 0, h, 0)),
            pl.BlockSpec((1, GDN_DV), lambda b, h: (0, 0)),
        ],
        out_specs=pl.BlockSpec((None, n_out, v_w), lambda b, h: (b, 0, h)),
        out_shape=jax.ShapeDtypeStruct((bsz, n_out, HV * GDN_DV), BF16),
        scratch_shapes=[pltpu.VMEM((6 + len(GDN_INV_LEVELS), C, C), F32),
                        pltpu.VMEM((P, n, gw), F32),
                        pltpu.VMEM((2, GDN_INST, GDN_PREP_CHUNKS, C, GDN_DV), F32),
                        pltpu.VMEM((2, GDN_INST, GDN_PREP_CHUNKS, 2 * C, GDN_DK), BF16),
                        pltpu.VMEM((2, GDN_INST, GDN_PREP_CHUNKS, C + GDN_DK, C), BF16),
                        pltpu.VMEM((P, 2, n, GDN_DV), F32), pltpu.VMEM((P, 2, n, GDN_DV), F32),
                        pltpu.VMEM((GDN_INST, GDN_DK, GDN_DV), F32)],
        compiler_params=_cparams("arbitrary", "arbitrary"),
        name="gdn_scan",
    )(qkv, qkv, qkv, proj, gcol, grow, norm_w.astype(F32).reshape(1, GDN_DV))


def kernel(x, c, ctx, c_ctx, ada_w, ada_b, ln_g, ln_b, ret_w_in, ret_decay, ret_w_out,
           gdn_w_in, gdn_conv, gdn_a_log, gdn_dt_bias, gdn_norm, gdn_w_out, ffn_w_in, ffn_w_out):
    bsz, n_lat, d = x.shape
    n_ctx = ctx.shape[1]
    depth = ada_w.shape[0]
    n_tot = n_ctx + n_lat
    alpha = (2 * depth) ** 0.25

    pad_rows = (-(bsz + 1)) % 16
    cc = jnp.concatenate([c, c_ctx[None], jnp.zeros((pad_rows, d), F32)], axis=0)
    mods = _ada_mod(cc, ada_w, ada_b)

    def layer_params(i):
        lat = mods[i, :bsz].reshape(bsz, 1, 6, d)
        cx = jnp.broadcast_to(mods[i, bsz].reshape(1, 1, 6, d), (bsz, 1, 6, d))
        return jnp.concatenate([cx, lat], axis=1)

    params = [layer_params(i) for i in range(depth)]
    h = (ctx, x)
    u = _modulate(h, params[0], shift=0, scale=1, n_ctx=n_ctx, n=n_tot)
    ctx_rows = n_ctx

    for i in range(depth):
        need_ctx = i < depth - 1
        j = i // 2
        u2 = u.reshape(bsz * n_tot, d)
        if i % 2 == 0:
            qkvg = _matmul(u2, ret_w_in, j, n_out=ret_w_in.shape[-1], tn=1024, tm=_row_tile(bsz * n_tot),
                           out_dtype=BF16, name="ret_in_proj").reshape(bsz, n_tot, -1)
            dk = qkvg.shape[-1] // (6 * RET_HEADS)
            cos, sin = _rope_tables(n_ctx, n_lat, dk)
            o = _retention(qkvg, ret_decay[j], cos, sin, n_ctx=n_ctx, need_ctx=need_ctx)
            w_out = ret_w_out
        else:
            qkv_w = gdn_conv.shape[-1]
            hv = gdn_a_log.shape[-1]
            main_w = qkv_w + hv * GDN_DV
            proj = _matmul(u2, gdn_w_in, j, n_out=main_w, tn=1024, tm=_row_tile(bsz * n_tot), out_dtype=BF16,
                           name="gdn_in_proj").reshape(bsz, n_tot, main_w)
            w_gate = jnp.take(gdn_w_in[j, :, main_w:], _gate_lane_order(hv // 2), axis=1)[None]
            ba = _matmul(u2, w_gate, 0, n_out=4 * hv, tn=4 * hv, out_dtype=F32,
                         name="gdn_gate_proj").reshape(bsz, n_tot, 4 * hv)
            gcol, grow = _gdn_gates(ba, gdn_a_log[j], gdn_dt_bias[j])
            qkv = _gdn_conv(proj, gdn_conv[j], n_ctx=n_ctx, n_qk_heads=hv // 2)
            o = _gdn_scan(qkv, proj, gcol, grow, gdn_norm[j], n_ctx=n_ctx, need_ctx=need_ctx,
                          n_qk_heads=hv // 2, z_col_off=qkv_w)
            w_out = gdn_w_out

        n_rows = o.shape[1]
        out_ctx = n_ctx if need_ctx else 0
        y = _matmul(o.reshape(bsz * n_rows, -1), w_out, j, n_out=d, tn=512, out_dtype=F32,
                    name="mixer_out_proj").reshape(bsz, n_rows, d)
        h_off = ctx_rows - out_ctx
        if isinstance(h, tuple) and h_off:
            h, h_off = h[1], 0
        h, u = _resid_ln(h, y, params[i], ln_g[i, 0], ln_b[i, 0], params[i], alpha=alpha, gate=2,
                         shift=3, scale=4, n_ctx=out_ctx, h_row_offset=h_off)
        ctx_rows = out_ctx
        y = _ffn(u, ffn_w_in, ffn_w_out, i)
        if need_ctx:
            h, u = _resid_ln(h, y, params[i], ln_g[i, 1], ln_b[i, 1], params[i + 1], alpha=alpha, gate=5,
                             shift=0, scale=1, n_ctx=out_ctx)
        else:
            h = _resid_ln(h, y, params[i], ln_g[i, 1], ln_b[i, 1], params[i], alpha=alpha, gate=5,
                          shift=0, scale=1, n_ctx=out_ctx, with_u=False)
    return h
```

```python
import functools
import math

import jax
import jax.numpy as jnp
from jax import lax
from jax.experimental import pallas as pl
from jax.experimental.pallas import tpu as pltpu

F32 = jnp.float32
BF16 = jnp.bfloat16

GRID_W = 64
RET_HEADS = 8
RET_CHUNK = 256
RET_UNROLL = 3
ROPE_BASE = 10000.0
GDN_DK = 128
GDN_DV = 128
GDN_CONV = 5
GDN_CHUNK = 64
LN_EPS = 1e-5

VMEM_LIMIT_BYTES = 56 * 1024 * 1024
ROW_TILE = 256
LN_STRIP_ROWS = 16
MM_ROW_TILE = 1024
MM_ROW_TILE_CAP = 1536


def _row_tile(m):
    return max(t for t in range(ROW_TILE, MM_ROW_TILE_CAP + 1, ROW_TILE) if m % t == 0)


def _cparams(*sem):
    return pltpu.CompilerParams(dimension_semantics=sem, vmem_limit_bytes=VMEM_LIMIT_BYTES)


def _dot(a, b):
    return jnp.dot(a, b, preferred_element_type=F32)


def _dot_nt(a, b):
    return lax.dot_general(a, b, (((1,), (1,)), ((), ())), preferred_element_type=F32)


def _ada_kernel(c_ref, w_ref, b_ref, o_ref):
    a = jax.nn.silu(c_ref[...]).astype(BF16)
    o_ref[...] = _dot(a, w_ref[...].astype(BF16)) + b_ref[...]


def _ada_mod(cc, ada_w, ada_b, tn=1024):
    depth, d, n = ada_w.shape
    rows = cc.shape[0]
    return pl.pallas_call(
        _ada_kernel,
        grid=(depth, n // tn),
        in_specs=[
            pl.BlockSpec((rows, d), lambda l, j: (0, 0)),
            pl.BlockSpec((None, d, tn), lambda l, j: (l, 0, j)),
            pl.BlockSpec((None, 1, tn), lambda l, j: (l, 0, j)),
        ],
        out_specs=pl.BlockSpec((None, rows, tn), lambda l, j: (l, 0, j)),
        out_shape=jax.ShapeDtypeStruct((depth, rows, n), F32),
        compiler_params=_cparams("arbitrary", "arbitrary"),
        name="ada_mod",
    )(cc, ada_w, ada_b.reshape(depth, 1, n))


def _load_h(h_refs, ctx_tiles, rows=slice(None)):
    if len(h_refs) == 1:
        return h_refs[0][rows, :]
    return jnp.where(pl.program_id(1) < ctx_tiles, h_refs[0][rows, :], h_refs[1][rows, :])


def _h_operands(h, d, ctx_tiles, off=0):
    if not isinstance(h, tuple):
        return [h], [pl.BlockSpec((None, ROW_TILE, d), lambda b, r: (b, r + off, 0))]
    assert off == 0
    return list(h), [
        pl.BlockSpec((None, ROW_TILE, d), lambda b, r: (b, jnp.minimum(r, ctx_tiles - 1), 0)),
        pl.BlockSpec((None, ROW_TILE, d), lambda b, r: (b, jnp.maximum(r - ctx_tiles, 0), 0))]


def _mod_kernel(*refs, shift, scale, ctx_tiles):
    *h_refs, p_ref, u_ref = refs
    p = p_ref[...]
    h = _load_h(h_refs, ctx_tiles)
    u_ref[...] = (h * (1.0 + p[scale:scale + 1]) + p[shift:shift + 1]).astype(u_ref.dtype)


def _param_spec(d, seg_of_tile):
    return pl.BlockSpec((None, None, 6, d), lambda b, r: (b, seg_of_tile(r), 0, 0))


def _modulate(h, p, *, shift, scale, n_ctx, n):
    bsz, _, _, d = p.shape
    ctx_tiles = n_ctx // ROW_TILE
    seg = lambda r: jnp.where(r < ctx_tiles, 0, 1)
    h_args, h_specs = _h_operands(h, d, ctx_tiles)
    return pl.pallas_call(
        functools.partial(_mod_kernel, shift=shift, scale=scale, ctx_tiles=ctx_tiles),
        grid=(bsz, n // ROW_TILE),
        in_specs=h_specs + [_param_spec(d, seg)],
        out_specs=pl.BlockSpec((None, ROW_TILE, d), lambda b, r: (b, r, 0)),
        out_shape=jax.ShapeDtypeStruct((bsz, n, d), BF16),
        compiler_params=_cparams("arbitrary", "arbitrary"),
        name="modulate",
    )(*h_args, p)


def _resid_ln_kernel(*refs, alpha, gate, shift, scale, ctx_tiles, n_h):
    h_refs, (y_ref, p_ref, g_ref, b_ref, p2_ref, h_out, *u_out) = refs[:n_h], refs[n_h:]
    p = p_ref[...]
    p2 = p2_ref[...]
    ln_g, ln_b = g_ref[...], b_ref[...]

    def strip(i, carry):
        rows = pl.ds(pl.multiple_of(i * LN_STRIP_ROWS, LN_STRIP_ROWS), LN_STRIP_ROWS)
        x = alpha * _load_h(h_refs, ctx_tiles, rows) + p[gate:gate + 1] * y_ref[rows, :].astype(F32)
        mu = jnp.mean(x, axis=-1, keepdims=True)
        xc = x - mu
        var = jnp.mean(xc * xc, axis=-1, keepdims=True)
        hn = xc * lax.rsqrt(var + LN_EPS) * ln_g + ln_b
        h_out[rows, :] = hn
        if u_out:
            u_out[0][rows, :] = (hn * (1.0 + p2[scale:scale + 1]) + p2[shift:shift + 1]).astype(BF16)
        return carry

    lax.fori_loop(0, ROW_TILE // LN_STRIP_ROWS, strip, 0, unroll=8)


def _resid_ln(h, y, p, ln_g, ln_b, p2, *, alpha, gate, shift, scale, n_ctx, h_row_offset=0, with_u=True):
    bsz, n, d = y.shape
    ctx_tiles = n_ctx // ROW_TILE
    seg = lambda r: jnp.where(r < ctx_tiles, 0, 1)
    h_args, h_specs = _h_operands(h, d, ctx_tiles, h_row_offset // ROW_TILE)
    row = pl.BlockSpec((None, ROW_TILE, d), lambda b, r: (b, r, 0))
    vec = pl.BlockSpec((1, d), lambda b, r: (0, 0))
    out_shape = [jax.ShapeDtypeStruct((bsz, n, d), F32)]
    out_specs = [row]
    if with_u:
        out_shape.append(jax.ShapeDtypeStruct((bsz, n, d), BF16))
        out_specs.append(row)
    outs = pl.pallas_call(
        functools.partial(_resid_ln_kernel, alpha=alpha, gate=gate, shift=shift, scale=scale,
                          ctx_tiles=ctx_tiles, n_h=len(h_args)),
        grid=(bsz, n // ROW_TILE),
        in_specs=h_specs + [row, _param_spec(d, seg), vec, vec, _param_spec(d, seg)],
        out_specs=out_specs,
        out_shape=out_shape,
        compiler_params=_cparams("arbitrary", "arbitrary"),
        name="resid_ln",
    )(*h_args, y, p, ln_g.reshape(1, d), ln_b.reshape(1, d), p2)
    return outs if with_u else outs[0]


def _mm_kernel(a_ref, w_ref, o_ref, wb_ref):
    @pl.when(pl.program_id(1) == 0)
    def _():
        wb_ref[...] = w_ref[...].astype(BF16)

    o_ref[...] = _dot(a_ref[...], wb_ref[...]).astype(o_ref.dtype)


def _matmul(a, w, layer, *, n_out, col_off=0, tn, tm=MM_ROW_TILE, out_dtype, name):
    m, k = a.shape
    assert m % tm == 0 and n_out % tn == 0 and col_off % tn == 0
    off = col_off // tn
    return pl.pallas_call(
        _mm_kernel,
        grid=(n_out // tn, m // tm),
        in_specs=[pl.BlockSpec((tm, k), lambda j, i: (i, 0)),
                  pl.BlockSpec((None, k, tn), lambda j, i: (layer, 0, j + off))],
        out_specs=pl.BlockSpec((tm, tn), lambda j, i: (i, j)),
        out_shape=jax.ShapeDtypeStruct((m, n_out), out_dtype),
        scratch_shapes=[pltpu.VMEM((k, tn), BF16)],
        compiler_params=_cparams("arbitrary", "arbitrary"),
        name=name,
    )(a, w)


def _swiglu_kernel(a_ref, wg_ref, wu_ref, o_ref, wgb_ref, wub_ref):
    @pl.when(pl.program_id(1) == 0)
    def _():
        wgb_ref[...] = wg_ref[...].astype(BF16)
        wub_ref[...] = wu_ref[...].astype(BF16)

    a = a_ref[...]
    gate = _dot(a, wgb_ref[...])
    up = _dot(a, wub_ref[...])
    o_ref[...] = (jax.nn.silu(gate) * up).astype(o_ref.dtype)


def _swiglu_in(a, w, layer, *, tn=512, tm=MM_ROW_TILE):
    m, k = a.shape
    f = w.shape[2] // 2
    assert m % tm == 0 and f % tn == 0
    nb = f // tn
    return pl.pallas_call(
        _swiglu_kernel,
        grid=(nb, m // tm),
        in_specs=[pl.BlockSpec((tm, k), lambda j, i: (i, 0)),
                  pl.BlockSpec((None, k, tn), lambda j, i: (layer, 0, j)),
                  pl.BlockSpec((None, k, tn), lambda j, i: (layer, 0, j + nb))],
        out_specs=pl.BlockSpec((tm, tn), lambda j, i: (i, j)),
        out_shape=jax.ShapeDtypeStruct((m, f), BF16),
        scratch_shapes=[pltpu.VMEM((k, tn), BF16), pltpu.VMEM((k, tn), BF16)],
        compiler_params=_cparams("arbitrary", "arbitrary"),
        name="ffn_in_swiglu",
    )(a, w, w)


def _ffn(u, w_in, w_out, layer):
    bsz, n, d = u.shape
    a = _swiglu_in(u.reshape(bsz * n, d), w_in, layer)
    y = _matmul(a, w_out, layer, n_out=d, tn=512, tm=512, out_dtype=F32, name="ffn_out")
    return y.reshape(bsz, n, d)


def _ret_kernel(draw_ref, q_ref, k_ref, v_ref, g_ref, cos_ref, sin_ref, o_ref,
                qs_ref, ks_ref, oacc_ref, sf_ref, sb_ref, *, n_chunks, ctx_chunks, skip_chunks, dk):
    C = RET_CHUNK
    h = pl.program_id(1)
    half = dk // 2
    lg_f = -jnp.exp(jnp.full((1, 1), draw_ref[0, h], F32))
    lg_b = -jnp.exp(jnp.full((1, 1), draw_ref[1, h], F32))

    ri = lax.broadcasted_iota(jnp.int32, (C, C), 0)
    ci = lax.broadcasted_iota(jnp.int32, (C, C), 1)
    diff = (ri - ci).astype(F32)
    lower = diff >= 0
    dmat = jnp.where(lower, jnp.exp(lg_f * jnp.where(lower, diff, 0.0)),
                     jnp.exp(lg_b * jnp.where(lower, 0.0, -diff)))
    rc = lax.broadcasted_iota(jnp.int32, (C, 1), 0).astype(F32)
    qdec_f = jnp.exp(lg_f * (rc + 1.0))
    kdec_f = jnp.exp(lg_f * (C - 1.0 - rc))
    cdec_f = jnp.exp(lg_f * C)
    qdec_b = jnp.exp(lg_b * (C - rc))
    kdec_b = jnp.exp(lg_b * rc)
    cdec_b = jnp.exp(lg_b * C)

    sf_ref[...] = jnp.zeros_like(sf_ref)
    sb_ref[...] = jnp.zeros_like(sb_ref)
    qscale = dk ** -0.5

    def rope(x, cs, sn):
        x1, x2 = x[:, :half], x[:, half:]
        return jnp.concatenate([x1 * cs - x2 * sn, x1 * sn + x2 * cs], axis=-1)

    def fwd(c, carry):
        rows = pl.ds(pl.multiple_of(c * C, C), C)
        cs, sn = cos_ref[rows, :], sin_ref[rows, :]
        q = rope(q_ref[rows, :].astype(F32), cs, sn) * qscale
        k = rope(k_ref[rows, :].astype(F32), cs, sn)
        qb, kb = q.astype(BF16), k.astype(BF16)
        qs_ref[rows, :] = qb
        ks_ref[rows, :] = kb
        v = v_ref[rows, :]
        s = (_dot_nt(qb, kb) * dmat).astype(BF16)
        state = sf_ref[...]
        oacc_ref[rows, :] = _dot(s, v) + _dot(qb, state.astype(BF16)) * qdec_f
        kd_t = jnp.transpose(k * kdec_f).astype(BF16)
        sf_ref[...] = cdec_f * state + _dot(kd_t, v)
        return carry

    lax.fori_loop(0, n_chunks, fwd, 0, unroll=RET_UNROLL)

    def bwd(t, carry):
        c = jnp.where(t < ctx_chunks, ctx_chunks - 1 - t, n_chunks - 1 + ctx_chunks - t)
        rows = pl.ds(pl.multiple_of(c * C, C), C)
        qb, kb, v = qs_ref[rows, :], ks_ref[rows, :], v_ref[rows, :]
        state = sb_ref[...]
        o = oacc_ref[rows, :] + _dot(qb, state.astype(BF16)) * qdec_b
        kd_t = jnp.transpose(kb.astype(F32) * kdec_b).astype(BF16)
        sb_ref[...] = cdec_b * state + _dot(kd_t, v)

        def emit():
            mu = jnp.mean(o, axis=-1, keepdims=True)
            oc = o - mu
            var = jnp.mean(oc * oc, axis=-1, keepdims=True)
            on = oc * lax.rsqrt(var + LN_EPS)
            out_rows = pl.ds(pl.multiple_of((c - skip_chunks) * C, C), C)
            o_ref[out_rows, :] = (on * jax.nn.silu(g_ref[rows, :].astype(F32))).astype(o_ref.dtype)

        if skip_chunks:
            pl.when(c >= skip_chunks)(emit)
        else:
            emit()
        return carry

    lax.fori_loop(0, n_chunks, bwd, 0, unroll=RET_UNROLL)


def _retention(qkvg, decay_raw, cos, sin, *, n_ctx, need_ctx):
    bsz, n, width = qkvg.shape
    H = RET_HEADS
    dk = width // (6 * H)
    dv = 2 * dk
    C = RET_CHUNK
    assert n % C == 0 and n_ctx % C == 0
    skip = 0 if need_ctx else n_ctx // C
    n_out = n - skip * C
    kern = functools.partial(_ret_kernel, n_chunks=n // C, ctx_chunks=n_ctx // C, skip_chunks=skip, dk=dk)
    return pl.pallas_call(
        kern,
        grid=(bsz, H),
        in_specs=[
            pl.BlockSpec(memory_space=pltpu.SMEM),
            pl.BlockSpec((None, n, dk), lambda b, h: (b, 0, h)),
            pl.BlockSpec((None, n, dk), lambda b, h: (b, 0, H + h)),
            pl.BlockSpec((None, n, dv), lambda b, h: (b, 0, H + h)),
            pl.BlockSpec((None, n, dv), lambda b, h: (b, 0, 2 * H + h)),
            pl.BlockSpec((n, dk // 2), lambda b, h: (0, 0)),
            pl.BlockSpec((n, dk // 2), lambda b, h: (0, 0)),
        ],
        out_specs=pl.BlockSpec((None, n_out, dv), lambda b, h: (b, 0, h)),
        out_shape=jax.ShapeDtypeStruct((bsz, n_out, H * dv), BF16),
        scratch_shapes=[pltpu.VMEM((n, dk), BF16), pltpu.VMEM((n, dk), BF16), pltpu.VMEM((n, dv), F32),
                        pltpu.VMEM((dk, dv), F32), pltpu.VMEM((dk, dv), F32)],
        compiler_params=_cparams("arbitrary", "arbitrary"),
        name="retention",
    )(decay_raw.astype(F32), qkvg, qkvg, qkvg, qkvg, cos, sin)


def _rope_tables(n_ctx, n_lat, dim):
    rows = n_lat // GRID_W
    row = jnp.repeat(jnp.arange(rows, dtype=F32), GRID_W)
    col = jnp.tile(jnp.arange(GRID_W, dtype=F32), rows)
    n_freq = dim // 4
    inv_freq = ROPE_BASE ** (-jnp.arange(n_freq, dtype=F32) / n_freq)
    ang = jnp.concatenate([row[:, None] * inv_freq, col[:, None] * inv_freq], axis=-1)
    cos = jnp.concatenate([jnp.ones((n_ctx, dim // 2), F32), jnp.cos(ang)], axis=0)
    sin = jnp.concatenate([jnp.zeros((n_ctx, dim // 2), F32), jnp.sin(ang)], axis=0)
    return cos, sin


def _conv_kernel(x_ref, w_ref, o_ref, *, n_ctx, n_q_blocks, n_qk_blocks, qscale):
    j = pl.program_id(1)
    x = x_ref[...].astype(F32)
    n, tc = x.shape
    w = w_ref[...]
    pad = GDN_CONV // 2
    t = lax.broadcasted_iota(jnp.int32, (n, 1), 0)
    lo = jnp.where(t < n_ctx, 0, n_ctx)
    hi = jnp.where(t < n_ctx, n_ctx, n)
    acc = x * w[pad:pad + 1]
    for d in range(-pad, pad + 1):
        if d == 0:
            continue
        xs = pltpu.roll(x, (n - d) % n, 0)
        valid = (t + d >= lo) & (t + d < hi)
        acc = acc + jnp.where(valid, xs, 0.0) * w[d + pad:d + pad + 1]
    y = jax.nn.silu(acc)
    do_norm = j < n_qk_blocks
    scale = jnp.where(j < n_q_blocks, qscale, 1.0).astype(F32)
    for g0 in range(0, tc, GDN_DK):
        yg = y[:, g0:g0 + GDN_DK]
        ss = jnp.sum(yg * yg, axis=-1, keepdims=True)
        yn = yg * (lax.rsqrt(ss + 1e-6) * scale)
        o_ref[:, g0:g0 + GDN_DK] = jnp.where(do_norm, yn, yg).astype(o_ref.dtype)


def _gdn_conv(proj, conv_w, *, n_ctx, n_qk_heads, tc=512):
    bsz, n, _ = proj.shape
    width = conv_w.shape[1]
    qw = n_qk_heads * GDN_DK
    assert qw % tc == 0 and width % tc == 0
    kern = functools.partial(_conv_kernel, n_ctx=n_ctx, n_q_blocks=qw // tc, n_qk_blocks=2 * qw // tc,
                             qscale=GDN_DK ** -0.5)
    return pl.pallas_call(
        kern,
        grid=(bsz, width // tc),
        in_specs=[pl.BlockSpec((None, n, tc), lambda b, j: (b, 0, j)),
                  pl.BlockSpec((GDN_CONV, tc), lambda b, j: (0, j))],
        out_specs=pl.BlockSpec((None, n, tc), lambda b, j: (b, 0, j)),
        out_shape=jax.ShapeDtypeStruct((bsz, n, width), BF16),
        compiler_params=_cparams("arbitrary", "arbitrary"),
        name="gdn_conv",
    )(proj, conv_w)


def _gates_kernel(x_ref, alog_ref, dt_ref, col_ref, row_ref):
    C = GDN_CHUNK
    S = GDN_INST
    x = x_ref[...]
    n = x.shape[0]
    beta = jax.nn.sigmoid(x)
    z = x + dt_ref[...]
    softplus = jnp.maximum(z, 0.0) + jnp.log1p(jnp.exp(-jnp.abs(z)))
    la = -jnp.exp(alog_ref[...]) * softplus
    tm = lax.broadcasted_iota(jnp.int32, (n, 1), 0) % C
    pre = la
    suf = la
    s = 1
    while s < C:
        pre = pre + jnp.where(tm >= s, pltpu.roll(pre, s, 0), 0.0)
        suf = suf + jnp.where(tm < C - s, pltpu.roll(suf, n - s, 0), 0.0)
        s *= 2
    lane = lax.broadcasted_iota(jnp.int32, (1, x.shape[1]), 1) % (2 * S)
    is_fwd = (lane % S) < S // 2
    col_ref[...] = jnp.where(lane < S, beta, jnp.where(is_fwd, pre, suf))

    def xpose(p, carry):
        t = jnp.transpose(col_ref[pl.ds(pl.multiple_of(p * 2 * C, 2 * C), 2 * C), :])
        row_ref[2 * p] = t[:, :C]
        row_ref[2 * p + 1] = t[:, C:]
        return carry

    lax.fori_loop(0, n // (2 * C), xpose, 0)


def _gate_lane_order(n_qk_heads):
    idx = jnp.arange(8 * n_qk_heads).reshape(2, 2, n_qk_heads, 2)
    return idx.transpose(2, 0, 1, 3).reshape(-1)


def _gdn_gates(ba, a_log, dt_bias):
    bsz, n, w = ba.shape
    C = GDN_CHUNK
    hq = w // (2 * GDN_INST)

    def per_lane(p):
        p = p.astype(F32).reshape(2, hq, 2).transpose(1, 0, 2).reshape(hq, GDN_INST)
        return jnp.concatenate([jnp.zeros_like(p), p], axis=1).reshape(1, w)

    return pl.pallas_call(
        _gates_kernel,
        grid=(bsz,),
        in_specs=[pl.BlockSpec((None, n, w), lambda b: (b, 0, 0)),
                  pl.BlockSpec((1, w), lambda b: (0, 0)), pl.BlockSpec((1, w), lambda b: (0, 0))],
        out_specs=[pl.BlockSpec((None, n, w), lambda b: (b, 0, 0)),
                   pl.BlockSpec((None, n // C, w, C), lambda b: (b, 0, 0, 0))],
        out_shape=[jax.ShapeDtypeStruct((bsz, n, w), F32), jax.ShapeDtypeStruct((bsz, n // C, w, C), F32)],
        compiler_params=_cparams("arbitrary"),
        name="gdn_gates",
    )(ba, per_lane(a_log), per_lane(dt_bias))


GDN_INST = 4
GDN_STACK = GDN_INST * GDN_CHUNK
GDN_PREP_CHUNKS = 6
GDN_EMIT_ROWS = 256
GDN_HEADS_PER_STEP = 2


GDN_INV_BASE = 8
GDN_INV_LEVELS = (16, 32, 64)


def _merge_rows(g, lower):
    h = g // 2
    return [(r0 + h, r0 + g) if lower else (r0, r0 + h) for r0 in range(0, GDN_CHUNK, g)]


def _unit_tri_inverse(a_list, lower_list, same_base, off_masks, eye, between):
    def mm(x, y):
        return _dot(x.astype(BF16), y.astype(BF16))

    ad = [a * same_base for a in a_list]
    x = [eye - d for d in ad]
    p = [mm(d, d) for d in ad]
    between()
    xp = [mm(xi, pi) for xi, pi in zip(x, p)]
    p = [mm(pi, pi) for pi in p]
    between()
    x = [xi + xpi for xi, xpi in zip(x, xp)]
    x = [xi + mm(xi, pi) for xi, pi in zip(x, p)]
    between()
    for g, off in zip(GDN_INV_LEVELS, off_masks):
        h = g // 2
        sels = [_merge_rows(g, lower) for lower in lower_list]
        top = [jnp.concatenate([xi[a:b] for a, b in sel], axis=0) for xi, sel in zip(x, sels)]
        n = [mm(ti, a * off) for ti, a in zip(top, a_list)]
        between()
        upd = [ti - mm(ni, xi) for ti, ni, xi in zip(top, n, x)]
        between()
        merged = []
        for xi, ui, sel in zip(x, upd, sels):
            pieces = []
            for idx, (a, b) in enumerate(sel):
                new = ui[idx * h:(idx + 1) * h]
                r0 = (a // g) * g
                pieces += [xi[r0:r0 + h], new] if a != r0 else [new, xi[r0 + h:r0 + g]]
            merged.append(jnp.concatenate(pieces, axis=0))
        x = merged
    return x


def _gdn_kernel(q_ref, k_ref, v_ref, z_ref, gcol_ref, grow_ref, nw_ref, o_ref,
                mask_ref, gsel_ref, u0_ref, wq_ref, qk_ref, of_ref, ob_ref, m_ref,
                *, n_chunks, ctx_chunks, skip_chunks):
    C = GDN_CHUNK
    S = GDN_INST
    dk = GDN_DK
    dv = GDN_DV
    n_lvl = len(GDN_INV_LEVELS)
    M_INCL, M_STRICT, M_BASE, M_OFF, M_EYE = 0, 2, 4, 5, 5 + n_lvl

    r = lax.broadcasted_iota(jnp.int32, (C, C), 0)
    c = lax.broadcasted_iota(jnp.int32, (C, C), 1)
    mask_ref[M_INCL] = (r >= c).astype(F32)
    mask_ref[M_INCL + 1] = (c >= r).astype(F32)
    mask_ref[M_STRICT] = (r > c).astype(F32)
    mask_ref[M_STRICT + 1] = (c > r).astype(F32)
    mask_ref[M_BASE] = ((r // GDN_INV_BASE) == (c // GDN_INV_BASE)).astype(F32)
    for lvl, g in enumerate(GDN_INV_LEVELS):
        h = g // 2
        mask_ref[M_OFF + lvl] = (((r // g) == (c // g)) & ((r // h) != (c // h))).astype(F32)
    mask_ref[M_EYE] = (r == c).astype(F32)

    P = GDN_HEADS_PER_STEP

    def lanes(start):
        return pl.ds(pl.multiple_of(start, dk), dk)

    gate_lanes = gcol_ref.shape[-1]

    def head_roll(hh):
        first = (pl.program_id(1) * P + hh) * 2 * S
        return (gate_lanes - first) % gate_lanes

    def head_rows(hh):
        return pl.ds(pl.multiple_of(hh * 2 * S, 2 * S), 2 * S)

    def select_gates(i, carry):
        rows = pl.ds(pl.multiple_of(i * GDN_EMIT_ROWS, GDN_EMIT_ROWS), GDN_EMIT_ROWS)
        g = gcol_ref[rows, :]
        for hh in range(P):
            gsel_ref[hh, rows, :] = pltpu.roll(g, head_roll(hh), 1)
        return carry

    lax.fori_loop(0, n_chunks * C // GDN_EMIT_ROWS, select_gates, 0, unroll=3)

    NB = GDN_PREP_CHUNKS
    n_blocks = n_chunks // NB

    def scan_chunks(t):
        cb = jnp.where(t < ctx_chunks, ctx_chunks - 1 - t, n_chunks - 1 + ctx_chunks - t)
        return t, cb

    def prep_block(g, buf, between):
        hh, j = g // n_blocks, g % n_blocks
        between()
        kts = {}
        for i in range(0, NB, 2):
            for d in range(2):
                lo = scan_chunks(j * NB + i + d)[d]
                slab = pl.ds(pl.multiple_of(lo * C, 2 * C), 2 * C)
                kt2 = jnp.transpose(k_ref[slab, lanes(hh * dk)].astype(F32))
                kts[(i + d, d)], kts[(i + 1 - d, d)] = kt2[:, :C], kt2[:, C:]
        items = []
        for i in range(NB):
            for d, ci in enumerate(scan_chunks(j * NB + i)):
                rows = pl.ds(pl.multiple_of(ci * C, C), C)
                qb, kb = q_ref[rows, lanes(hh * dk)], k_ref[rows, lanes(hh * dk)]
                pk = _dot_nt(jnp.concatenate([kb, qb], axis=0), kb)
                kk, qk = pk[:C], pk[C:]
                g8 = gsel_ref[hh, rows, :]
                grow8 = grow_ref[ci, head_rows(hh), :]
                qf, kf = qb.astype(F32), kb.astype(F32)
                kt = kts[(i, d)]
                for vj in range(2):
                    s = 2 * d + vj
                    vf = v_ref[rows, lanes((2 * hh + vj) * dv)].astype(F32)
                    beta, gc = g8[:, s:s + 1], g8[:, S + s:S + s + 1]
                    g_row = grow8[S + s:S + s + 1, :]
                    e = jnp.exp(jnp.minimum(gc - g_row, 0.0))
                    a = beta * kk * (e * mask_ref[M_STRICT + d])
                    qkm = (qk * (e * mask_ref[M_INCL + d])).astype(BF16)
                    eg = jnp.exp(gc)
                    rhs = jnp.concatenate([vf * beta, kf * (beta * eg)], axis=-1)
                    g_last = g_row[:, 0:1] if d else g_row[:, C - 1:C]
                    kdt = (kt * jnp.exp(g_last - g_row)).astype(BF16)
                    items.append(dict(slot=i, s=s, a=a, qkm=qkm, rhs=rhs.astype(BF16), qd=qf * eg, kdt=kdt))
        between()

        xs = _unit_tri_inverse([it["a"] for it in items], [it["s"] < 2 for it in items], mask_ref[M_BASE],
                               [mask_ref[M_OFF + i] for i in range(n_lvl)], mask_ref[M_EYE], between)
        sols = [_dot(x.astype(BF16), it["rhs"]) for x, it in zip(xs, items)]
        between()
        for it, sol in zip(items, sols):
            s, slot = it["s"], it["slot"]
            u0_ref[buf, s, slot] = sol[:, :dv]
            wq_ref[buf, s, slot, 0:C, :] = sol[:, dv:].astype(BF16)
            wq_ref[buf, s, slot, C:2 * C, :] = it["qd"].astype(BF16)
            qk_ref[buf, s, slot, 0:C, :] = it["qkm"]
            qk_ref[buf, s, slot, C:C + dk, :] = it["kdt"]

    def scan_first(g, buf, i, st):
        j = g % n_blocks
        ms = [m_ref[s] for s in range(S)]
        if i == 0:
            ms = [jnp.where(j == 0, 0.0, m) for m in ms]
        st["ms"] = ms
        st["wm"] = [_dot(wq_ref[buf, s, i], ms[s].astype(BF16)) for s in range(S)]

    def scan_second(g, buf, i, st):
        hh, j = g // n_blocks, g % n_blocks
        cf, cb = scan_chunks(j * NB + i)
        cis = [cf, cf, cb, cb]
        ms, wm = st["ms"], st["wm"]
        ub = [(u0_ref[buf, s, i] - wm[s][:C]).astype(BF16) for s in range(S)]
        om = [_dot(qk_ref[buf, s, i], ub[s]) for s in range(S)]
        for s in range(S):
            ci = cis[s]
            g_s = grow_ref[ci, head_rows(hh), :][S + s:S + s + 1, :]
            g_last = g_s[:, C - 1:C] if s < 2 else g_s[:, 0:1]
            m_ref[s] = jnp.exp(g_last) * ms[s] + om[s][C:]
            o = wm[s][C:] + om[s][:C]
            rows = pl.ds(pl.multiple_of(ci * C, C), C)
            if s < 2:
                of_ref[hh, s, rows, :] = o
            else:
                ob_ref[hh, s - 2, rows, :] = o

    def scan_thunks(g, buf):
        thunks = []
        for i in range(NB):
            st = {}
            thunks += [functools.partial(scan_first, g, buf, i, st), functools.partial(scan_second, g, buf, i, st)]
        return thunks

    m_ref[...] = jnp.zeros_like(m_ref)
    prep_block(0, 0, lambda: None)

    def pipelined(g, carry):
        buf = g % 2
        pending = scan_thunks(g, buf)

        def between():
            if pending:
                pending.pop(0)()

        prep_block(g + 1, 1 - buf, between)
        while pending:
            pending.pop(0)()
        return carry

    nw = nw_ref[...]
    E = GDN_EMIT_ROWS
    emit_blocks = range(skip_chunks * C // E, n_chunks * C // E)

    def emit(hh, i):
        aligned = (lambda s: s) if isinstance(i, int) else (lambda s: pl.multiple_of(s, E))
        rows = pl.ds(aligned(i * E), E)
        out_rows = pl.ds(aligned(i * E - skip_chunks * C), E)
        for vj in range(2):
            cols = slice((2 * hh + vj) * dv, (2 * hh + vj + 1) * dv)
            o = of_ref[hh, vj, rows, :] + ob_ref[hh, vj, rows, :]
            zf = z_ref[rows, cols].astype(F32)
            on = o * lax.rsqrt(jnp.mean(o * o, axis=-1, keepdims=True) + 1e-6) * nw * jax.nn.silu(zf)
            o_ref[out_rows, cols] = on.astype(o_ref.dtype)

    last = P * n_blocks - 1
    lax.fori_loop(0, last, pipelined, 0)

    done = [functools.partial(emit, hh, i) for hh in range(P - 1) for i in emit_blocks]
    tail = scan_thunks(last, last % 2)
    for k, thunk in enumerate(tail):
        thunk()
        for _ in range(-(-len(done) // (len(tail) - k))):
            done.pop(0)()

    def emit_last(i, carry):
        emit(P - 1, i)
        return carry

    lax.fori_loop(emit_blocks.start, emit_blocks.stop, emit_last, 0)


def _gdn_scan(qkv, proj, gcol, grow, norm_w, *, n_ctx, need_ctx, n_qk_heads, z_col_off):
    bsz, n, _ = qkv.shape
    C = GDN_CHUNK
    P = GDN_HEADS_PER_STEP
    HQ = n_qk_heads
    HV = 2 * HQ
    gw = gcol.shape[-1]
    assert gw == HQ * 2 * GDN_INST and HQ % P == 0 and n % (GDN_PREP_CHUNKS * C) == 0 and n_ctx % C == 0
    assert n % GDN_EMIT_ROWS == 0 and n_ctx % GDN_EMIT_ROWS == 0
    assert (n // C) % 2 == 0 and (n_ctx // C) % 2 == 0 and GDN_PREP_CHUNKS % 2 == 0
    nch = n // C
    skip = 0 if need_ctx else n_ctx // C
    n_out = n - skip * C
    qk_w = P * GDN_DK
    v_w = P * 2 * GDN_DV
    assert z_col_off % v_w == 0
    k_blk, v_blk, z_blk = HQ * GDN_DK // qk_w, 2 * HQ * GDN_DK // v_w, z_col_off // v_w
    kern = functools.partial(_gdn_kernel, n_chunks=nch, ctx_chunks=n_ctx // C, skip_chunks=skip)
    return pl.pallas_call(
        kern,
        grid=(bsz, HQ // P),
        in_specs=[
            pl.BlockSpec((None, n, qk_w), lambda b, h: (b, 0, h)),
            pl.BlockSpec((None, n, qk_w), lambda b, h: (b, 0, k_blk + h)),
            pl.BlockSpec((None, n, v_w), lambda b, h: (b, 0, v_blk + h)),
            pl.BlockSpec((None, n, v_w), lambda b, h: (b, 0, z_blk + h)),
            pl.BlockSpec((None, n, gw), lambda b, h: (b, 0, 0)),
            pl.BlockSpec((None, nch, P * 2 * GDN_INST, C), lambda b, h: (b, 0, h, 0)),
            pl.BlockSpec((1, GDN_DV), lambda b, h: (0, 0)),
        ],
        out_specs=pl.BlockSpec((None, n_out, v_w), lambda b, h: (b, 0, h)),
        out_shape=jax.ShapeDtypeStruct((bsz, n_out, HV * GDN_DV), BF16),
        scratch_shapes=[pltpu.VMEM((6 + len(GDN_INV_LEVELS), C, C), F32),
                        pltpu.VMEM((P, n, gw), F32),
                        pltpu.VMEM((2, GDN_INST, GDN_PREP_CHUNKS, C, GDN_DV), F32),
                        pltpu.VMEM((2, GDN_INST, GDN_PREP_CHUNKS, 2 * C, GDN_DK), BF16),
                        pltpu.VMEM((2, GDN_INST, GDN_PREP_CHUNKS, C + GDN_DK, C), BF16),
                        pltpu.VMEM((P, 2, n, GDN_DV), F32), pltpu.VMEM((P, 2, n, GDN_DV), F32),
                        pltpu.VMEM((GDN_INST, GDN_DK, GDN_DV), F32)],
        compiler_params=_cparams("arbitrary", "arbitrary"),
        name="gdn_scan",
    )(qkv, qkv, qkv, proj, gcol, grow, norm_w.astype(F32).reshape(1, GDN_DV))


def kernel(x, c, ctx, c_ctx, ada_w, ada_b, ln_g, ln_b, ret_w_in, ret_decay, ret_w_out,
           gdn_w_in, gdn_conv, gdn_a_log, gdn_dt_bias, gdn_norm, gdn_w_out, ffn_w_in, ffn_w_out):
    bsz, n_lat, d = x.shape
    n_ctx = ctx.shape[1]
    depth = ada_w.shape[0]
    n_tot = n_ctx + n_lat
    alpha = (2 * depth) ** 0.25

    pad_rows = (-(bsz + 1)) % 16
    cc = jnp.concatenate([c, c_ctx[None], jnp.zeros((pad_rows, d), F32)], axis=0)
    mods = _ada_mod(cc, ada_w, ada_b)

    def layer_params(i):
        lat = mods[i, :bsz].reshape(bsz, 1, 6, d)
        cx = jnp.broadcast_to(mods[i, bsz].reshape(1, 1, 6, d), (bsz, 1, 6, d))
        return jnp.concatenate([cx, lat], axis=1)

    params = [layer_params(i) for i in range(depth)]
    h = (ctx, x)
    u = _modulate(h, params[0], shift=0, scale=1, n_ctx=n_ctx, n=n_tot)
    ctx_rows = n_ctx

    for i in range(depth):
        need_ctx = i < depth - 1
        j = i // 2
        u2 = u.reshape(bsz * n_tot, d)
        if i % 2 == 0:
            qkvg = _matmul(u2, ret_w_in, j, n_out=ret_w_in.shape[-1], tn=1024, tm=_row_tile(bsz * n_tot),
                           out_dtype=BF16, name="ret_in_proj").reshape(bsz, n_tot, -1)
            dk = qkvg.shape[-1] // (6 * RET_HEADS)
            cos, sin = _rope_tables(n_ctx, n_lat, dk)
            o = _retention(qkvg, ret_decay[j], cos, sin, n_ctx=n_ctx, need_ctx=need_ctx)
            w_out = ret_w_out
        else:
            qkv_w = gdn_conv.shape[-1]
            hv = gdn_a_log.shape[-1]
            main_w = qkv_w + hv * GDN_DV
            proj = _matmul(u2, gdn_w_in, j, n_out=main_w, tn=1024, tm=_row_tile(bsz * n_tot), out_dtype=BF16,
                           name="gdn_in_proj").reshape(bsz, n_tot, main_w)
            w_gate = jnp.take(gdn_w_in[j, :, main_w:], _gate_lane_order(hv // 2), axis=1)[None]
            ba = _matmul(u2, w_gate, 0, n_out=4 * hv, tn=4 * hv, out_dtype=F32,
                         name="gdn_gate_proj").reshape(bsz, n_tot, 4 * hv)
            gcol, grow = _gdn_gates(ba, gdn_a_log[j], gdn_dt_bias[j])
            qkv = _gdn_conv(proj, gdn_conv[j], n_ctx=n_ctx, n_qk_heads=hv // 2)
            o = _gdn_scan(qkv, proj, gcol, grow, gdn_norm[j], n_ctx=n_ctx, need_ctx=need_ctx,
                          n_qk_heads=hv // 2, z_col_off=qkv_w)
            w_out = gdn_w_out

        n_rows = o.shape[1]
        out_ctx = n_ctx if need_ctx else 0
        y = _matmul(o.reshape(bsz * n_rows, -1), w_out, j, n_out=d, tn=512, out_dtype=F32,
                    name="mixer_out_proj").reshape(bsz, n_rows, d)
        h_off = ctx_rows - out_ctx
        if isinstance(h, tuple) and h_off:
            h, h_off = h[1], 0
        h, u = _resid_ln(h, y, params[i], ln_g[i, 0], ln_b[i, 0], params[i], alpha=alpha, gate=2,
                         shift=3, scale=4, n_ctx=out_ctx, h_row_offset=h_off)
        ctx_rows = out_ctx
        y = _ffn(u, ffn_w_in, ffn_w_out, i)
        if need_ctx:
            h, u = _resid_ln(h, y, params[i], ln_g[i, 1], ln_b[i, 1], params[i + 1], alpha=alpha, gate=5,
                             shift=0, scale=1, n_ctx=out_ctx)
        else:
            h = _resid_ln(h, y, params[i], ln_g[i, 1], ln_b[i, 1], params[i], alpha=alpha, gate=5,
                          shift=0, scale=1, n_ctx=out_ctx, with_u=False)
    return h
```
